```python
import jax, jax.numpy as jnp
from jax import lax
import numpy as np

D_MODEL = 2048
BATCH = 4
SEQ = 2048
DEPTH = 2

GRID_W = 64
CTX_LEN = 256
N_HEADS = 16
N_KV_HEADS = 4
GROUP = N_HEADS // N_KV_HEADS
HEAD_DIM = 128
WINDOW = 128
BLOCK = 128
ROPE_THETA = 10000.0
CONV_WIDTH = 3
D_CONV = D_MODEL
D_FF = 5632
N_EXPERTS = 8
TOP_K = 2
D_FF_EXPERT = 7168
RMS_EPS = 1e-6
NEG_INF = -1e30
N_MOD = 6
N_DENSE = (DEPTH + 1) // 2
N_MOE = DEPTH // 2
Q_W = N_HEADS * HEAD_DIM
KV_W = N_KV_HEADS * HEAD_DIM
SPLIT_POINTS = (Q_W, Q_W + KV_W, Q_W + 2 * KV_W, Q_W + 2 * KV_W + D_CONV, Q_W + 2 * KV_W + 2 * D_CONV, Q_W + 2 * KV_W + 3 * D_CONV, Q_W + 2 * KV_W + 3 * D_CONV + D_MODEL)
P_WIDTH = Q_W + 2 * KV_W + 3 * D_CONV + 2 * D_MODEL

kernel_name = "hybrid_gated_swa_shortconv_moe_dit"


def rmsnorm(t, g):
    tf = t.astype(jnp.float32)
    tf = tf * lax.rsqrt(jnp.mean(tf * tf, axis=-1, keepdims=True) + RMS_EPS)
    return tf.astype(t.dtype) * g


def rope_1d(t, pos):
    half = t.shape[-1] // 2
    inv_freq = jnp.power(ROPE_THETA, -jnp.arange(half, dtype=jnp.float32) / half)
    ang = pos.astype(jnp.float32)[:, None] * inv_freq[None, :]
    cos = jnp.cos(ang)[:, None, :]
    sin = jnp.sin(ang)[:, None, :]
    t1 = t[..., :half].astype(jnp.float32)
    t2 = t[..., half:].astype(jnp.float32)
    return jnp.concatenate([t1 * cos - t2 * sin, t2 * cos + t1 * sin], axis=-1).astype(t.dtype)


def rope_2d(t, rows, cols):
    half = t.shape[-1] // 2
    return jnp.concatenate([rope_1d(t[..., :half], rows), rope_1d(t[..., half:], cols)], axis=-1)


def latent_attention(q, k, v, kc, vc, sink):
    b, s = q.shape[0], q.shape[1]
    nb = s // BLOCK
    scale = HEAD_DIM ** -0.5
    pad = ((0, 0), (BLOCK, BLOCK), (0, 0), (0, 0))
    kp = jnp.pad(k, pad).reshape(b, nb + 2, BLOCK, N_KV_HEADS, HEAD_DIM)
    vp = jnp.pad(v, pad).reshape(b, nb + 2, BLOCK, N_KV_HEADS, HEAD_DIM)
    kband = jnp.concatenate([kp[:, :-2], kp[:, 1:-1], kp[:, 2:]], axis=2)
    vband = jnp.concatenate([vp[:, :-2], vp[:, 1:-1], vp[:, 2:]], axis=2)
    qb = q.reshape(b, nb, BLOCK, N_KV_HEADS, GROUP, HEAD_DIM)
    s_loc = jnp.einsum('bnqhgd,bnkhd->bhgnqk', qb, kband).astype(jnp.float32) * scale
    blk = jnp.arange(nb)[:, None, None] * BLOCK
    qpos = blk + jnp.arange(BLOCK)[None, :, None]
    kpos = blk - BLOCK + jnp.arange(3 * BLOCK)[None, None, :]
    valid = (jnp.abs(qpos - kpos) <= WINDOW) & (kpos >= 0) & (kpos < s)
    s_loc = jnp.where(valid, s_loc, NEG_INF)
    s_ctx = jnp.einsum('bnqhgd,bchd->bhgnqc', qb, kc).astype(jnp.float32) * scale
    sink_col = jnp.broadcast_to(sink.astype(jnp.float32).reshape(N_KV_HEADS, GROUP)[None, :, :, None, None, None], s_loc.shape[:-1] + (1,))
    probs = jax.nn.softmax(jnp.concatenate([s_loc, s_ctx, sink_col], axis=-1), axis=-1).astype(v.dtype)
    n_ctx = kc.shape[1]
    p_loc = probs[..., :3 * BLOCK]
    p_ctx = probs[..., 3 * BLOCK:3 * BLOCK + n_ctx]
    o = jnp.einsum('bhgnqk,bnkhd->bnqhgd', p_loc, vband) + jnp.einsum('bhgnqc,bchd->bnqhgd', p_ctx, vc)
    return o.reshape(b, s, N_HEADS * HEAD_DIM)


def context_attention(qc, kc, vc, sink):
    b, n = qc.shape[0], qc.shape[1]
    scale = HEAD_DIM ** -0.5
    qg = qc.reshape(b, n, N_KV_HEADS, GROUP, HEAD_DIM)
    sc = jnp.einsum('bqhgd,bkhd->bhgqk', qg, kc).astype(jnp.float32) * scale
    sink_col = jnp.broadcast_to(sink.astype(jnp.float32).reshape(N_KV_HEADS, GROUP)[None, :, :, None, None], sc.shape[:-1] + (1,))
    probs = jax.nn.softmax(jnp.concatenate([sc, sink_col], axis=-1), axis=-1).astype(vc.dtype)
    o = jnp.einsum('bhgqk,bkhd->bqhgd', probs[..., :n], vc)
    return o.reshape(b, n, N_HEADS * HEAD_DIM)


def short_conv(u, gate_b, gate_c, w):
    z = gate_c * u
    n = z.shape[1]
    zp = jnp.pad(z, ((0, 0), (CONV_WIDTH // 2, CONV_WIDTH // 2), (0, 0)))
    y = zp[:, 0:n] * w[0]
    for i in range(1, CONV_WIDTH):
        y = y + zp[:, i:i + n] * w[i]
    return gate_b * y


def merge_branches(y_attn, y_conv, g_attn, g_conv, w_oa, w_oc, w_o):
    return (jax.nn.sigmoid(g_attn) * (y_attn @ w_oa) + jax.nn.sigmoid(g_conv) * (y_conv @ w_oc)) @ w_o


def swiglu(t, w1, w3, w2):
    return (jax.nn.silu(t @ w1) * (t @ w3)) @ w2


def moe_swiglu(t, w_router, w1, w3, w2):
    shp = t.shape
    tf = t.reshape(-1, shp[-1])
    logits = (tf @ w_router).astype(jnp.float32)
    top_vals, top_idx = lax.top_k(logits, TOP_K)
    top_p = jax.nn.softmax(top_vals, axis=-1)
    combine = jnp.sum(jax.nn.one_hot(top_idx, N_EXPERTS, dtype=jnp.float32) * top_p[..., None], axis=1).astype(t.dtype)
    out = jnp.zeros_like(tf)
    for e in range(N_EXPERTS):
        out = out + combine[:, e:e + 1] * swiglu(tf, w1[e], w3[e], w2[e])
    return out.reshape(shp)


def setup_inputs(seed: int = 0) -> dict:
    key = jax.random.key(seed)
    ks = jax.random.split(key, 24)
    f32 = jnp.float32

    def dense(k, shape, fan_in, gain=1.0):
        return jax.random.normal(k, shape, f32) * (gain * fan_in ** -0.5)

    return {
        "x": jax.random.normal(ks[0], (BATCH, SEQ, D_MODEL), f32),
        "c": jax.random.normal(ks[1], (BATCH, D_MODEL), f32),
        "ctx": jax.random.normal(ks[2], (BATCH, CTX_LEN, D_MODEL), f32),
        "c_ctx": jax.random.normal(ks[3], (D_MODEL,), f32),
        "w_mod": dense(ks[4], (DEPTH, D_MODEL, N_MOD * D_MODEL), D_MODEL, 0.5),
        "b_mod": 0.02 * jax.random.normal(ks[5], (DEPTH, N_MOD * D_MODEL), f32),
        "norm1": 1.0 + 0.02 * jax.random.normal(ks[6], (DEPTH, D_MODEL), f32),
        "w_in": dense(ks[7], (DEPTH, D_MODEL, P_WIDTH), D_MODEL),
        "sink": 0.5 * jax.random.normal(ks[8], (DEPTH, N_HEADS), f32),
        "conv_w": dense(ks[9], (DEPTH, CONV_WIDTH, D_CONV), CONV_WIDTH),
        "w_o_attn": dense(ks[10], (DEPTH, Q_W, D_MODEL), Q_W),
        "w_o_conv": dense(ks[11], (DEPTH, D_CONV, D_MODEL), D_CONV),
        "w_out": dense(ks[12], (DEPTH, D_MODEL, D_MODEL), D_MODEL),
        "norm2": 1.0 + 0.02 * jax.random.normal(ks[13], (DEPTH, D_MODEL), f32),
        "ffn_w1": dense(ks[14], (N_DENSE, D_MODEL, D_FF), D_MODEL),
        "ffn_w3": dense(ks[15], (N_DENSE, D_MODEL, D_FF), D_MODEL),
        "ffn_w2": dense(ks[16], (N_DENSE, D_FF, D_MODEL), D_FF),
        "router": dense(ks[17], (N_MOE, D_MODEL, N_EXPERTS), D_MODEL),
        "moe_w1": dense(ks[18], (N_MOE, N_EXPERTS, D_MODEL, D_FF_EXPERT), D_MODEL),
        "moe_w3": dense(ks[19], (N_MOE, N_EXPERTS, D_MODEL, D_FF_EXPERT), D_MODEL),
        "moe_w2": dense(ks[20], (N_MOE, N_EXPERTS, D_FF_EXPERT, D_MODEL), D_FF_EXPERT),
        "norm_f": 1.0 + 0.02 * jax.random.normal(ks[21], (D_MODEL,), f32),
    }


def reference(x, c, ctx, c_ctx, w_mod, b_mod, norm1, w_in, sink, conv_w, w_o_attn, w_o_conv, w_out, norm2, ffn_w1, ffn_w3, ffn_w2, router, moe_w1, moe_w3, moe_w2, norm_f):
    b, s, _ = x.shape
    n_ctx = ctx.shape[1]
    n_rows = s // GRID_W
    rows = jnp.broadcast_to(jnp.arange(n_rows)[:, None], (n_rows, GRID_W)).reshape(s)
    cols = jnp.broadcast_to(jnp.arange(GRID_W)[None, :], (n_rows, GRID_W)).reshape(s)
    c_act = jax.nn.silu(c)
    cc_act = jax.nn.silu(c_ctx)
    xc = ctx
    for layer in range(DEPTH):
        ctx_out = layer < DEPTH - 1
        mod = c_act @ w_mod[layer] + b_mod[layer]
        mod_c = cc_act @ w_mod[layer] + b_mod[layer]
        sh1, sc1, g1, sh2, sc2, g2 = jnp.split(mod, N_MOD, axis=-1)
        csh1, csc1, cg1, csh2, csc2, cg2 = jnp.split(mod_c, N_MOD, axis=-1)

        h = rmsnorm(x, norm1[layer]) * (1.0 + sc1[:, None]) + sh1[:, None]
        hc = rmsnorm(xc, norm1[layer]) * (1.0 + csc1) + csh1
        q, k, v, u, gate_b, gate_c, ga, gc = jnp.split(h @ w_in[layer], list(SPLIT_POINTS), axis=-1)
        q = rope_2d(q.reshape(b, s, N_HEADS, HEAD_DIM), rows, cols)
        k = rope_2d(k.reshape(b, s, N_KV_HEADS, HEAD_DIM), rows, cols)
        v = v.reshape(b, s, N_KV_HEADS, HEAD_DIM)
        if ctx_out:
            qc, kc, vc, uc, gate_bc, gate_cc, gac, gcc = jnp.split(hc @ w_in[layer], list(SPLIT_POINTS), axis=-1)
        else:
            kc, vc = jnp.split(hc @ w_in[layer][:, Q_W:Q_W + 2 * KV_W], 2, axis=-1)
        kc = kc.reshape(b, n_ctx, N_KV_HEADS, HEAD_DIM)
        vc = vc.reshape(b, n_ctx, N_KV_HEADS, HEAD_DIM)

        y_attn = latent_attention(q, k, v, kc, vc, sink[layer])
        y_conv = short_conv(u, gate_b, gate_c, conv_w[layer])
        x = x + g1[:, None] * merge_branches(y_attn, y_conv, ga, gc, w_o_attn[layer], w_o_conv[layer], w_out[layer])
        if ctx_out:
            yc_attn = context_attention(qc.reshape(b, n_ctx, N_HEADS, HEAD_DIM), kc, vc, sink[layer])
            yc_conv = short_conv(uc, gate_bc, gate_cc, conv_w[layer])
            xc = xc + cg1 * merge_branches(yc_attn, yc_conv, gac, gcc, w_o_attn[layer], w_o_conv[layer], w_out[layer])

        h = rmsnorm(x, norm2[layer]) * (1.0 + sc2[:, None]) + sh2[:, None]
        if layer % 2 == 0:
            i = layer // 2
            x = x + g2[:, None] * swiglu(h, ffn_w1[i], ffn_w3[i], ffn_w2[i])
            if ctx_out:
                hc = rmsnorm(xc, norm2[layer]) * (1.0 + csc2) + csh2
                xc = xc + cg2 * swiglu(hc, ffn_w1[i], ffn_w3[i], ffn_w2[i])
        else:
            i = layer // 2
            x = x + g2[:, None] * moe_swiglu(h, router[i], moe_w1[i], moe_w3[i], moe_w2[i])
            if ctx_out:
                hc = rmsnorm(xc, norm2[layer]) * (1.0 + csc2) + csh2
                xc = xc + cg2 * moe_swiglu(hc, router[i], moe_w1[i], moe_w3[i], moe_w2[i])
    return rmsnorm(x, norm_f)
```

```python
import functools

import numpy as np
import jax
import jax.numpy as jnp
from jax import lax
from jax.experimental import pallas as pl
from jax.experimental.pallas import tpu as pltpu

D_MODEL = 2048
BATCH = 4
SEQ = 2048
DEPTH = 2
GRID_W = 64
CTX_LEN = 256
N_HEADS = 16
N_KV_HEADS = 4
GROUP = N_HEADS // N_KV_HEADS
HEAD_DIM = 128
WINDOW = 128
BLOCK = 128
ROPE_THETA = 10000.0
D_FF = 5632
N_EXPERTS = 8
TOP_K = 2
D_FF_EXPERT = 7168
RMS_EPS = 1e-6
NEG_INF = -1e30
N_MOD = 6
Q_W = N_HEADS * HEAD_DIM
KV_W = N_KV_HEADS * HEAD_DIM
QKV_W = Q_W + 2 * KV_W
REST_W = 3 * D_MODEL + 2 * D_MODEL

N_LAT = BATCH * SEQ
N_CTX = BATCH * CTX_LEN
N_ALL = N_LAT + N_CTX
CTX_MOD_ROW = BATCH
MOD_ROWS = 8

LANES = 128
VMEM_LIMIT = 56 * 1024 * 1024
ROW_TILE = 1024
MOE_TILE = 512
MOE_ROWS = TOP_K * N_LAT + N_EXPERTS * MOE_TILE
CAST_ROWS = 256

F32 = jnp.float32
BF16 = jnp.bfloat16


def _params(n_axes):
    return pltpu.CompilerParams(
        dimension_semantics=("arbitrary",) * n_axes, vmem_limit_bytes=VMEM_LIMIT)


def _sigmoid(v):
    return 1.0 / (1.0 + jnp.exp(-v))


def _mod_row(m_tile_start_row):
    return jnp.minimum(m_tile_start_row // SEQ, CTX_MOD_ROW)


def _modulation_kernel(c_ref, w_ref, b_ref, o_ref):
    cv = c_ref[...]
    act = (cv * _sigmoid(cv)).astype(BF16)
    o_ref[...] = jnp.dot(act, w_ref[...].astype(BF16), preferred_element_type=F32) + b_ref[...]


def _modulation(cvec, w_mod, b_mod):
    tn = 1024
    n_tiles = N_MOD * D_MODEL // tn
    return pl.pallas_call(
        _modulation_kernel,
        grid=(DEPTH, n_tiles),
        in_specs=[
            pl.BlockSpec((MOD_ROWS, D_MODEL), lambda l, n: (0, 0)),
            pl.BlockSpec((None, D_MODEL, tn), lambda l, n: (l, 0, n)),
            pl.BlockSpec((None, 1, tn), lambda l, n: (l, 0, n)),
        ],
        out_specs=pl.BlockSpec((None, MOD_ROWS, tn), lambda l, n: (l, 0, n)),
        out_shape=jax.ShapeDtypeStruct((DEPTH, MOD_ROWS, N_MOD * D_MODEL), F32),
        compiler_params=_params(2),
        name="modulation",
    )(cvec, w_mod, b_mod.reshape(DEPTH, 1, N_MOD * D_MODEL))


def _norm_kernel(x_ref, g_ref, sh_ref, sc_ref, o_ref, *, tm):
    row = _mod_row(pl.program_id(0) * tm)
    xf = x_ref[...]
    inv = lax.rsqrt(jnp.mean(xf * xf, axis=-1, keepdims=True) + RMS_EPS)
    y = (xf * inv) * g_ref[...]
    y = y * (1.0 + sc_ref[pl.ds(row, 1), :]) + sh_ref[pl.ds(row, 1), :]
    o_ref[...] = y.astype(o_ref.dtype)


def _norm_mod(x_rows, gain, mod, layer, shift_chunk, scale_chunk, n_rows, out_dtype):
    tm = 512
    return pl.pallas_call(
        functools.partial(_norm_kernel, tm=tm),
        grid=(n_rows // tm,),
        in_specs=[
            pl.BlockSpec((tm, D_MODEL), lambda m: (m, 0)),
            pl.BlockSpec((None, 1, D_MODEL), lambda m: (layer, 0, 0)),
            pl.BlockSpec((None, MOD_ROWS, D_MODEL), lambda m: (layer, 0, shift_chunk)),
            pl.BlockSpec((None, MOD_ROWS, D_MODEL), lambda m: (layer, 0, scale_chunk)),
        ],
        out_specs=pl.BlockSpec((tm, D_MODEL), lambda m: (m, 0)),
        out_shape=jax.ShapeDtypeStruct((n_rows, D_MODEL), out_dtype),
        compiler_params=_params(1),
        name="norm_mod",
    )(x_rows, gain.reshape(DEPTH, 1, D_MODEL), mod, mod)


def _ws_kernel(group_ref, nused_ref, *refs, n_lhs, n_w, n_extra, pairs, epilogue, k_dim):
    lhs_refs = refs[:n_lhs]
    w_refs = refs[n_lhs:n_lhs + n_w]
    ex_refs = refs[n_lhs + n_w:n_lhs + n_w + n_extra]
    out_ref = refs[n_lhs + n_w + n_extra]
    wbf_refs = refs[n_lhs + n_w + n_extra + 1:]
    n = pl.program_id(0)
    m = pl.program_id(1)
    changed = (m == 0) | (group_ref[m] != group_ref[jnp.maximum(m - 1, 0)])

    @pl.when(changed)
    def _():
        def cast_step(i, carry):
            rows = pl.ds(pl.multiple_of(i * CAST_ROWS, CAST_ROWS), CAST_ROWS)
            for w_ref, wbf in zip(w_refs, wbf_refs):
                wbf[rows, :] = w_ref[rows, :].astype(BF16)
            return carry
        lax.fori_loop(0, k_dim // CAST_ROWS, cast_step, 0)

    @pl.when(m < nused_ref[0])
    def _():
        dots = [jnp.dot(lhs_refs[li][...], wbf_refs[wi][...], preferred_element_type=F32)
                for li, wi in pairs]
        out_ref[...] = epilogue(dots, ex_refs, n, m).astype(out_ref.dtype)

    @pl.when(m >= nused_ref[0])
    def _():
        out_ref[...] = jnp.zeros_like(out_ref)


def _ws_matmul(lhs, weights, *, pairs, epilogue, extras=(), group, n_used, m_tiles, tm, tn,
               n_out, col_off=0, lhs_row_off=0, out_dtype, name):
    k_dim = lhs[0].shape[1]
    col_blk = col_off
    row_blk = lhs_row_off

    def wrap(fn):
        return lambda n, m, g, u: fn(n, m)

    in_specs = [pl.BlockSpec((tm, k_dim), lambda n, m, g, u: (m + row_blk, 0)) for _ in lhs]
    in_specs += [pl.BlockSpec((None, k_dim, tn), lambda n, m, g, u: (g[m], 0, n + col_blk))
                 for _ in weights]
    in_specs += [pl.BlockSpec(shape, wrap(fn)) for _, shape, fn in extras]
    grid_spec = pltpu.PrefetchScalarGridSpec(
        num_scalar_prefetch=2,
        grid=(n_out // tn, m_tiles),
        in_specs=in_specs,
        out_specs=pl.BlockSpec((tm, tn), lambda n, m, g, u: (m, n)),
        scratch_shapes=[pltpu.VMEM((k_dim, tn), BF16) for _ in weights],
    )
    kern = functools.partial(
        _ws_kernel, n_lhs=len(lhs), n_w=len(weights), n_extra=len(extras), pairs=pairs,
        epilogue=epilogue, k_dim=k_dim)
    return pl.pallas_call(
        kern,
        grid_spec=grid_spec,
        out_shape=jax.ShapeDtypeStruct((m_tiles * tm, n_out), out_dtype),
        compiler_params=_params(2),
        name=name,
    )(group, n_used, *lhs, *weights, *[e[0] for e in extras])


def _layer_group(layer, m_tiles):
    return jnp.full((m_tiles,), layer, jnp.int32), jnp.full((1,), m_tiles, jnp.int32)


def _ep_plain(dots, ex, n, m):
    return dots[0]


def _ep_rope(dots, ex, n, m):
    cos_ref, sa_ref, sb_ref = ex
    d = dots[0]
    cos, sa, sb = cos_ref[...], sa_ref[...], sb_ref[...]
    quarter = HEAD_DIM // 4
    heads = []
    for h in range(d.shape[1] // HEAD_DIM):
        t = d[:, h * HEAD_DIM:(h + 1) * HEAD_DIM]
        heads.append(t * cos + pltpu.roll(t, HEAD_DIM - quarter, 1) * sa
                     + pltpu.roll(t, quarter, 1) * sb)
    return jnp.concatenate(heads, axis=1)


def _ep_merge(dots, ex, n, m):
    ga_ref, gc_ref = ex
    return (_sigmoid(ga_ref[...].astype(F32)) * dots[0]
            + _sigmoid(gc_ref[...].astype(F32)) * dots[1])


def _ep_residual(dots, ex, n, m, *, tm, row_off):
    x_ref, gate_ref = ex
    row = _mod_row((m + row_off) * tm)
    return x_ref[...] + gate_ref[pl.ds(row, 1), :] * dots[0]


def _ep_swiglu(dots, ex, n, m):
    a = dots[0]
    return (a * _sigmoid(a)) * dots[1]


def _rope_tables(tm):
    quarter = HEAD_DIM // 4
    inv_freq = np.power(np.float32(ROPE_THETA),
                        -np.arange(quarter, dtype=np.float32) / np.float32(quarter)).astype(np.float32)
    pos = np.arange(SEQ)
    rows = (pos // GRID_W).astype(np.float32)
    cols = (pos % GRID_W).astype(np.float32)
    ang = np.concatenate([rows[:, None] * inv_freq[None, :]] * 2
                         + [cols[:, None] * inv_freq[None, :]] * 2, axis=1).astype(np.float32)
    cos = np.cos(ang).astype(np.float32)
    sin = np.sin(ang).astype(np.float32)
    lane = np.arange(HEAD_DIM)
    first = ((lane // quarter) % 2 == 0)[None, :]
    sa = np.where(first, -sin, 0.0).astype(np.float32)
    sb = np.where(first, 0.0, sin).astype(np.float32)
    ones = np.ones((tm, HEAD_DIM), np.float32)
    zeros = np.zeros((tm, HEAD_DIM), np.float32)
    scale = np.float32(HEAD_DIM ** -0.5)

    def variants(tbl, ident):
        k = np.concatenate([tbl, ident], axis=0)
        ident_all = np.concatenate([np.broadcast_to(ident[:1], tbl.shape), ident], axis=0)
        return np.stack([k * scale, k, ident_all])

    return (jnp.asarray(variants(cos, ones)), jnp.asarray(variants(sa, zeros)),
            jnp.asarray(variants(sb, zeros)))


def _attn_kernel(sink_ref, q_ref, kp_ref, kc_ref, kn_ref, vp_ref, vc_ref, vn_ref, kx_ref, vx_ref,
                 o_ref, *, n_lat_blocks):
    g = pl.program_id(1)
    i = pl.program_id(2)
    q = q_ref[...]
    qs = jnp.concatenate([q[:, h * HEAD_DIM:(h + 1) * HEAD_DIM] for h in range(GROUP)], axis=0)
    keys = jnp.concatenate([kp_ref[...], kc_ref[...], kn_ref[...], kx_ref[...]], axis=0)
    vals = jnp.concatenate([vp_ref[...], vc_ref[...], vn_ref[...], vx_ref[...]], axis=0)
    s = lax.dot_general(qs, keys, (((1,), (1,)), ((), ())), preferred_element_type=F32)
    n_loc = 3 * BLOCK
    rows = lax.broadcasted_iota(jnp.int32, s.shape, 0) & (BLOCK - 1)
    cols = lax.broadcasted_iota(jnp.int32, s.shape, 1)
    delta = cols - rows
    col_min = jnp.where(i == 0, BLOCK, jnp.where(i >= n_lat_blocks, n_loc, 0))
    col_max = jnp.where(i == n_lat_blocks - 1, 2 * BLOCK, n_loc)
    valid = (cols >= n_loc) | ((delta >= 0) & (delta <= 2 * WINDOW)
                               & (cols >= col_min) & (cols < col_max))
    s = jnp.where(valid, s, NEG_INF)
    sink = jnp.concatenate(
        [jnp.broadcast_to(sink_ref[pl.ds(g * GROUP + h, 1), 0:1], (BLOCK, 1)) for h in range(GROUP)],
        axis=0)
    mx = jnp.maximum(jnp.max(s, axis=-1, keepdims=True), sink)
    p = jnp.exp(s - mx)
    denom = jnp.sum(p, axis=-1, keepdims=True) + jnp.exp(sink - mx)
    o = jnp.dot(p.astype(BF16), vals, preferred_element_type=F32) / denom
    o_ref[...] = jnp.concatenate(
        [o[h * BLOCK:(h + 1) * BLOCK, :] for h in range(GROUP)], axis=1).astype(o_ref.dtype)


def _attention(qkv, kv_ctx, sink_tbl, *, with_ctx_queries, ctx_row_blk, ctx_k_col, ctx_v_col):
    n_lat_blocks = SEQ // BLOCK
    n_ctx_blocks = CTX_LEN // BLOCK if with_ctx_queries else 0
    n_i = n_lat_blocks + n_ctx_blocks
    n_rows = N_ALL if with_ctx_queries else N_LAT
    k_col = Q_W // HEAD_DIM
    v_col = (Q_W + KV_W) // HEAD_DIM
    lat_last = n_lat_blocks - 1

    def q_blk(b, i):
        return jnp.where(i < n_lat_blocks, b * n_lat_blocks + i,
                         N_LAT // BLOCK + b * (CTX_LEN // BLOCK) + (i - n_lat_blocks))

    def loc_blk(b, i, off):
        return b * n_lat_blocks + jnp.clip(i + off, 0, lat_last)

    def kspec(off, col):
        return pl.BlockSpec((BLOCK, HEAD_DIM), lambda b, g, i: (loc_blk(b, i, off), col + g))

    in_specs = [
        pl.BlockSpec((N_HEADS, LANES), lambda b, g, i: (0, 0)),
        pl.BlockSpec((BLOCK, GROUP * HEAD_DIM), lambda b, g, i: (q_blk(b, i), g)),
        kspec(-1, k_col), kspec(0, k_col), kspec(1, k_col),
        kspec(-1, v_col), kspec(0, v_col), kspec(1, v_col),
        pl.BlockSpec((CTX_LEN, HEAD_DIM), lambda b, g, i: (ctx_row_blk + b, ctx_k_col + g)),
        pl.BlockSpec((CTX_LEN, HEAD_DIM), lambda b, g, i: (ctx_row_blk + b, ctx_v_col + g)),
    ]
    return pl.pallas_call(
        functools.partial(_attn_kernel, n_lat_blocks=n_lat_blocks),
        grid=(BATCH, N_KV_HEADS, n_i),
        in_specs=in_specs,
        out_specs=pl.BlockSpec((BLOCK, GROUP * HEAD_DIM), lambda b, g, i: (q_blk(b, i), g)),
        out_shape=jax.ShapeDtypeStruct((n_rows, Q_W), BF16),
        compiler_params=_params(3),
        name="attention",
    )(sink_tbl, qkv, qkv, qkv, qkv, qkv, qkv, qkv, kv_ctx, kv_ctx)


CONV_ROWS = 256
HALO = 16


def _conv_kernel(w_ref, u_ref, gb_ref, gc_ref, up_ref, gcp_ref, un_ref, gcn_ref, o_ref):
    r = pl.program_id(0)
    lat_blocks = N_LAT // CONV_ROWS
    per_seq = SEQ // CONV_ROWS
    is_ctx = r >= lat_blocks
    seq_start = is_ctx | (r % per_seq == 0)
    seq_end = is_ctx | (r % per_seq == per_seq - 1)
    z = gc_ref[...].astype(F32) * u_ref[...].astype(F32)
    z_before = (gcp_ref[HALO - 1:HALO, :].astype(F32) * up_ref[HALO - 1:HALO, :].astype(F32))
    z_after = gcn_ref[0:1, :].astype(F32) * un_ref[0:1, :].astype(F32)
    z_before = jnp.where(seq_start, 0.0, z_before)
    z_after = jnp.where(seq_end, 0.0, z_after)
    row = lax.broadcasted_iota(jnp.int32, z.shape, 0)
    z_prev = jnp.where(row == 0, z_before, pltpu.roll(z, 1, 0))
    z_next = jnp.where(row == CONV_ROWS - 1, z_after, pltpu.roll(z, CONV_ROWS - 1, 0))
    y = z_prev * w_ref[0:1, :] + z * w_ref[1:2, :] + z_next * w_ref[2:3, :]
    o_ref[...] = (gb_ref[...].astype(F32) * y).astype(o_ref.dtype)


def _short_conv(rest, conv_w, layer, n_rows):
    n_blocks = n_rows // CONV_ROWS
    tc = D_MODEL
    halo_per_block = CONV_ROWS // HALO
    last_halo = n_rows // HALO - 1

    def main(chunk):
        return pl.BlockSpec((CONV_ROWS, tc), lambda r: (r, chunk))

    def before(chunk):
        return pl.BlockSpec((HALO, tc), lambda r: (jnp.maximum(r * halo_per_block - 1, 0), chunk))

    def after(chunk):
        return pl.BlockSpec((HALO, tc),
                            lambda r: (jnp.minimum((r + 1) * halo_per_block, last_halo), chunk))

    return pl.pallas_call(
        _conv_kernel,
        grid=(n_blocks,),
        in_specs=[
            pl.BlockSpec((None, 8, tc), lambda r: (layer, 0, 0)),
            main(0), main(1), main(2), before(0), before(2), after(0), after(2),
        ],
        out_specs=pl.BlockSpec((CONV_ROWS, tc), lambda r: (r, 0)),
        out_shape=jax.ShapeDtypeStruct((n_rows, D_MODEL), BF16),
        compiler_params=_params(1),
        name="short_conv",
    )(conv_w, rest, rest, rest, rest, rest, rest, rest)


def _router_kernel(h_ref, w_ref, o_ref):
    h = h_ref[...]
    w = w_ref[...]
    h_hi = h.astype(BF16)
    h_lo = (h - h_hi.astype(F32)).astype(BF16)
    w_hi = w.astype(BF16)
    w_lo = (w - w_hi.astype(F32)).astype(BF16)
    logits = (jnp.dot(h_hi, w_hi, preferred_element_type=F32)
              + jnp.dot(h_hi, w_lo, preferred_element_type=F32)
              + jnp.dot(h_lo, w_hi, preferred_element_type=F32))
    lane = lax.broadcasted_iota(jnp.int32, logits.shape, 1)
    lowest = float(jnp.finfo(F32).min)
    logits = jnp.where(lane < N_EXPERTS, logits, lowest)
    v1 = jnp.max(logits, axis=-1, keepdims=True)
    i1 = jnp.min(jnp.where(logits == v1, lane, LANES), axis=-1, keepdims=True)
    rest = jnp.where(lane == i1, lowest, logits)
    v2 = jnp.max(rest, axis=-1, keepdims=True)
    i2 = jnp.min(jnp.where(rest == v2, lane, LANES), axis=-1, keepdims=True)
    e2 = jnp.exp(v2 - v1)
    p1 = 1.0 / (1.0 + e2)
    p2 = e2 / (1.0 + e2)
    out = jnp.where(lane == 0, i1.astype(F32),
                    jnp.where(lane == 1, i2.astype(F32),
                              jnp.where(lane == 2, p1, jnp.where(lane == 3, p2, 0.0))))
    o_ref[...] = out


def _router(h, w_router_padded):
    tm = 512
    n_rows = h.shape[0]
    return pl.pallas_call(
        _router_kernel,
        grid=(n_rows // tm,),
        in_specs=[pl.BlockSpec((tm, D_MODEL), lambda m: (m, 0)),
                  pl.BlockSpec((D_MODEL, LANES), lambda m: (0, 0))],
        out_specs=pl.BlockSpec((tm, LANES), lambda m: (m, 0)),
        out_shape=jax.ShapeDtypeStruct((n_rows, LANES), F32),
        compiler_params=_params(1),
        name="router",
    )(h, w_router_padded)


DISPATCH_TILE = 256


def _dispatch_kernel(tok_ref, h_hbm, o_ref, buf, sem):
    base = pl.program_id(0) * DISPATCH_TILE

    def row_copy(j):
        return pltpu.make_async_copy(
            h_hbm.at[pl.ds(tok_ref[base + j], 1), :], buf.at[pl.ds(j, 1), :], sem)

    def start(j, carry):
        row_copy(j).start()
        return carry

    def wait(j, carry):
        row_copy(j).wait()
        return carry

    lax.fori_loop(0, DISPATCH_TILE, start, 0)
    lax.fori_loop(0, DISPATCH_TILE, wait, 0)
    o_ref[...] = buf[...].astype(o_ref.dtype)


def _dispatch(h, tok_of_row):
    n_rows = tok_of_row.shape[0]
    grid_spec = pltpu.PrefetchScalarGridSpec(
        num_scalar_prefetch=1,
        grid=(n_rows // DISPATCH_TILE,),
        in_specs=[pl.BlockSpec(memory_space=pl.ANY)],
        out_specs=pl.BlockSpec((DISPATCH_TILE, D_MODEL), lambda t, tok: (t, 0)),
        scratch_shapes=[pltpu.VMEM((DISPATCH_TILE, D_MODEL), F32), pltpu.SemaphoreType.DMA(())],
    )
    return pl.pallas_call(
        _dispatch_kernel,
        grid_spec=grid_spec,
        out_shape=jax.ShapeDtypeStruct((n_rows, D_MODEL), BF16),
        compiler_params=_params(1),
        name="moe_dispatch",
    )(tok_of_row, h)


COMBINE_TILE = 256


def _combine_kernel(pos_ref, y_hbm, x_ref, route_ref, gate_ref, gain_ref, o_ref, buf0, buf1, sem):
    t = pl.program_id(0)
    base = t * COMBINE_TILE

    def copies(j):
        c0 = pltpu.make_async_copy(
            y_hbm.at[pl.ds(pos_ref[2 * (base + j)], 1), :], buf0.at[pl.ds(j, 1), :], sem)
        c1 = pltpu.make_async_copy(
            y_hbm.at[pl.ds(pos_ref[2 * (base + j) + 1], 1), :], buf1.at[pl.ds(j, 1), :], sem)
        return c0, c1

    def start(j, carry):
        c0, c1 = copies(j)
        c0.start()
        c1.start()
        return carry

    def wait(j, carry):
        c0, c1 = copies(j)
        c0.wait()
        c1.wait()
        return carry

    lax.fori_loop(0, COMBINE_TILE, start, 0)
    lax.fori_loop(0, COMBINE_TILE, wait, 0)
    route = route_ref[...]
    moe = route[:, 2:3] * buf0[...] + route[:, 3:4] * buf1[...]
    row = _mod_row(base)
    xn = x_ref[...] + gate_ref[pl.ds(row, 1), :] * moe
    inv = lax.rsqrt(jnp.mean(xn * xn, axis=-1, keepdims=True) + RMS_EPS)
    o_ref[...] = (xn * inv) * gain_ref[...]


def _combine(y_rows, pos, x_rows, route, mod, layer, gate_chunk, norm_f):
    grid_spec = pltpu.PrefetchScalarGridSpec(
        num_scalar_prefetch=1,
        grid=(N_LAT // COMBINE_TILE,),
        in_specs=[
            pl.BlockSpec(memory_space=pl.ANY),
            pl.BlockSpec((COMBINE_TILE, D_MODEL), lambda t, pos: (t, 0)),
            pl.BlockSpec((COMBINE_TILE, LANES), lambda t, pos: (t, 0)),
            pl.BlockSpec((None, MOD_ROWS, D_MODEL), lambda t, pos: (layer, 0, gate_chunk)),
            pl.BlockSpec((1, D_MODEL), lambda t, pos: (0, 0)),
        ],
        out_specs=pl.BlockSpec((COMBINE_TILE, D_MODEL), lambda t, pos: (t, 0)),
        scratch_shapes=[pltpu.VMEM((COMBINE_TILE, D_MODEL), F32),
                        pltpu.VMEM((COMBINE_TILE, D_MODEL), F32),
                        pltpu.SemaphoreType.DMA(())],
    )
    return pl.pallas_call(
        _combine_kernel,
        grid_spec=grid_spec,
        out_shape=jax.ShapeDtypeStruct((N_LAT, D_MODEL), F32),
        compiler_params=_params(1),
        name="moe_combine",
    )(pos, y_rows, x_rows, route, mod, norm_f.reshape(1, D_MODEL))


def _routing_tables(route):
    experts = route[:, :TOP_K].astype(jnp.int32).reshape(-1)
    onehot = (experts[:, None] == jnp.arange(N_EXPERTS, dtype=jnp.int32)[None, :]).astype(jnp.int32)
    rank = jnp.sum((jnp.cumsum(onehot, axis=0) - onehot) * onehot, axis=1)
    counts = jnp.sum(onehot, axis=0)
    padded = ((counts + MOE_TILE - 1) // MOE_TILE) * MOE_TILE
    ends = jnp.cumsum(padded)
    starts = ends - padded
    pos = (jnp.sum(onehot * starts[None, :], axis=1) + rank).astype(jnp.int32)
    tok_of_row = jnp.zeros((MOE_ROWS,), jnp.int32).at[pos].set(
        jnp.arange(TOP_K * N_LAT, dtype=jnp.int32) // TOP_K)
    n_tiles = MOE_ROWS // MOE_TILE
    n_used = (ends[-1] // MOE_TILE).astype(jnp.int32)
    tile_start = jnp.arange(n_tiles, dtype=jnp.int32) * MOE_TILE
    tile_expert = jnp.sum((tile_start[:, None] >= ends[None, :]).astype(jnp.int32), axis=1)
    last_expert = jnp.sum((tile_start[jnp.maximum(n_used - 1, 0)] >= ends).astype(jnp.int32))
    tile_expert = jnp.where(jnp.arange(n_tiles) < n_used, tile_expert, last_expert)
    tile_expert = jnp.minimum(tile_expert, N_EXPERTS - 1).astype(jnp.int32)
    return pos, tok_of_row, tile_expert, n_used.reshape(1)


def kernel(x, c, ctx, c_ctx, w_mod, b_mod, norm1, w_in, sink, conv_w, w_o_attn, w_o_conv, w_out,
           norm2, ffn_w1, ffn_w3, ffn_w2, router, moe_w1, moe_w3, moe_w2, norm_f):
    tm = ROW_TILE
    cvec = jnp.concatenate(
        [c, c_ctx[None, :], jnp.zeros((MOD_ROWS - BATCH - 1, D_MODEL), F32)], axis=0)
    mod = _modulation(cvec, w_mod, b_mod)
    x_all = jnp.concatenate([x.reshape(N_LAT, D_MODEL), ctx.reshape(N_CTX, D_MODEL)], axis=0)
    cos_t, sa_t, sb_t = _rope_tables(tm)
    conv_w8 = jnp.pad(conv_w, ((0, 0), (0, 8 - conv_w.shape[1]), (0, 0)))
    sink_tbl = jnp.broadcast_to(sink[:, :, None], (DEPTH, N_HEADS, LANES))
    seq_tiles = SEQ // tm
    lat_tiles = N_LAT // tm
    out = None

    for layer in range(DEPTH):
        ctx_out = layer < DEPTH - 1
        n_rows = N_ALL if ctx_out else N_LAT
        m_tiles = n_rows // tm
        grp, used = _layer_group(layer, m_tiles)
        chunk = D_MODEL // 512

        h = _norm_mod(x_all, norm1, mod, layer, 0, 1, N_ALL, BF16)
        tn_qkv = 512

        def variant(n):
            return jnp.where(n < Q_W // tn_qkv, 0, jnp.where(n < (Q_W + KV_W) // tn_qkv, 1, 2))

        def tbl_map(n, m):
            return (variant(n), jnp.where(m < lat_tiles, m % seq_tiles, seq_tiles), 0)

        tbl_shape = (None, tm, HEAD_DIM)
        qkv = _ws_matmul(
            [h], [w_in], pairs=[(0, 0)], epilogue=_ep_rope,
            extras=[(cos_t, tbl_shape, tbl_map), (sa_t, tbl_shape, tbl_map),
                    (sb_t, tbl_shape, tbl_map)],
            group=grp, n_used=used, m_tiles=m_tiles, tm=tm, tn=tn_qkv, n_out=QKV_W,
            out_dtype=BF16, name="in_proj_qkv")
        rest = _ws_matmul(
            [h], [w_in], pairs=[(0, 0)], epilogue=_ep_plain,
            group=grp, n_used=used, m_tiles=m_tiles, tm=tm, tn=1024, n_out=REST_W,
            col_off=QKV_W // 1024, out_dtype=BF16, name="in_proj_rest")
        if ctx_out:
            kv_ctx, ctx_row_blk = qkv, N_LAT // CTX_LEN
            ctx_k_col, ctx_v_col = Q_W // HEAD_DIM, (Q_W + KV_W) // HEAD_DIM
        else:
            grp1, used1 = _layer_group(layer, N_CTX // tm)
            kv_ctx = _ws_matmul(
                [h], [w_in], pairs=[(0, 0)], epilogue=_ep_plain,
                group=grp1, n_used=used1, m_tiles=N_CTX // tm, tm=tm, tn=512, n_out=2 * KV_W,
                col_off=Q_W // 512, lhs_row_off=lat_tiles, out_dtype=BF16, name="in_proj_ctx_kv")
            ctx_row_blk, ctx_k_col, ctx_v_col = 0, 0, KV_W // HEAD_DIM
        y_attn = _attention(qkv, kv_ctx, sink_tbl[layer], with_ctx_queries=ctx_out,
                            ctx_row_blk=ctx_row_blk, ctx_k_col=ctx_k_col, ctx_v_col=ctx_v_col)
        y_conv = _short_conv(rest, conv_w8, layer, n_rows)
        tn = 512
        gate_blk = D_MODEL // tn
        merged = _ws_matmul(
            [y_attn, y_conv], [w_o_attn, w_o_conv], pairs=[(0, 0), (1, 1)], epilogue=_ep_merge,
            extras=[(rest, (tm, tn), lambda n, m: (m, 3 * gate_blk + n)),
                    (rest, (tm, tn), lambda n, m: (m, 4 * gate_blk + n))],
            group=grp, n_used=used, m_tiles=m_tiles, tm=tm, tn=tn, n_out=D_MODEL,
            out_dtype=BF16, name="merge")
        x_mid = _ws_matmul(
            [merged], [w_out], pairs=[(0, 0)],
            epilogue=functools.partial(_ep_residual, tm=tm, row_off=0),
            extras=[(x_all, (tm, tn), lambda n, m: (m, n)),
                    (mod, (None, MOD_ROWS, tn), lambda n, m: (layer, 0, 2 * chunk + n))],
            group=grp, n_used=used, m_tiles=m_tiles, tm=tm, tn=tn, n_out=D_MODEL,
            out_dtype=F32, name="out_proj")

        if layer % 2 == 0:
            i = layer // 2
            grp_i, _ = _layer_group(i, m_tiles)
            h2 = _norm_mod(x_mid, norm2, mod, layer, 3, 4, n_rows, BF16)
            hidden = _ws_matmul(
                [h2], [ffn_w1, ffn_w3], pairs=[(0, 0), (0, 1)], epilogue=_ep_swiglu,
                group=grp_i, n_used=used, m_tiles=m_tiles, tm=tm, tn=512, n_out=D_FF,
                out_dtype=BF16, name="ffn_up")
            tn2 = 256
            x_all = _ws_matmul(
                [hidden], [ffn_w2], pairs=[(0, 0)],
                epilogue=functools.partial(_ep_residual, tm=tm, row_off=0),
                extras=[(x_mid, (tm, tn2), lambda n, m: (m, n)),
                        (mod, (None, MOD_ROWS, tn2),
                         lambda n, m: (layer, 0, 5 * (D_MODEL // tn2) + n))],
                group=grp_i, n_used=used, m_tiles=m_tiles, tm=tm, tn=tn2, n_out=D_MODEL,
                out_dtype=F32, name="ffn_down")
        else:
            i = layer // 2
            h2 = _norm_mod(x_mid, norm2, mod, layer, 3, 4, n_rows, F32)
            w_router = jnp.pad(router[i], ((0, 0), (0, LANES - N_EXPERTS)))
            route = _router(h2, w_router)
            pos, tok_of_row, tile_expert, n_used = _routing_tables(route)
            rows = _dispatch(h2, tok_of_row)
            moe_tiles = MOE_ROWS // MOE_TILE
            n_stack = moe_w1.shape[0] * N_EXPERTS
            w1s = moe_w1.reshape(n_stack, D_MODEL, D_FF_EXPERT)
            w3s = moe_w3.reshape(n_stack, D_MODEL, D_FF_EXPERT)
            w2s = moe_w2.reshape(n_stack, D_FF_EXPERT, D_MODEL)
            tile_group = tile_expert + i * N_EXPERTS
            hidden = _ws_matmul(
                [rows], [w1s, w3s], pairs=[(0, 0), (0, 1)], epilogue=_ep_swiglu,
                group=tile_group, n_used=n_used, m_tiles=moe_tiles, tm=MOE_TILE, tn=1024,
                n_out=D_FF_EXPERT, out_dtype=BF16, name="moe_up")
            y_rows = _ws_matmul(
                [hidden], [w2s], pairs=[(0, 0)], epilogue=_ep_plain,
                group=tile_group, n_used=n_used, m_tiles=moe_tiles, tm=MOE_TILE, tn=256,
                n_out=D_MODEL, out_dtype=F32, name="moe_down")
            out = _combine(y_rows, pos, x_mid, route, mod, layer, 5, norm_f)

    return out.reshape(BATCH, SEQ, D_MODEL)
```

```python
import functools

import numpy as np
import jax
import jax.numpy as jnp
from jax import lax
from jax.experimental import pallas as pl
from jax.experimental.pallas import tpu as pltpu

D_MODEL = 2048
BATCH = 4
SEQ = 2048
DEPTH = 2
GRID_W = 64
CTX_LEN = 256
N_HEADS = 16
N_KV_HEADS = 4
GROUP = N_HEADS // N_KV_HEADS
HEAD_DIM = 128
WINDOW = 128
BLOCK = 128
ROPE_THETA = 10000.0
D_FF = 5632
N_EXPERTS = 8
TOP_K = 2
D_FF_EXPERT = 7168
RMS_EPS = 1e-6
NEG_INF = -1e30
N_MOD = 6
Q_W = N_HEADS * HEAD_DIM
KV_W = N_KV_HEADS * HEAD_DIM
QKV_W = Q_W + 2 * KV_W
REST_W = 3 * D_MODEL + 2 * D_MODEL

N_LAT = BATCH * SEQ
N_CTX = BATCH * CTX_LEN
N_ALL = N_LAT + N_CTX
CTX_MOD_ROW = BATCH
MOD_ROWS = 8

LANES = 128
VMEM_LIMIT = 60 * 1024 * 1024
ROW_TILE = 1024
MOE_TILE = 512
MOE_ROWS = TOP_K * N_LAT + N_EXPERTS * MOE_TILE
CAST_ROWS = 256

F32 = jnp.float32
BF16 = jnp.bfloat16


def _params(n_axes):
    return pltpu.CompilerParams(
        dimension_semantics=("arbitrary",) * n_axes, vmem_limit_bytes=VMEM_LIMIT)


def _sigmoid(v):
    return 1.0 / (1.0 + jnp.exp(-v))


def _mod_row(m_tile_start_row):
    return jnp.minimum(m_tile_start_row // SEQ, CTX_MOD_ROW)


def _modulation_kernel(c_ref, w_ref, b_ref, o_ref):
    cv = c_ref[...]
    act = (cv * _sigmoid(cv)).astype(BF16)
    o_ref[...] = jnp.dot(act, w_ref[...].astype(BF16), preferred_element_type=F32) + b_ref[...]


def _modulation(cvec, w_mod, b_mod):
    tn = 1024
    n_tiles = N_MOD * D_MODEL // tn
    return pl.pallas_call(
        _modulation_kernel,
        grid=(DEPTH, n_tiles),
        in_specs=[
            pl.BlockSpec((MOD_ROWS, D_MODEL), lambda l, n: (0, 0)),
            pl.BlockSpec((None, D_MODEL, tn), lambda l, n: (l, 0, n)),
            pl.BlockSpec((None, 1, tn), lambda l, n: (l, 0, n)),
        ],
        out_specs=pl.BlockSpec((None, MOD_ROWS, tn), lambda l, n: (l, 0, n)),
        out_shape=jax.ShapeDtypeStruct((DEPTH, MOD_ROWS, N_MOD * D_MODEL), F32),
        compiler_params=_params(2),
        name="modulation",
    )(cvec, w_mod, b_mod.reshape(DEPTH, 1, N_MOD * D_MODEL))


def _norm_kernel(x_ref, g_ref, sh_ref, sc_ref, o_ref, *, tm):
    row = _mod_row(pl.program_id(0) * tm)
    xf = x_ref[...]
    inv = lax.rsqrt(jnp.mean(xf * xf, axis=-1, keepdims=True) + RMS_EPS)
    y = (xf * inv) * g_ref[...]
    y = y * (1.0 + sc_ref[pl.ds(row, 1), :]) + sh_ref[pl.ds(row, 1), :]
    o_ref[...] = y.astype(o_ref.dtype)


def _norm_mod(x_rows, gain, mod, layer, shift_chunk, scale_chunk, n_rows, out_dtype):
    tm = 512
    return pl.pallas_call(
        functools.partial(_norm_kernel, tm=tm),
        grid=(n_rows // tm,),
        in_specs=[
            pl.BlockSpec((tm, D_MODEL), lambda m: (m, 0)),
            pl.BlockSpec((None, 1, D_MODEL), lambda m: (layer, 0, 0)),
            pl.BlockSpec((None, MOD_ROWS, D_MODEL), lambda m: (layer, 0, shift_chunk)),
            pl.BlockSpec((None, MOD_ROWS, D_MODEL), lambda m: (layer, 0, scale_chunk)),
        ],
        out_specs=pl.BlockSpec((tm, D_MODEL), lambda m: (m, 0)),
        out_shape=jax.ShapeDtypeStruct((n_rows, D_MODEL), out_dtype),
        compiler_params=_params(1),
        name="norm_mod",
    )(x_rows, gain.reshape(DEPTH, 1, D_MODEL), mod, mod)


def _ws_kernel(group_ref, nused_ref, *refs, n_lhs, n_w, n_extra, pairs, epilogue, k_dim,
               tail_tiles):
    lhs_refs = refs[:n_lhs]
    w_refs = refs[n_lhs:n_lhs + n_w]
    ex_refs = refs[n_lhs + n_w:n_lhs + n_w + n_extra]
    out_ref = refs[n_lhs + n_w + n_extra]
    wbf_refs = refs[n_lhs + n_w + n_extra + 1:]
    n = pl.program_id(0)
    m = pl.program_id(1)
    changed = (m == 0) | (group_ref[m] != group_ref[jnp.maximum(m - 1, 0)])

    @pl.when(changed)
    def _():
        def cast_step(i, carry):
            rows = pl.ds(pl.multiple_of(i * CAST_ROWS, CAST_ROWS), CAST_ROWS)
            for w_ref, wbf in zip(w_refs, wbf_refs):
                wbf[rows, :] = w_ref[rows, :].astype(BF16)
            return carry
        lax.fori_loop(0, k_dim // CAST_ROWS, cast_step, 0)

    def lhs_tile(li):
        if li == 0 and tail_tiles is not None:
            return jnp.where(m < tail_tiles, lhs_refs[0][...], lhs_refs[n_lhs - 1][...])
        return lhs_refs[li][...]

    @pl.when(m < nused_ref[0])
    def _():
        dots = [jnp.dot(lhs_tile(li), wbf_refs[wi][...], preferred_element_type=F32)
                for li, wi in pairs]
        out_ref[...] = epilogue(dots, ex_refs, n, m).astype(out_ref.dtype)

    @pl.when(m >= nused_ref[0])
    def _():
        out_ref[...] = jnp.zeros_like(out_ref)


def _ws_matmul(lhs, weights, *, pairs, epilogue, extras=(), group, n_used, m_tiles, tm, tn,
               n_out, col_off=0, lhs_row_off=0, lhs0_tail=None, out_dtype, name):
    k_dim = lhs[0].shape[1]
    col_blk = col_off
    row_blk = lhs_row_off

    def wrap(fn):
        return lambda n, m, g, u: fn(n, m)

    in_specs = [pl.BlockSpec((tm, k_dim), lambda n, m, g, u: (m + row_blk, 0)) for _ in lhs]
    tail_tiles = None
    if lhs0_tail is not None:
        tail_tiles = lhs[0].shape[0] // tm
        in_specs[0] = pl.BlockSpec(
            (tm, k_dim), lambda n, m, g, u: (jnp.minimum(m, tail_tiles - 1), 0))
        in_specs.append(pl.BlockSpec(
            (tm, k_dim), lambda n, m, g, u: (jnp.maximum(m - tail_tiles, 0), 0)))
        lhs = list(lhs) + [lhs0_tail]
    in_specs += [pl.BlockSpec((None, k_dim, tn), lambda n, m, g, u: (g[m], 0, n + col_blk))
                 for _ in weights]
    in_specs += [pl.BlockSpec(shape, wrap(fn)) for _, shape, fn in extras]
    grid_spec = pltpu.PrefetchScalarGridSpec(
        num_scalar_prefetch=2,
        grid=(n_out // tn, m_tiles),
        in_specs=in_specs,
        out_specs=pl.BlockSpec((tm, tn), lambda n, m, g, u: (m, n)),
        scratch_shapes=[pltpu.VMEM((k_dim, tn), BF16) for _ in weights],
    )
    kern = functools.partial(
        _ws_kernel, n_lhs=len(lhs), n_w=len(weights), n_extra=len(extras), pairs=pairs,
        epilogue=epilogue, k_dim=k_dim, tail_tiles=tail_tiles)
    return pl.pallas_call(
        kern,
        grid_spec=grid_spec,
        out_shape=jax.ShapeDtypeStruct((m_tiles * tm, n_out), out_dtype),
        compiler_params=_params(2),
        name=name,
    )(group, n_used, *lhs, *weights, *[e[0] for e in extras])


def _layer_group(layer, m_tiles):
    return jnp.full((m_tiles,), layer, jnp.int32), jnp.full((1,), m_tiles, jnp.int32)


def _ep_plain(dots, ex, n, m):
    return dots[0]


def _ep_rope(dots, ex, n, m):
    cos_ref, sa_ref, sb_ref = ex
    d = dots[0]
    cos, sa, sb = cos_ref[...], sa_ref[...], sb_ref[...]
    quarter = HEAD_DIM // 4
    heads = []
    for h in range(d.shape[1] // HEAD_DIM):
        t = d[:, h * HEAD_DIM:(h + 1) * HEAD_DIM]
        heads.append(t * cos + pltpu.roll(t, HEAD_DIM - quarter, 1) * sa
                     + pltpu.roll(t, quarter, 1) * sb)
    return jnp.concatenate(heads, axis=1)


def _ep_merge(dots, ex, n, m):
    ga_ref, gc_ref = ex
    return (_sigmoid(ga_ref[...].astype(F32)) * dots[0]
            + _sigmoid(gc_ref[...].astype(F32)) * dots[1])


def _ep_residual(dots, ex, n, m, *, tm, row_off):
    x_ref, gate_ref = ex
    row = _mod_row((m + row_off) * tm)
    return x_ref[...] + gate_ref[pl.ds(row, 1), :] * dots[0]


def _ep_swiglu(dots, ex, n, m):
    a = dots[0]
    return (a * _sigmoid(a)) * dots[1]


def _rope_tables(tm):
    quarter = HEAD_DIM // 4
    inv_freq = np.power(np.float32(ROPE_THETA),
                        -np.arange(quarter, dtype=np.float32) / np.float32(quarter)).astype(np.float32)
    pos = np.arange(SEQ)
    rows = (pos // GRID_W).astype(np.float32)
    cols = (pos % GRID_W).astype(np.float32)
    ang = np.concatenate([rows[:, None] * inv_freq[None, :]] * 2
                         + [cols[:, None] * inv_freq[None, :]] * 2, axis=1).astype(np.float32)
    cos = np.cos(ang).astype(np.float32)
    sin = np.sin(ang).astype(np.float32)
    lane = np.arange(HEAD_DIM)
    first = ((lane // quarter) % 2 == 0)[None, :]
    sa = np.where(first, -sin, 0.0).astype(np.float32)
    sb = np.where(first, 0.0, sin).astype(np.float32)
    ones = np.ones((tm, HEAD_DIM), np.float32)
    zeros = np.zeros((tm, HEAD_DIM), np.float32)
    scale = np.float32(HEAD_DIM ** -0.5)

    def variants(tbl, ident):
        k = np.concatenate([tbl, ident], axis=0)
        ident_all = np.concatenate([np.broadcast_to(ident[:1], tbl.shape), ident], axis=0)
        return np.stack([k * scale, k, ident_all])

    return (jnp.asarray(variants(cos, ones)), jnp.asarray(variants(sa, zeros)),
            jnp.asarray(variants(sb, zeros)))


ATTN_ROWS = 2 * BLOCK
BAND = 3 * BLOCK


def _band_masks():
    r = np.arange(BLOCK)[:, None]
    c = np.arange(BAND)[None, :]
    masks = [(np.abs(off * BLOCK + r - c) <= WINDOW).astype(np.float32) for off in range(3)]
    return jnp.asarray(np.stack([np.tile(m, (GROUP, 1)) for m in masks]))


def _stack_heads(q):
    return jnp.concatenate([q[:, h * HEAD_DIM:(h + 1) * HEAD_DIM] for h in range(GROUP)], axis=0)


def _unstack_heads(o):
    return jnp.concatenate([o[h * BLOCK:(h + 1) * BLOCK, :] for h in range(GROUP)], axis=1)


def _sink_column(sink_ref, g):
    return jnp.concatenate(
        [jnp.broadcast_to(sink_ref[pl.ds(g * GROUP + h, 1), 0:1], (BLOCK, 1)) for h in range(GROUP)],
        axis=0)


def _softmax_pv(s, sink, vals):
    mx = jnp.maximum(jnp.max(s, axis=-1, keepdims=True), sink)
    p = jnp.exp(s - mx)
    denom = jnp.sum(p, axis=-1, keepdims=True) + jnp.exp(sink - mx)
    return jnp.dot(p.astype(BF16), vals, preferred_element_type=F32) / denom


_CONTRACT_LAST = (((1,), (1,)), ((), ()))


def _attn_latent_kernel(sink_ref, mask_ref, q_ref, k_ref, v_ref, kx_ref, vx_ref, o_ref, s_even, s_odd):
    n_blocks = SEQ // BLOCK
    sink = _sink_column(sink_ref, pl.program_id(1))

    def band_start(i):
        first = jnp.clip(i - 1, 0, n_blocks - 3)
        return first, pl.multiple_of(first * BLOCK, BLOCK)

    def scores(i, s_ref):
        first, start = band_start(i)
        qs = _stack_heads(q_ref[pl.ds(pl.multiple_of(i * BLOCK, BLOCK), BLOCK), :])
        s_loc = lax.dot_general(qs, k_ref[pl.ds(start, BAND), :], _CONTRACT_LAST,
                                preferred_element_type=F32)
        s_ref[:, 0:BAND] = jnp.where(mask_ref[i - first] > 0.5, s_loc, NEG_INF)
        s_ref[:, BAND:] = lax.dot_general(qs, kx_ref[...], _CONTRACT_LAST,
                                          preferred_element_type=F32)

    def finish(i, s_ref):
        _, start = band_start(i)
        vals = jnp.concatenate([v_ref[pl.ds(start, BAND), :], vx_ref[...]], axis=0)
        o = _softmax_pv(s_ref[...], sink, vals)
        o_ref[pl.ds(pl.multiple_of(i * BLOCK, BLOCK), BLOCK), :] = _unstack_heads(o).astype(o_ref.dtype)

    scores(0, s_even)

    def pair(t, carry):
        i = 2 * t
        scores(i + 1, s_odd)
        finish(i, s_even)
        scores(jnp.minimum(i + 2, n_blocks - 1), s_even)
        finish(i + 1, s_odd)
        return carry

    lax.fori_loop(0, n_blocks // 2, pair, 0)


def _attn_ctx_kernel(sink_ref, q_ref, kx_ref, vx_ref, o_ref):
    sink = _sink_column(sink_ref, pl.program_id(1))
    for sub in range(CTX_LEN // BLOCK):
        rows = slice(sub * BLOCK, (sub + 1) * BLOCK)
        qs = _stack_heads(q_ref[rows, :])
        s = lax.dot_general(qs, kx_ref[...], _CONTRACT_LAST, preferred_element_type=F32)
        o_ref[rows, :] = _unstack_heads(_softmax_pv(s, sink, vx_ref[...])).astype(o_ref.dtype)


def _attention(qkv, kv_ctx, sink_tbl, masks, *, with_ctx_queries, ctx_row_blk, ctx_k_col, ctx_v_col):
    k_col = Q_W // HEAD_DIM
    v_col = (Q_W + KV_W) // HEAD_DIM
    width = GROUP * HEAD_DIM
    sink_spec = pl.BlockSpec((N_HEADS, LANES), lambda b, g: (0, 0))
    kx_spec = pl.BlockSpec((CTX_LEN, HEAD_DIM), lambda b, g: (ctx_row_blk + b, ctx_k_col + g))
    vx_spec = pl.BlockSpec((CTX_LEN, HEAD_DIM), lambda b, g: (ctx_row_blk + b, ctx_v_col + g))
    y = pl.pallas_call(
        _attn_latent_kernel,
        grid=(BATCH, N_KV_HEADS),
        in_specs=[
            sink_spec,
            pl.BlockSpec((3, GROUP * BLOCK, BAND), lambda b, g: (0, 0, 0)),
            pl.BlockSpec((SEQ, width), lambda b, g: (b, g)),
            pl.BlockSpec((SEQ, HEAD_DIM), lambda b, g: (b, k_col + g)),
            pl.BlockSpec((SEQ, HEAD_DIM), lambda b, g: (b, v_col + g)),
            kx_spec, vx_spec,
        ],
        out_specs=pl.BlockSpec((SEQ, width), lambda b, g: (b, g)),
        out_shape=jax.ShapeDtypeStruct((N_LAT, Q_W), BF16),
        scratch_shapes=[pltpu.VMEM((GROUP * BLOCK, BAND + CTX_LEN), F32),
                        pltpu.VMEM((GROUP * BLOCK, BAND + CTX_LEN), F32)],
        compiler_params=_params(2),
        name="attention",
    )(sink_tbl, masks, qkv, qkv, qkv, kv_ctx, kv_ctx)
    if not with_ctx_queries:
        return y, None
    ctx_blk = N_LAT // CTX_LEN
    y_ctx = pl.pallas_call(
        _attn_ctx_kernel,
        grid=(BATCH, N_KV_HEADS),
        in_specs=[
            sink_spec,
            pl.BlockSpec((CTX_LEN, width), lambda b, g: (ctx_blk + b, g)),
            kx_spec, vx_spec,
        ],
        out_specs=pl.BlockSpec((CTX_LEN, width), lambda b, g: (b, g)),
        out_shape=jax.ShapeDtypeStruct((N_CTX, Q_W), BF16),
        compiler_params=_params(2),
        name="attention_ctx",
    )(sink_tbl, qkv, kv_ctx, kv_ctx)
    return y, y_ctx


CONV_ROWS = 256
HALO = 16


def _conv_kernel(w_ref, u_ref, gb_ref, gc_ref, up_ref, gcp_ref, un_ref, gcn_ref, o_ref):
    r = pl.program_id(0)
    lat_blocks = N_LAT // CONV_ROWS
    per_seq = SEQ // CONV_ROWS
    is_ctx = r >= lat_blocks
    seq_start = is_ctx | (r % per_seq == 0)
    seq_end = is_ctx | (r % per_seq == per_seq - 1)
    z = gc_ref[...].astype(F32) * u_ref[...].astype(F32)
    z_before = (gcp_ref[HALO - 1:HALO, :].astype(F32) * up_ref[HALO - 1:HALO, :].astype(F32))
    z_after = gcn_ref[0:1, :].astype(F32) * un_ref[0:1, :].astype(F32)
    z_before = jnp.where(seq_start, 0.0, z_before)
    z_after = jnp.where(seq_end, 0.0, z_after)
    row = lax.broadcasted_iota(jnp.int32, z.shape, 0)
    z_prev = jnp.where(row == 0, z_before, pltpu.roll(z, 1, 0))
    z_next = jnp.where(row == CONV_ROWS - 1, z_after, pltpu.roll(z, CONV_ROWS - 1, 0))
    y = z_prev * w_ref[0:1, :] + z * w_ref[1:2, :] + z_next * w_ref[2:3, :]
    o_ref[...] = (gb_ref[...].astype(F32) * y).astype(o_ref.dtype)


def _short_conv(rest, conv_w, layer, n_rows):
    n_blocks = n_rows // CONV_ROWS
    tc = D_MODEL
    halo_per_block = CONV_ROWS // HALO
    last_halo = n_rows // HALO - 1

    def main(chunk):
        return pl.BlockSpec((CONV_ROWS, tc), lambda r: (r, chunk))

    def before(chunk):
        return pl.BlockSpec((HALO, tc), lambda r: (jnp.maximum(r * halo_per_block - 1, 0), chunk))

    def after(chunk):
        return pl.BlockSpec((HALO, tc),
                            lambda r: (jnp.minimum((r + 1) * halo_per_block, last_halo), chunk))

    return pl.pallas_call(
        _conv_kernel,
        grid=(n_blocks,),
        in_specs=[
            pl.BlockSpec((None, 8, tc), lambda r: (layer, 0, 0)),
            main(0), main(1), main(2), before(0), before(2), after(0), after(2),
        ],
        out_specs=pl.BlockSpec((CONV_ROWS, tc), lambda r: (r, 0)),
        out_shape=jax.ShapeDtypeStruct((n_rows, D_MODEL), BF16),
        compiler_params=_params(1),
        name="short_conv",
    )(conv_w, rest, rest, rest, rest, rest, rest, rest)


ROW_CHUNKS = D_MODEL // LANES


def _norm_route_kernel(x_ref, g_ref, sh_ref, sc_ref, w_ref, chunks_ref, route_ref, *, tm):
    row = _mod_row(pl.program_id(0) * tm)
    xf = x_ref[...]
    inv = lax.rsqrt(jnp.mean(xf * xf, axis=-1, keepdims=True) + RMS_EPS)
    h = (xf * inv) * g_ref[...]
    h = h * (1.0 + sc_ref[pl.ds(row, 1), :]) + sh_ref[pl.ds(row, 1), :]
    for j in range(ROW_CHUNKS):
        chunks_ref[pl.ds(j, tm, stride=ROW_CHUNKS), :] = h[:, j * LANES:(j + 1) * LANES]
    route_ref[...] = _route(h, w_ref[...])


def _norm_route(x_rows, gain, mod, layer, shift_chunk, scale_chunk, w_router_padded, n_rows):
    tm = 512
    return pl.pallas_call(
        functools.partial(_norm_route_kernel, tm=tm),
        grid=(n_rows // tm,),
        in_specs=[
            pl.BlockSpec((tm, D_MODEL), lambda m: (m, 0)),
            pl.BlockSpec((None, 1, D_MODEL), lambda m: (layer, 0, 0)),
            pl.BlockSpec((None, MOD_ROWS, D_MODEL), lambda m: (layer, 0, shift_chunk)),
            pl.BlockSpec((None, MOD_ROWS, D_MODEL), lambda m: (layer, 0, scale_chunk)),
            pl.BlockSpec((D_MODEL, LANES), lambda m: (0, 0)),
        ],
        out_specs=[pl.BlockSpec((tm * ROW_CHUNKS, LANES), lambda m: (m, 0)),
                   pl.BlockSpec((tm, LANES), lambda m: (m, 0))],
        out_shape=[jax.ShapeDtypeStruct((n_rows * ROW_CHUNKS, LANES), F32),
                   jax.ShapeDtypeStruct((n_rows, LANES), F32)],
        compiler_params=_params(1),
        name="norm_route",
    )(x_rows, gain.reshape(DEPTH, 1, D_MODEL), mod, mod, w_router_padded)


def _route(h, w):
    h_hi = h.astype(BF16)
    h_lo = (h - h_hi.astype(F32)).astype(BF16)
    w_hi = w.astype(BF16)
    w_lo = (w - w_hi.astype(F32)).astype(BF16)
    logits = (jnp.dot(h_hi, w_hi, preferred_element_type=F32)
              + jnp.dot(h_hi, w_lo, preferred_element_type=F32)
              + jnp.dot(h_lo, w_hi, preferred_element_type=F32))
    lane = lax.broadcasted_iota(jnp.int32, logits.shape, 1)
    lowest = float(jnp.finfo(F32).min)
    logits = jnp.where(lane < N_EXPERTS, logits, lowest)
    v1 = jnp.max(logits, axis=-1, keepdims=True)
    i1 = jnp.min(jnp.where(logits == v1, lane, LANES), axis=-1, keepdims=True)
    rest = jnp.where(lane == i1, lowest, logits)
    v2 = jnp.max(rest, axis=-1, keepdims=True)
    i2 = jnp.min(jnp.where(rest == v2, lane, LANES), axis=-1, keepdims=True)
    e2 = jnp.exp(v2 - v1)
    p1 = 1.0 / (1.0 + e2)
    p2 = e2 / (1.0 + e2)
    return jnp.where(lane == 0, i1.astype(F32),
                     jnp.where(lane == 1, i2.astype(F32),
                               jnp.where(lane == 2, p1, jnp.where(lane == 3, p2, 0.0))))


DISPATCH_TILE = 256


DMA_UNROLL = 8


def _dispatch_kernel(nused_ref, tok_ref, tok_next_ref, h_hbm, o_ref, buf, sem):
    t = pl.program_id(0)
    slot = t % 2
    n_active = nused_ref[0] * (MOE_TILE // DISPATCH_TILE)
    tile_chunks = DISPATCH_TILE * ROW_CHUNKS

    def gather(idx_ref, dst_slot):
        def body(i, carry):
            for u in range(DMA_UNROLL):
                j = i * DMA_UNROLL + u
                src = pl.multiple_of(idx_ref[0, j] * ROW_CHUNKS, ROW_CHUNKS)
                pltpu.make_async_copy(
                    h_hbm.at[pl.ds(src, ROW_CHUNKS), :],
                    buf.at[dst_slot, pl.ds(j * ROW_CHUNKS, ROW_CHUNKS), :],
                    sem.at[dst_slot]).start(priority=u % 2)
            return carry
        lax.fori_loop(0, DISPATCH_TILE // DMA_UNROLL, body, 0)

    @pl.when((t == 0) & (n_active > 0))
    def _():
        gather(tok_ref, 0)

    @pl.when(t + 1 < n_active)
    def _():
        gather(tok_next_ref, 1 - slot)

    @pl.when(t < n_active)
    def _():
        pltpu.make_async_copy(h_hbm.at[pl.ds(0, tile_chunks), :], buf.at[slot], sem.at[slot]).wait()
        cols = [buf[slot, pl.ds(j, DISPATCH_TILE, stride=ROW_CHUNKS), :] for j in range(ROW_CHUNKS)]
        o_ref[...] = jnp.concatenate(cols, axis=1).astype(o_ref.dtype)

    @pl.when(t >= n_active)
    def _():
        o_ref[...] = jnp.zeros_like(o_ref)


def _dispatch(h_chunks, tok_of_row, n_used):
    n_rows = tok_of_row.shape[0]
    n_steps = n_rows // DISPATCH_TILE
    tok_blocks = tok_of_row.reshape(n_steps, 1, DISPATCH_TILE)
    grid_spec = pltpu.PrefetchScalarGridSpec(
        num_scalar_prefetch=1,
        grid=(n_steps,),
        in_specs=[
            pl.BlockSpec((None, 1, DISPATCH_TILE), lambda t, u: (t, 0, 0), memory_space=pltpu.SMEM),
            pl.BlockSpec((None, 1, DISPATCH_TILE),
                         lambda t, u: (jnp.minimum(t + 1, n_steps - 1), 0, 0),
                         memory_space=pltpu.SMEM),
            pl.BlockSpec(memory_space=pl.ANY),
        ],
        out_specs=pl.BlockSpec((DISPATCH_TILE, D_MODEL), lambda t, u: (t, 0)),
        scratch_shapes=[pltpu.VMEM((2, DISPATCH_TILE * ROW_CHUNKS, LANES), F32),
                        pltpu.SemaphoreType.DMA((2,))],
    )
    return pl.pallas_call(
        _dispatch_kernel,
        grid_spec=grid_spec,
        out_shape=jax.ShapeDtypeStruct((n_rows, D_MODEL), BF16),
        compiler_params=_params(1),
        name="moe_dispatch",
    )(n_used, tok_blocks, tok_blocks, h_chunks)


COMBINE_TILE = 256


def _combine_kernel(pos_ref, pos_next_ref, y_hbm, x_ref, route_ref, gate_ref, gain_ref, o_ref,
                    buf, sem):
    t = pl.program_id(0)
    n_steps = pl.num_programs(0)
    slot = t % 2

    def gather(idx_ref, dst_slot):
        def body(i, carry):
            for u in range(DMA_UNROLL):
                j = i * DMA_UNROLL + u
                for k in range(TOP_K):
                    pltpu.make_async_copy(
                        y_hbm.at[pl.ds(idx_ref[0, TOP_K * j + k], 1), :],
                        buf.at[dst_slot, k, pl.ds(j, 1), :],
                        sem.at[dst_slot]).start(priority=k)
            return carry
        lax.fori_loop(0, COMBINE_TILE // DMA_UNROLL, body, 0)

    @pl.when(t == 0)
    def _():
        gather(pos_ref, 0)

    @pl.when(t + 1 < n_steps)
    def _():
        gather(pos_next_ref, 1 - slot)

    for k in range(TOP_K):
        pltpu.make_async_copy(
            y_hbm.at[pl.ds(0, COMBINE_TILE), :], buf.at[slot, k], sem.at[slot]).wait()
    route = route_ref[...]
    moe = route[:, 2:3] * buf[slot, 0] + route[:, 3:4] * buf[slot, 1]
    row = _mod_row(t * COMBINE_TILE)
    xn = x_ref[...] + gate_ref[pl.ds(row, 1), :] * moe
    inv = lax.rsqrt(jnp.mean(xn * xn, axis=-1, keepdims=True) + RMS_EPS)
    o_ref[...] = (xn * inv) * gain_ref[...]


def _combine(y_rows, pos, x_rows, route, mod, layer, gate_chunk, norm_f):
    n_steps = N_LAT // COMBINE_TILE
    pos_blocks = pos.reshape(n_steps, 1, TOP_K * COMBINE_TILE)
    idx_shape = (None, 1, TOP_K * COMBINE_TILE)
    return pl.pallas_call(
        _combine_kernel,
        grid=(n_steps,),
        in_specs=[
            pl.BlockSpec(idx_shape, lambda t: (t, 0, 0), memory_space=pltpu.SMEM),
            pl.BlockSpec(idx_shape, lambda t: (jnp.minimum(t + 1, n_steps - 1), 0, 0),
                         memory_space=pltpu.SMEM),
            pl.BlockSpec(memory_space=pl.ANY),
            pl.BlockSpec((COMBINE_TILE, D_MODEL), lambda t: (t, 0)),
            pl.BlockSpec((COMBINE_TILE, LANES), lambda t: (t, 0)),
            pl.BlockSpec((None, MOD_ROWS, D_MODEL), lambda t: (layer, 0, gate_chunk)),
            pl.BlockSpec((1, D_MODEL), lambda t: (0, 0)),
        ],
        out_specs=pl.BlockSpec((COMBINE_TILE, D_MODEL), lambda t: (t, 0)),
        out_shape=jax.ShapeDtypeStruct((N_LAT, D_MODEL), F32),
        scratch_shapes=[pltpu.VMEM((2, TOP_K, COMBINE_TILE, D_MODEL), F32),
                        pltpu.SemaphoreType.DMA((2,))],
        compiler_params=_params(1),
        name="moe_combine",
    )(pos_blocks, pos_blocks, y_rows, x_rows, route, mod, norm_f.reshape(1, D_MODEL))


def _routing_tables(route):
    experts = route[:, :TOP_K].astype(jnp.int32).reshape(-1)
    onehot = (experts[:, None] == jnp.arange(N_EXPERTS, dtype=jnp.int32)[None, :]).astype(jnp.int32)
    rank = jnp.sum((jnp.cumsum(onehot, axis=0) - onehot) * onehot, axis=1)
    counts = jnp.sum(onehot, axis=0)
    padded = ((counts + MOE_TILE - 1) // MOE_TILE) * MOE_TILE
    ends = jnp.cumsum(padded)
    starts = ends - padded
    pos = (jnp.sum(onehot * starts[None, :], axis=1) + rank).astype(jnp.int32)
    tok_of_row = jnp.zeros((MOE_ROWS,), jnp.int32).at[pos].set(
        jnp.arange(TOP_K * N_LAT, dtype=jnp.int32) // TOP_K)
    n_tiles = MOE_ROWS // MOE_TILE
    n_used = (ends[-1] // MOE_TILE).astype(jnp.int32)
    tile_start = jnp.arange(n_tiles, dtype=jnp.int32) * MOE_TILE
    tile_expert = jnp.sum((tile_start[:, None] >= ends[None, :]).astype(jnp.int32), axis=1)
    last_expert = jnp.sum((tile_start[jnp.maximum(n_used - 1, 0)] >= ends).astype(jnp.int32))
    tile_expert = jnp.where(jnp.arange(n_tiles) < n_used, tile_expert, last_expert)
    tile_expert = jnp.minimum(tile_expert, N_EXPERTS - 1).astype(jnp.int32)
    return pos, tok_of_row, tile_expert, n_used.reshape(1)


def kernel(x, c, ctx, c_ctx, w_mod, b_mod, norm1, w_in, sink, conv_w, w_o_attn, w_o_conv, w_out,
           norm2, ffn_w1, ffn_w3, ffn_w2, router, moe_w1, moe_w3, moe_w2, norm_f):
    tm = ROW_TILE
    cvec = jnp.concatenate(
        [c, c_ctx[None, :], jnp.zeros((MOD_ROWS - BATCH - 1, D_MODEL), F32)], axis=0)
    mod = _modulation(cvec, w_mod, b_mod)
    x_all = jnp.concatenate([x.reshape(N_LAT, D_MODEL), ctx.reshape(N_CTX, D_MODEL)], axis=0)
    cos_t, sa_t, sb_t = _rope_tables(tm)
    conv_w8 = jnp.pad(conv_w, ((0, 0), (0, 8 - conv_w.shape[1]), (0, 0)))
    sink_tbl = jnp.broadcast_to(sink[:, :, None], (DEPTH, N_HEADS, LANES))
    masks = _band_masks()
    seq_tiles = SEQ // tm
    lat_tiles = N_LAT // tm
    out = None

    for layer in range(DEPTH):
        ctx_out = layer < DEPTH - 1
        n_rows = N_ALL if ctx_out else N_LAT
        m_tiles = n_rows // tm
        grp, used = _layer_group(layer, m_tiles)
        chunk = D_MODEL // 512

        h = _norm_mod(x_all, norm1, mod, layer, 0, 1, N_ALL, BF16)
        tn_qkv = 512

        def variant(n):
            return jnp.where(n < Q_W // tn_qkv, 0, jnp.where(n < (Q_W + KV_W) // tn_qkv, 1, 2))

        def tbl_map(n, m):
            return (variant(n), jnp.where(m < lat_tiles, m % seq_tiles, seq_tiles), 0)

        tbl_shape = (None, tm, HEAD_DIM)
        qkv = _ws_matmul(
            [h], [w_in], pairs=[(0, 0)], epilogue=_ep_rope,
            extras=[(cos_t, tbl_shape, tbl_map), (sa_t, tbl_shape, tbl_map),
                    (sb_t, tbl_shape, tbl_map)],
            group=grp, n_used=used, m_tiles=m_tiles, tm=tm, tn=tn_qkv, n_out=QKV_W,
            out_dtype=BF16, name="in_proj_qkv")
        rest = _ws_matmul(
            [h], [w_in], pairs=[(0, 0)], epilogue=_ep_plain,
            group=grp, n_used=used, m_tiles=m_tiles, tm=tm, tn=1024, n_out=REST_W,
            col_off=QKV_W // 1024, out_dtype=BF16, name="in_proj_rest")
        if ctx_out:
            kv_ctx, ctx_row_blk = qkv, N_LAT // CTX_LEN
            ctx_k_col, ctx_v_col = Q_W // HEAD_DIM, (Q_W + KV_W) // HEAD_DIM
        else:
            grp1, used1 = _layer_group(layer, N_CTX // tm)
            kv_ctx = _ws_matmul(
                [h], [w_in], pairs=[(0, 0)], epilogue=_ep_plain,
                group=grp1, n_used=used1, m_tiles=N_CTX // tm, tm=tm, tn=512, n_out=2 * KV_W,
                col_off=Q_W // 512, lhs_row_off=lat_tiles, out_dtype=BF16, name="in_proj_ctx_kv")
            ctx_row_blk, ctx_k_col, ctx_v_col = 0, 0, KV_W // HEAD_DIM
        y_attn, y_attn_ctx = _attention(
            qkv, kv_ctx, sink_tbl[layer], masks, with_ctx_queries=ctx_out,
            ctx_row_blk=ctx_row_blk, ctx_k_col=ctx_k_col, ctx_v_col=ctx_v_col)
        y_conv = _short_conv(rest, conv_w8, layer, n_rows)
        tn = 512
        gate_blk = D_MODEL // tn
        merged = _ws_matmul(
            [y_attn, y_conv], [w_o_attn, w_o_conv], pairs=[(0, 0), (1, 1)], epilogue=_ep_merge,
            extras=[(rest, (tm, tn), lambda n, m: (m, 3 * gate_blk + n)),
                    (rest, (tm, tn), lambda n, m: (m, 4 * gate_blk + n))],
            group=grp, n_used=used, m_tiles=m_tiles, tm=tm, tn=tn, n_out=D_MODEL,
            lhs0_tail=y_attn_ctx, out_dtype=BF16, name="merge")
        x_mid = _ws_matmul(
            [merged], [w_out], pairs=[(0, 0)],
            epilogue=functools.partial(_ep_residual, tm=tm, row_off=0),
            extras=[(x_all, (tm, tn), lambda n, m: (m, n)),
                    (mod, (None, MOD_ROWS, tn), lambda n, m: (layer, 0, 2 * chunk + n))],
            group=grp, n_used=used, m_tiles=m_tiles, tm=tm, tn=tn, n_out=D_MODEL,
            out_dtype=F32, name="out_proj")

        if layer % 2 == 0:
            i = layer // 2
            grp_i, _ = _layer_group(i, m_tiles)
            h2 = _norm_mod(x_mid, norm2, mod, layer, 3, 4, n_rows, BF16)
            hidden = _ws_matmul(
                [h2], [ffn_w1, ffn_w3], pairs=[(0, 0), (0, 1)], epilogue=_ep_swiglu,
                group=grp_i, n_used=used, m_tiles=m_tiles, tm=tm, tn=512, n_out=D_FF,
                out_dtype=BF16, name="ffn_up")
            tm2, tn2 = 512, 512
            grp_d, used_d = _layer_group(i, n_rows // tm2)
            x_all = _ws_matmul(
                [hidden], [ffn_w2], pairs=[(0, 0)],
                epilogue=functools.partial(_ep_residual, tm=tm2, row_off=0),
                extras=[(x_mid, (tm2, tn2), lambda n, m: (m, n)),
                        (mod, (None, MOD_ROWS, tn2),
                         lambda n, m: (layer, 0, 5 * (D_MODEL // tn2) + n))],
                group=grp_d, n_used=used_d, m_tiles=n_rows // tm2, tm=tm2, tn=tn2, n_out=D_MODEL,
                out_dtype=F32, name="ffn_down")
        else:
            i = layer // 2
            w_router = jnp.pad(router[i], ((0, 0), (0, LANES - N_EXPERTS)))
            h2_chunks, route = _norm_route(x_mid, norm2, mod, layer, 3, 4, w_router, n_rows)
            pos, tok_of_row, tile_expert, n_used = _routing_tables(route)
            rows = _dispatch(h2_chunks, tok_of_row, n_used)
            moe_tiles = MOE_ROWS // MOE_TILE
            n_stack = moe_w1.shape[0] * N_EXPERTS
            w1s = moe_w1.reshape(n_stack, D_MODEL, D_FF_EXPERT)
            w3s = moe_w3.reshape(n_stack, D_MODEL, D_FF_EXPERT)
            w2s = moe_w2.reshape(n_stack, D_FF_EXPERT, D_MODEL)
            tile_group = tile_expert + i * N_EXPERTS
            hidden = _ws_matmul(
                [rows], [w1s, w3s], pairs=[(0, 0), (0, 1)], epilogue=_ep_swiglu,
                group=tile_group, n_used=n_used, m_tiles=moe_tiles, tm=MOE_TILE, tn=1024,
                n_out=D_FF_EXPERT, out_dtype=BF16, name="moe_up")
            y_rows = _ws_matmul(
                [hidden], [w2s], pairs=[(0, 0)], epilogue=_ep_plain,
                group=tile_group, n_used=n_used, m_tiles=moe_tiles, tm=MOE_TILE, tn=512,
                n_out=D_MODEL, out_dtype=F32, name="moe_down")
            out = _combine(y_rows, pos, x_mid, route, mod, layer, 5, norm_f)

    return out.reshape(BATCH, SEQ, D_MODEL)
```

```python
import functools

import numpy as np
import jax
import jax.numpy as jnp
from jax import lax
from jax.experimental import pallas as pl
from jax.experimental.pallas import tpu as pltpu

D_MODEL = 2048
BATCH = 4
SEQ = 2048
DEPTH = 2
GRID_W = 64
CTX_LEN = 256
N_HEADS = 16
N_KV_HEADS = 4
GROUP = N_HEADS // N_KV_HEADS
HEAD_DIM = 128
WINDOW = 128
BLOCK = 128
ROPE_THETA = 10000.0
D_FF = 5632
N_EXPERTS = 8
TOP_K = 2
D_FF_EXPERT = 7168
RMS_EPS = 1e-6
NEG_INF = -1e30
N_MOD = 6
Q_W = N_HEADS * HEAD_DIM
KV_W = N_KV_HEADS * HEAD_DIM
QKV_W = Q_W + 2 * KV_W
REST_W = 3 * D_MODEL + 2 * D_MODEL

N_LAT = BATCH * SEQ
N_CTX = BATCH * CTX_LEN
N_ALL = N_LAT + N_CTX
CTX_MOD_ROW = BATCH
MOD_ROWS = 8

LANES = 128
VMEM_LIMIT = 60 * 1024 * 1024
ROW_TILE = 1024
MOE_TILE = 512
MOE_ROWS = TOP_K * N_LAT + N_EXPERTS * MOE_TILE
PART_ROWS = 128

F32 = jnp.float32
BF16 = jnp.bfloat16


def _params(n_axes):
    return pltpu.CompilerParams(
        dimension_semantics=("arbitrary",) * n_axes, vmem_limit_bytes=VMEM_LIMIT)


def _sigmoid(v):
    return 1.0 / (1.0 + jnp.exp(-v))


def _mod_row(m_tile_start_row):
    return jnp.minimum(m_tile_start_row // SEQ, CTX_MOD_ROW)


def _modulation_kernel(c_ref, w_ref, b_ref, o_ref):
    cv = c_ref[...]
    act = (cv * _sigmoid(cv)).astype(BF16)
    o_ref[...] = jnp.dot(act, w_ref[...].astype(BF16), preferred_element_type=F32) + b_ref[...]


def _modulation(cvec, w_mod, b_mod):
    tn = 1024
    n_tiles = N_MOD * D_MODEL // tn
    return pl.pallas_call(
        _modulation_kernel,
        grid=(DEPTH, n_tiles),
        in_specs=[
            pl.BlockSpec((MOD_ROWS, D_MODEL), lambda l, n: (0, 0)),
            pl.BlockSpec((None, D_MODEL, tn), lambda l, n: (l, 0, n)),
            pl.BlockSpec((None, 1, tn), lambda l, n: (l, 0, n)),
        ],
        out_specs=pl.BlockSpec((None, MOD_ROWS, tn), lambda l, n: (l, 0, n)),
        out_shape=jax.ShapeDtypeStruct((DEPTH, MOD_ROWS, N_MOD * D_MODEL), F32),
        compiler_params=_params(2),
        name="modulation",
    )(cvec, w_mod, b_mod.reshape(DEPTH, 1, N_MOD * D_MODEL))


def _norm_kernel(x_ref, *refs, tm, tail_tiles):
    m = pl.program_id(0)
    row = _mod_row(m * tm)
    if tail_tiles is None:
        g_ref, sh_ref, sc_ref, o_ref = refs
        xf = x_ref[...]
    else:
        x_tail_ref, g_ref, sh_ref, sc_ref, o_ref = refs
        xf = jnp.where(m < tail_tiles, x_ref[...], x_tail_ref[...])
    inv = lax.rsqrt(jnp.mean(xf * xf, axis=-1, keepdims=True) + RMS_EPS)
    y = (xf * inv) * g_ref[...]
    y = y * (1.0 + sc_ref[pl.ds(row, 1), :]) + sh_ref[pl.ds(row, 1), :]
    o_ref[...] = y.astype(o_ref.dtype)


def _norm_mod(x_rows, gain, mod, layer, shift_chunk, scale_chunk, n_rows, out_dtype, x_tail=None):
    tm = 512
    xs = [x_rows]
    x_specs = [pl.BlockSpec((tm, D_MODEL), lambda m: (m, 0))]
    tail_tiles = None
    if x_tail is not None:
        tail_tiles = x_rows.shape[0] // tm
        xs.append(x_tail)
        x_specs = [pl.BlockSpec((tm, D_MODEL), lambda m: (jnp.minimum(m, tail_tiles - 1), 0)),
                   pl.BlockSpec((tm, D_MODEL), lambda m: (jnp.maximum(m - tail_tiles, 0), 0))]
    return pl.pallas_call(
        functools.partial(_norm_kernel, tm=tm, tail_tiles=tail_tiles),
        grid=(n_rows // tm,),
        in_specs=x_specs + [
            pl.BlockSpec((None, 1, D_MODEL), lambda m: (layer, 0, 0)),
            pl.BlockSpec((None, MOD_ROWS, D_MODEL), lambda m: (layer, 0, shift_chunk)),
            pl.BlockSpec((None, MOD_ROWS, D_MODEL), lambda m: (layer, 0, scale_chunk)),
        ],
        out_specs=pl.BlockSpec((tm, D_MODEL), lambda m: (m, 0)),
        out_shape=jax.ShapeDtypeStruct((n_rows, D_MODEL), out_dtype),
        compiler_params=_params(1),
        name="norm_mod",
    )(*xs, gain.reshape(DEPTH, 1, D_MODEL), mod, mod)


def _ws_kernel(group_ref, rows_ref, *refs, n_lhs, n_w, n_extra, pairs, epilogue, tail_tiles, tm,
               n_sub, partial):
    del group_ref
    lhs_refs = refs[:n_lhs]
    w_refs = refs[n_lhs:n_lhs + n_w]
    ex_refs = refs[n_lhs + n_w:n_lhs + n_w + n_extra]
    out_ref = refs[n_lhs + n_w + n_extra]
    n = pl.program_id(0)
    m = pl.program_id(1)

    def lhs_tile(li, rs):
        if li == 0 and tail_tiles is not None:
            return jnp.where(m < tail_tiles, lhs_refs[0][rs, :], lhs_refs[n_lhs - 1][rs, :])
        return lhs_refs[li][rs, :]

    def compute(row_slices):
        w_bf = [w_ref[...].astype(BF16) for w_ref in w_refs]
        for rs in row_slices:
            dots = [jnp.dot(lhs_tile(li, rs), w_bf[wi], preferred_element_type=F32)
                    for li, wi in pairs]
            out_ref[rs, :] = epilogue(dots, ex_refs, n, m, rs).astype(out_ref.dtype)

    def slices(size):
        return [slice(r, r + size) for r in range(0, tm, size)]

    if not partial:
        compute(slices(tm // n_sub))
        return

    rows = rows_ref[m]

    @pl.when(rows == tm)
    def _():
        compute(slices(tm // n_sub))

    for rs in slices(PART_ROWS):
        @pl.when((rows < tm) & (rs.start < rows))
        def _():
            compute([rs])

        @pl.when((rows < tm) & (rs.start >= rows))
        def _():
            out_ref[rs, :] = jnp.zeros((PART_ROWS, out_ref.shape[1]), out_ref.dtype)


def _ws_matmul(lhs, weights, *, pairs, epilogue, extras=(), group, tile_rows=None, m_tiles, tm, tn,
               n_out, col_off=0, lhs_row_off=0, lhs0_tail=None, n_sub=1, out_dtype, name):
    k_dim = lhs[0].shape[1]
    col_blk = col_off
    row_blk = lhs_row_off
    partial = tile_rows is not None
    if tile_rows is None:
        tile_rows = jnp.full((m_tiles,), tm, jnp.int32)

    def wrap(fn):
        return lambda n, m, g, u: fn(n, m)

    in_specs = [pl.BlockSpec((tm, k_dim), lambda n, m, g, u: (m + row_blk, 0)) for _ in lhs]
    tail_tiles = None
    if lhs0_tail is not None:
        tail_tiles = lhs[0].shape[0] // tm
        in_specs[0] = pl.BlockSpec(
            (tm, k_dim), lambda n, m, g, u: (jnp.minimum(m, tail_tiles - 1), 0))
        in_specs.append(pl.BlockSpec(
            (tm, k_dim), lambda n, m, g, u: (jnp.maximum(m - tail_tiles, 0), 0)))
        lhs = list(lhs) + [lhs0_tail]
    in_specs += [pl.BlockSpec((None, k_dim, tn), lambda n, m, g, u: (g[m], 0, n + col_blk))
                 for _ in weights]
    in_specs += [pl.BlockSpec(shape, wrap(fn)) for _, shape, fn in extras]
    grid_spec = pltpu.PrefetchScalarGridSpec(
        num_scalar_prefetch=2,
        grid=(n_out // tn, m_tiles),
        in_specs=in_specs,
        out_specs=pl.BlockSpec((tm, tn), lambda n, m, g, u: (m, n)),
    )
    kern = functools.partial(
        _ws_kernel, n_lhs=len(lhs), n_w=len(weights), n_extra=len(extras), pairs=pairs,
        epilogue=epilogue, tail_tiles=tail_tiles, tm=tm, n_sub=n_sub, partial=partial)
    return pl.pallas_call(
        kern,
        grid_spec=grid_spec,
        out_shape=jax.ShapeDtypeStruct((m_tiles * tm, n_out), out_dtype),
        compiler_params=_params(2),
        name=name,
    )(group, tile_rows, *lhs, *weights, *[e[0] for e in extras])


def _layer_group(layer, m_tiles):
    return jnp.full((m_tiles,), layer, jnp.int32)


def _ep_plain(dots, ex, n, m, rs):
    return dots[0]


def _ep_rope(dots, ex, n, m, rs):
    cos_ref, sa_ref, sb_ref = ex
    d = dots[0]
    cos, sa, sb = cos_ref[rs, :], sa_ref[rs, :], sb_ref[rs, :]
    quarter = HEAD_DIM // 4
    heads = []
    for h in range(d.shape[1] // HEAD_DIM):
        t = d[:, h * HEAD_DIM:(h + 1) * HEAD_DIM]
        heads.append(t * cos + pltpu.roll(t, HEAD_DIM - quarter, 1) * sa
                     + pltpu.roll(t, quarter, 1) * sb)
    return jnp.concatenate(heads, axis=1)


def _ep_merge(dots, ex, n, m, rs):
    ga_ref, gc_ref = ex
    return (_sigmoid(ga_ref[rs, :].astype(F32)) * dots[0]
            + _sigmoid(gc_ref[rs, :].astype(F32)) * dots[1])


def _ep_residual(dots, ex, n, m, rs, *, tm, tail_tiles=None):
    if tail_tiles is None:
        x_ref, gate_ref = ex
        x = x_ref[rs, :]
    else:
        x_ref, x_tail_ref, gate_ref = ex
        x = jnp.where(m < tail_tiles, x_ref[rs, :], x_tail_ref[rs, :])
    row = _mod_row(m * tm)
    return x + gate_ref[pl.ds(row, 1), :] * dots[0]


def _ep_swiglu(dots, ex, n, m, rs):
    a = dots[0]
    return (a * _sigmoid(a)) * dots[1]


def _rope_tables(tm):
    quarter = HEAD_DIM // 4
    inv_freq = np.power(np.float32(ROPE_THETA),
                        -np.arange(quarter, dtype=np.float32) / np.float32(quarter)).astype(np.float32)
    pos = np.arange(SEQ)
    rows = (pos // GRID_W).astype(np.float32)
    cols = (pos % GRID_W).astype(np.float32)
    ang = np.concatenate([rows[:, None] * inv_freq[None, :]] * 2
                         + [cols[:, None] * inv_freq[None, :]] * 2, axis=1).astype(np.float32)
    cos = np.cos(ang).astype(np.float32)
    sin = np.sin(ang).astype(np.float32)
    lane = np.arange(HEAD_DIM)
    first = ((lane // quarter) % 2 == 0)[None, :]
    sa = np.where(first, -sin, 0.0).astype(np.float32)
    sb = np.where(first, 0.0, sin).astype(np.float32)
    ones = np.ones((tm, HEAD_DIM), np.float32)
    zeros = np.zeros((tm, HEAD_DIM), np.float32)
    scale = np.float32(HEAD_DIM ** -0.5)

    def variants(tbl, ident):
        k = np.concatenate([tbl, ident], axis=0)
        ident_all = np.concatenate([np.broadcast_to(ident[:1], tbl.shape), ident], axis=0)
        return np.stack([k * scale, k, ident_all])

    return (jnp.asarray(variants(cos, ones)), jnp.asarray(variants(sa, zeros)),
            jnp.asarray(variants(sb, zeros)))


ATTN_ROWS = 2 * BLOCK
BAND = 3 * BLOCK


def _band_masks():
    r = np.arange(BLOCK)[:, None]
    c = np.arange(BAND)[None, :]
    masks = [(np.abs(off * BLOCK + r - c) <= WINDOW).astype(np.float32) for off in range(3)]
    return jnp.asarray(np.stack([np.tile(m, (GROUP, 1)) for m in masks]))


def _stack_heads(q):
    return jnp.concatenate([q[:, h * HEAD_DIM:(h + 1) * HEAD_DIM] for h in range(GROUP)], axis=0)


def _unstack_heads(o):
    return jnp.concatenate([o[h * BLOCK:(h + 1) * BLOCK, :] for h in range(GROUP)], axis=1)


def _sink_column(sink_ref, g):
    return jnp.concatenate(
        [jnp.broadcast_to(sink_ref[pl.ds(g * GROUP + h, 1), 0:1], (BLOCK, 1)) for h in range(GROUP)],
        axis=0)


def _softmax_pv(s, sink, vals):
    mx = jnp.maximum(jnp.max(s, axis=-1, keepdims=True), sink)
    p = jnp.exp(s - mx)
    denom = jnp.sum(p, axis=-1, keepdims=True) + jnp.exp(sink - mx)
    return jnp.dot(p.astype(BF16), vals, preferred_element_type=F32) / denom


_CONTRACT_LAST = (((1,), (1,)), ((), ()))


def _attn_latent_kernel(sink_ref, mask_ref, q_ref, k_ref, v_ref, kx_ref, vx_ref, o_ref, s_even, s_odd):
    n_blocks = SEQ // BLOCK
    sink = _sink_column(sink_ref, pl.program_id(1))

    def band_start(i):
        first = jnp.clip(i - 1, 0, n_blocks - 3)
        return first, pl.multiple_of(first * BLOCK, BLOCK)

    def scores(i, s_ref):
        first, start = band_start(i)
        qs = _stack_heads(q_ref[pl.ds(pl.multiple_of(i * BLOCK, BLOCK), BLOCK), :])
        s_loc = lax.dot_general(qs, k_ref[pl.ds(start, BAND), :], _CONTRACT_LAST,
                                preferred_element_type=F32)
        s_ref[:, 0:BAND] = jnp.where(mask_ref[i - first] > 0.5, s_loc, NEG_INF)
        s_ref[:, BAND:] = lax.dot_general(qs, kx_ref[...], _CONTRACT_LAST,
                                          preferred_element_type=F32)

    def finish(i, s_ref):
        _, start = band_start(i)
        vals = jnp.concatenate([v_ref[pl.ds(start, BAND), :], vx_ref[...]], axis=0)
        o = _softmax_pv(s_ref[...], sink, vals)
        o_ref[pl.ds(pl.multiple_of(i * BLOCK, BLOCK), BLOCK), :] = _unstack_heads(o).astype(o_ref.dtype)

    scores(0, s_even)

    def pair(t, carry):
        i = 2 * t
        scores(i + 1, s_odd)
        finish(i, s_even)
        scores(jnp.minimum(i + 2, n_blocks - 1), s_even)
        finish(i + 1, s_odd)
        return carry

    lax.fori_loop(0, n_blocks // 2, pair, 0)


def _attn_ctx_kernel(sink_ref, q_ref, kx_ref, vx_ref, o_ref):
    sink = _sink_column(sink_ref, pl.program_id(1))
    for sub in range(CTX_LEN // BLOCK):
        rows = slice(sub * BLOCK, (sub + 1) * BLOCK)
        qs = _stack_heads(q_ref[rows, :])
        s = lax.dot_general(qs, kx_ref[...], _CONTRACT_LAST, preferred_element_type=F32)
        o_ref[rows, :] = _unstack_heads(_softmax_pv(s, sink, vx_ref[...])).astype(o_ref.dtype)


def _attention(qkv, kv_ctx, sink_tbl, masks, *, with_ctx_queries, ctx_row_blk, ctx_k_col, ctx_v_col):
    k_col = Q_W // HEAD_DIM
    v_col = (Q_W + KV_W) // HEAD_DIM
    width = GROUP * HEAD_DIM
    sink_spec = pl.BlockSpec((N_HEADS, LANES), lambda b, g: (0, 0))
    kx_spec = pl.BlockSpec((CTX_LEN, HEAD_DIM), lambda b, g: (ctx_row_blk + b, ctx_k_col + g))
    vx_spec = pl.BlockSpec((CTX_LEN, HEAD_DIM), lambda b, g: (ctx_row_blk + b, ctx_v_col + g))
    y = pl.pallas_call(
        _attn_latent_kernel,
        grid=(BATCH, N_KV_HEADS),
        in_specs=[
            sink_spec,
            pl.BlockSpec((3, GROUP * BLOCK, BAND), lambda b, g: (0, 0, 0)),
            pl.BlockSpec((SEQ, width), lambda b, g: (b, g)),
            pl.BlockSpec((SEQ, HEAD_DIM), lambda b, g: (b, k_col + g)),
            pl.BlockSpec((SEQ, HEAD_DIM), lambda b, g: (b, v_col + g)),
            kx_spec, vx_spec,
        ],
        out_specs=pl.BlockSpec((SEQ, width), lambda b, g: (b, g)),
        out_shape=jax.ShapeDtypeStruct((N_LAT, Q_W), BF16),
        scratch_shapes=[pltpu.VMEM((GROUP * BLOCK, BAND + CTX_LEN), F32),
                        pltpu.VMEM((GROUP * BLOCK, BAND + CTX_LEN), F32)],
        compiler_params=_params(2),
        name="attention",
    )(sink_tbl, masks, qkv, qkv, qkv, kv_ctx, kv_ctx)
    if not with_ctx_queries:
        return y, None
    ctx_blk = N_LAT // CTX_LEN
    y_ctx = pl.pallas_call(
        _attn_ctx_kernel,
        grid=(BATCH, N_KV_HEADS),
        in_specs=[
            sink_spec,
            pl.BlockSpec((CTX_LEN, width), lambda b, g: (ctx_blk + b, g)),
            kx_spec, vx_spec,
        ],
        out_specs=pl.BlockSpec((CTX_LEN, width), lambda b, g: (b, g)),
        out_shape=jax.ShapeDtypeStruct((N_CTX, Q_W), BF16),
        compiler_params=_params(2),
        name="attention_ctx",
    )(sink_tbl, qkv, kv_ctx, kv_ctx)
    return y, y_ctx


CONV_ROWS = 256
HALO = 16


def _conv_kernel(w_ref, u_ref, gb_ref, gc_ref, up_ref, gcp_ref, un_ref, gcn_ref, o_ref):
    r = pl.program_id(0)
    lat_blocks = N_LAT // CONV_ROWS
    per_seq = SEQ // CONV_ROWS
    is_ctx = r >= lat_blocks
    seq_start = is_ctx | (r % per_seq == 0)
    seq_end = is_ctx | (r % per_seq == per_seq - 1)
    z = gc_ref[...].astype(F32) * u_ref[...].astype(F32)
    z_before = (gcp_ref[HALO - 1:HALO, :].astype(F32) * up_ref[HALO - 1:HALO, :].astype(F32))
    z_after = gcn_ref[0:1, :].astype(F32) * un_ref[0:1, :].astype(F32)
    z_before = jnp.where(seq_start, 0.0, z_before)
    z_after = jnp.where(seq_end, 0.0, z_after)
    row = lax.broadcasted_iota(jnp.int32, z.shape, 0)
    z_prev = jnp.where(row == 0, z_before, pltpu.roll(z, 1, 0))
    z_next = jnp.where(row == CONV_ROWS - 1, z_after, pltpu.roll(z, CONV_ROWS - 1, 0))
    y = z_prev * w_ref[0:1, :] + z * w_ref[1:2, :] + z_next * w_ref[2:3, :]
    o_ref[...] = (gb_ref[...].astype(F32) * y).astype(o_ref.dtype)


def _short_conv(rest, conv_w, layer, n_rows):
    n_blocks = n_rows // CONV_ROWS
    tc = D_MODEL
    halo_per_block = CONV_ROWS // HALO
    last_halo = n_rows // HALO - 1

    def main(chunk):
        return pl.BlockSpec((CONV_ROWS, tc), lambda r: (r, chunk))

    def before(chunk):
        return pl.BlockSpec((HALO, tc), lambda r: (jnp.maximum(r * halo_per_block - 1, 0), chunk))

    def after(chunk):
        return pl.BlockSpec((HALO, tc),
                            lambda r: (jnp.minimum((r + 1) * halo_per_block, last_halo), chunk))

    return pl.pallas_call(
        _conv_kernel,
        grid=(n_blocks,),
        in_specs=[
            pl.BlockSpec((None, 8, tc), lambda r: (layer, 0, 0)),
            main(0), main(1), main(2), before(0), before(2), after(0), after(2),
        ],
        out_specs=pl.BlockSpec((CONV_ROWS, tc), lambda r: (r, 0)),
        out_shape=jax.ShapeDtypeStruct((n_rows, D_MODEL), BF16),
        compiler_params=_params(1),
        name="short_conv",
    )(conv_w, rest, rest, rest, rest, rest, rest, rest)


ROW_CHUNKS = D_MODEL // LANES


def _norm_route_kernel(x_ref, g_ref, sh_ref, sc_ref, w_ref, chunks_ref, route_ref, *, tm):
    row = _mod_row(pl.program_id(0) * tm)
    xf = x_ref[...]
    inv = lax.rsqrt(jnp.mean(xf * xf, axis=-1, keepdims=True) + RMS_EPS)
    h = (xf * inv) * g_ref[...]
    h = h * (1.0 + sc_ref[pl.ds(row, 1), :]) + sh_ref[pl.ds(row, 1), :]
    for j in range(ROW_CHUNKS):
        chunks_ref[pl.ds(j, tm, stride=ROW_CHUNKS), :] = h[:, j * LANES:(j + 1) * LANES]
    route_ref[...] = _route(h, w_ref[...])


def _norm_route(x_rows, gain, mod, layer, shift_chunk, scale_chunk, w_router_padded, n_rows):
    tm = 512
    return pl.pallas_call(
        functools.partial(_norm_route_kernel, tm=tm),
        grid=(n_rows // tm,),
        in_specs=[
            pl.BlockSpec((tm, D_MODEL), lambda m: (m, 0)),
            pl.BlockSpec((None, 1, D_MODEL), lambda m: (layer, 0, 0)),
            pl.BlockSpec((None, MOD_ROWS, D_MODEL), lambda m: (layer, 0, shift_chunk)),
            pl.BlockSpec((None, MOD_ROWS, D_MODEL), lambda m: (layer, 0, scale_chunk)),
            pl.BlockSpec((D_MODEL, LANES), lambda m: (0, 0)),
        ],
        out_specs=[pl.BlockSpec((tm * ROW_CHUNKS, LANES), lambda m: (m, 0)),
                   pl.BlockSpec((tm, LANES), lambda m: (m, 0))],
        out_shape=[jax.ShapeDtypeStruct((n_rows * ROW_CHUNKS, LANES), F32),
                   jax.ShapeDtypeStruct((n_rows, LANES), F32)],
        compiler_params=_params(1),
        name="norm_route",
    )(x_rows, gain.reshape(DEPTH, 1, D_MODEL), mod, mod, w_router_padded)


def _route(h, w):
    h_hi = h.astype(BF16)
    h_lo = (h - h_hi.astype(F32)).astype(BF16)
    w_hi = w.astype(BF16)
    w_lo = (w - w_hi.astype(F32)).astype(BF16)
    logits = (jnp.dot(h_hi, w_hi, preferred_element_type=F32)
              + jnp.dot(h_hi, w_lo, preferred_element_type=F32)
              + jnp.dot(h_lo, w_hi, preferred_element_type=F32))
    lane = lax.broadcasted_iota(jnp.int32, logits.shape, 1)
    lowest = float(jnp.finfo(F32).min)
    logits = jnp.where(lane < N_EXPERTS, logits, lowest)
    v1 = jnp.max(logits, axis=-1, keepdims=True)
    i1 = jnp.min(jnp.where(logits == v1, lane, LANES), axis=-1, keepdims=True)
    rest = jnp.where(lane == i1, lowest, logits)
    v2 = jnp.max(rest, axis=-1, keepdims=True)
    i2 = jnp.min(jnp.where(rest == v2, lane, LANES), axis=-1, keepdims=True)
    e2 = jnp.exp(v2 - v1)
    p1 = 1.0 / (1.0 + e2)
    p2 = e2 / (1.0 + e2)
    return jnp.where(lane == 0, i1.astype(F32),
                     jnp.where(lane == 1, i2.astype(F32),
                               jnp.where(lane == 2, p1, jnp.where(lane == 3, p2, 0.0))))


DISPATCH_TILE = 256


DMA_UNROLL = 8


def _dispatch_kernel(nused_ref, tok_ref, tok_next_ref, h_hbm, o_ref, buf, sem):
    t = pl.program_id(0)
    slot = t % 2
    n_active = nused_ref[0] * (MOE_TILE // DISPATCH_TILE)
    tile_chunks = DISPATCH_TILE * ROW_CHUNKS

    def gather(idx_ref, dst_slot):
        def body(i, carry):
            for u in range(DMA_UNROLL):
                j = i * DMA_UNROLL + u
                src = pl.multiple_of(idx_ref[0, j] * ROW_CHUNKS, ROW_CHUNKS)
                pltpu.make_async_copy(
                    h_hbm.at[pl.ds(src, ROW_CHUNKS), :],
                    buf.at[dst_slot, pl.ds(j * ROW_CHUNKS, ROW_CHUNKS), :],
                    sem.at[dst_slot]).start(priority=u % 2)
            return carry
        lax.fori_loop(0, DISPATCH_TILE // DMA_UNROLL, body, 0)

    @pl.when((t == 0) & (n_active > 0))
    def _():
        gather(tok_ref, 0)

    @pl.when(t + 1 < n_active)
    def _():
        gather(tok_next_ref, 1 - slot)

    @pl.when(t < n_active)
    def _():
        pltpu.make_async_copy(h_hbm.at[pl.ds(0, tile_chunks), :], buf.at[slot], sem.at[slot]).wait()
        cols = [buf[slot, pl.ds(j, DISPATCH_TILE, stride=ROW_CHUNKS), :] for j in range(ROW_CHUNKS)]
        o_ref[...] = jnp.concatenate(cols, axis=1).astype(o_ref.dtype)

    @pl.when(t >= n_active)
    def _():
        o_ref[...] = jnp.zeros_like(o_ref)


def _dispatch(h_chunks, tok_of_row, n_used):
    n_rows = tok_of_row.shape[0]
    n_steps = n_rows // DISPATCH_TILE
    tok_blocks = tok_of_row.reshape(n_steps, 1, DISPATCH_TILE)
    grid_spec = pltpu.PrefetchScalarGridSpec(
        num_scalar_prefetch=1,
        grid=(n_steps,),
        in_specs=[
            pl.BlockSpec((None, 1, DISPATCH_TILE), lambda t, u: (t, 0, 0), memory_space=pltpu.SMEM),
            pl.BlockSpec((None, 1, DISPATCH_TILE),
                         lambda t, u: (jnp.minimum(t + 1, n_steps - 1), 0, 0),
                         memory_space=pltpu.SMEM),
            pl.BlockSpec(memory_space=pl.ANY),
        ],
        out_specs=pl.BlockSpec((DISPATCH_TILE, D_MODEL), lambda t, u: (t, 0)),
        scratch_shapes=[pltpu.VMEM((2, DISPATCH_TILE * ROW_CHUNKS, LANES), F32),
                        pltpu.SemaphoreType.DMA((2,))],
    )
    return pl.pallas_call(
        _dispatch_kernel,
        grid_spec=grid_spec,
        out_shape=jax.ShapeDtypeStruct((n_rows, D_MODEL), BF16),
        compiler_params=_params(1),
        name="moe_dispatch",
    )(n_used, tok_blocks, tok_blocks, h_chunks)


COMBINE_TILE = 256


def _combine_kernel(pos_ref, pos_next_ref, y_hbm, x_ref, route_ref, gate_ref, gain_ref, o_ref,
                    buf, sem):
    t = pl.program_id(0)
    n_steps = pl.num_programs(0)
    slot = t % 2

    def gather(idx_ref, dst_slot):
        def body(i, carry):
            for u in range(DMA_UNROLL):
                j = i * DMA_UNROLL + u
                for k in range(TOP_K):
                    pltpu.make_async_copy(
                        y_hbm.at[pl.ds(idx_ref[0, TOP_K * j + k], 1), :],
                        buf.at[dst_slot, k, pl.ds(j, 1), :],
                        sem.at[dst_slot]).start(priority=k)
            return carry
        lax.fori_loop(0, COMBINE_TILE // DMA_UNROLL, body, 0)

    @pl.when(t == 0)
    def _():
        gather(pos_ref, 0)

    @pl.when(t + 1 < n_steps)
    def _():
        gather(pos_next_ref, 1 - slot)

    for k in range(TOP_K):
        pltpu.make_async_copy(
            y_hbm.at[pl.ds(0, COMBINE_TILE), :], buf.at[slot, k], sem.at[slot]).wait()
    route = route_ref[...]
    moe = route[:, 2:3] * buf[slot, 0] + route[:, 3:4] * buf[slot, 1]
    row = _mod_row(t * COMBINE_TILE)
    xn = x_ref[...] + gate_ref[pl.ds(row, 1), :] * moe
    inv = lax.rsqrt(jnp.mean(xn * xn, axis=-1, keepdims=True) + RMS_EPS)
    o_ref[...] = (xn * inv) * gain_ref[...]


def _combine(y_rows, pos, x_rows, route, mod, layer, gate_chunk, norm_f):
    n_steps = N_LAT // COMBINE_TILE
    pos_blocks = pos.reshape(n_steps, 1, TOP_K * COMBINE_TILE)
    idx_shape = (None, 1, TOP_K * COMBINE_TILE)
    return pl.pallas_call(
        _combine_kernel,
        grid=(n_steps,),
        in_specs=[
            pl.BlockSpec(idx_shape, lambda t: (t, 0, 0), memory_space=pltpu.SMEM),
            pl.BlockSpec(idx_shape, lambda t: (jnp.minimum(t + 1, n_steps - 1), 0, 0),
                         memory_space=pltpu.SMEM),
            pl.BlockSpec(memory_space=pl.ANY),
            pl.BlockSpec((COMBINE_TILE, D_MODEL), lambda t: (t, 0)),
            pl.BlockSpec((COMBINE_TILE, LANES), lambda t: (t, 0)),
            pl.BlockSpec((None, MOD_ROWS, D_MODEL), lambda t: (layer, 0, gate_chunk)),
            pl.BlockSpec((1, D_MODEL), lambda t: (0, 0)),
        ],
        out_specs=pl.BlockSpec((COMBINE_TILE, D_MODEL), lambda t: (t, 0)),
        out_shape=jax.ShapeDtypeStruct((N_LAT, D_MODEL), F32),
        scratch_shapes=[pltpu.VMEM((2, TOP_K, COMBINE_TILE, D_MODEL), F32),
                        pltpu.SemaphoreType.DMA((2,))],
        compiler_params=_params(1),
        name="moe_combine",
    )(pos_blocks, pos_blocks, y_rows, x_rows, route, mod, norm_f.reshape(1, D_MODEL))


def _routing_tables(route):
    experts = route[:, :TOP_K].astype(jnp.int32).reshape(-1)
    onehot = (experts[:, None] == jnp.arange(N_EXPERTS, dtype=jnp.int32)[None, :]).astype(jnp.int32)
    rank = jnp.sum((jnp.cumsum(onehot, axis=0) - onehot) * onehot, axis=1)
    counts = jnp.sum(onehot, axis=0)
    padded = ((counts + MOE_TILE - 1) // MOE_TILE) * MOE_TILE
    ends = jnp.cumsum(padded)
    starts = ends - padded
    pos = (jnp.sum(onehot * starts[None, :], axis=1) + rank).astype(jnp.int32)
    tok_of_row = jnp.zeros((MOE_ROWS,), jnp.int32).at[pos].set(
        jnp.arange(TOP_K * N_LAT, dtype=jnp.int32) // TOP_K)
    n_tiles = MOE_ROWS // MOE_TILE
    n_used = (ends[-1] // MOE_TILE).astype(jnp.int32)
    tile_start = jnp.arange(n_tiles, dtype=jnp.int32) * MOE_TILE
    tile_expert = jnp.sum((tile_start[:, None] >= ends[None, :]).astype(jnp.int32), axis=1)
    last_expert = jnp.sum((tile_start[jnp.maximum(n_used - 1, 0)] >= ends).astype(jnp.int32))
    in_use = jnp.arange(n_tiles) < n_used
    tile_expert = jnp.where(in_use, tile_expert, last_expert)
    tile_expert = jnp.minimum(tile_expert, N_EXPERTS - 1).astype(jnp.int32)
    tile_rows = jnp.clip((starts + counts)[tile_expert] - tile_start, 0, MOE_TILE)
    tile_rows = jnp.where(in_use, tile_rows, 0).astype(jnp.int32)
    return pos, tok_of_row, tile_expert, tile_rows, n_used.reshape(1)


def kernel(x, c, ctx, c_ctx, w_mod, b_mod, norm1, w_in, sink, conv_w, w_o_attn, w_o_conv, w_out,
           norm2, ffn_w1, ffn_w3, ffn_w2, router, moe_w1, moe_w3, moe_w2, norm_f):
    tm = ROW_TILE
    cvec = jnp.concatenate(
        [c, c_ctx[None, :], jnp.zeros((MOD_ROWS - BATCH - 1, D_MODEL), F32)], axis=0)
    mod = _modulation(cvec, w_mod, b_mod)
    x_main, x_tail = x.reshape(N_LAT, D_MODEL), ctx.reshape(N_CTX, D_MODEL)
    cos_t, sa_t, sb_t = _rope_tables(tm)
    conv_w8 = jnp.pad(conv_w, ((0, 0), (0, 8 - conv_w.shape[1]), (0, 0)))
    sink_tbl = jnp.broadcast_to(sink[:, :, None], (DEPTH, N_HEADS, LANES))
    masks = _band_masks()
    seq_tiles = SEQ // tm
    lat_tiles = N_LAT // tm
    out = None

    for layer in range(DEPTH):
        ctx_out = layer < DEPTH - 1
        n_rows = N_ALL if ctx_out else N_LAT
        m_tiles = n_rows // tm
        grp = _layer_group(layer, m_tiles)
        chunk = D_MODEL // 512

        h = _norm_mod(x_main, norm1, mod, layer, 0, 1, N_ALL, BF16, x_tail=x_tail)
        tn_qkv = 512

        def variant(n):
            return jnp.where(n < Q_W // tn_qkv, 0, jnp.where(n < (Q_W + KV_W) // tn_qkv, 1, 2))

        def tbl_map(n, m):
            return (variant(n), jnp.where(m < lat_tiles, m % seq_tiles, seq_tiles), 0)

        tbl_shape = (None, tm, HEAD_DIM)
        qkv = _ws_matmul(
            [h], [w_in], pairs=[(0, 0)], epilogue=_ep_rope,
            extras=[(cos_t, tbl_shape, tbl_map), (sa_t, tbl_shape, tbl_map),
                    (sb_t, tbl_shape, tbl_map)],
            group=grp, m_tiles=m_tiles, tm=tm, tn=tn_qkv, n_out=QKV_W, n_sub=2,
            out_dtype=BF16, name="in_proj_qkv")
        rest = _ws_matmul(
            [h], [w_in], pairs=[(0, 0)], epilogue=_ep_plain,
            group=grp, m_tiles=m_tiles, tm=tm, tn=1024, n_out=REST_W,
            col_off=QKV_W // 1024, out_dtype=BF16, name="in_proj_rest")
        if ctx_out:
            kv_ctx, ctx_row_blk = qkv, N_LAT // CTX_LEN
            ctx_k_col, ctx_v_col = Q_W // HEAD_DIM, (Q_W + KV_W) // HEAD_DIM
        else:
            kv_ctx = _ws_matmul(
                [h], [w_in], pairs=[(0, 0)], epilogue=_ep_plain,
                group=_layer_group(layer, N_CTX // tm), m_tiles=N_CTX // tm, tm=tm, tn=512,
                n_out=2 * KV_W, col_off=Q_W // 512, lhs_row_off=lat_tiles, out_dtype=BF16,
                name="in_proj_ctx_kv")
            ctx_row_blk, ctx_k_col, ctx_v_col = 0, 0, KV_W // HEAD_DIM
        y_attn, y_attn_ctx = _attention(
            qkv, kv_ctx, sink_tbl[layer], masks, with_ctx_queries=ctx_out,
            ctx_row_blk=ctx_row_blk, ctx_k_col=ctx_k_col, ctx_v_col=ctx_v_col)
        y_conv = _short_conv(rest, conv_w8, layer, n_rows)
        tn = 512
        gate_blk = D_MODEL // tn
        merged = _ws_matmul(
            [y_attn, y_conv], [w_o_attn, w_o_conv], pairs=[(0, 0), (1, 1)], epilogue=_ep_merge,
            extras=[(rest, (tm, tn), lambda n, m: (m, 3 * gate_blk + n)),
                    (rest, (tm, tn), lambda n, m: (m, 4 * gate_blk + n))],
            group=grp, m_tiles=m_tiles, tm=tm, tn=tn, n_out=D_MODEL,
            lhs0_tail=y_attn_ctx, out_dtype=BF16, name="merge")
        gate_spec = (mod, (None, MOD_ROWS, tn), lambda n, m: (layer, 0, 2 * chunk + n))
        if x_tail is None:
            res_extras = [(x_main, (tm, tn), lambda n, m: (m, n)), gate_spec]
            res_tail = None
        else:
            res_tail = x_main.shape[0] // tm
            res_extras = [
                (x_main, (tm, tn), lambda n, m: (jnp.minimum(m, res_tail - 1), n)),
                (x_tail, (tm, tn), lambda n, m: (jnp.maximum(m - res_tail, 0), n)),
                gate_spec]
        x_mid = _ws_matmul(
            [merged], [w_out], pairs=[(0, 0)],
            epilogue=functools.partial(_ep_residual, tm=tm, tail_tiles=res_tail),
            extras=res_extras, group=grp, m_tiles=m_tiles, tm=tm, tn=tn, n_out=D_MODEL,
            out_dtype=F32, name="out_proj")

        if layer % 2 == 0:
            i = layer // 2
            h2 = _norm_mod(x_mid, norm2, mod, layer, 3, 4, n_rows, BF16)
            hidden = _ws_matmul(
                [h2], [ffn_w1, ffn_w3], pairs=[(0, 0), (0, 1)], epilogue=_ep_swiglu,
                group=_layer_group(i, m_tiles), m_tiles=m_tiles, tm=tm, tn=512, n_out=D_FF,
                out_dtype=BF16, name="ffn_up")
            tm2, tn2 = 512, 512
            x_main, x_tail = _ws_matmul(
                [hidden], [ffn_w2], pairs=[(0, 0)],
                epilogue=functools.partial(_ep_residual, tm=tm2),
                extras=[(x_mid, (tm2, tn2), lambda n, m: (m, n)),
                        (mod, (None, MOD_ROWS, tn2),
                         lambda n, m: (layer, 0, 5 * (D_MODEL // tn2) + n))],
                group=_layer_group(i, n_rows // tm2), m_tiles=n_rows // tm2, tm=tm2, tn=tn2,
                n_out=D_MODEL, out_dtype=F32, name="ffn_down"), None
        else:
            i = layer // 2
            w_router = jnp.pad(router[i], ((0, 0), (0, LANES - N_EXPERTS)))
            h2_chunks, route = _norm_route(x_mid, norm2, mod, layer, 3, 4, w_router, n_rows)
            pos, tok_of_row, tile_expert, tile_rows, n_used = _routing_tables(route)
            rows = _dispatch(h2_chunks, tok_of_row, n_used)
            moe_tiles = MOE_ROWS // MOE_TILE
            n_stack = moe_w1.shape[0] * N_EXPERTS
            w1s = moe_w1.reshape(n_stack, D_MODEL, D_FF_EXPERT)
            w3s = moe_w3.reshape(n_stack, D_MODEL, D_FF_EXPERT)
            w2s = moe_w2.reshape(n_stack, D_FF_EXPERT, D_MODEL)
            tile_group = tile_expert + i * N_EXPERTS
            hidden = _ws_matmul(
                [rows], [w1s, w3s], pairs=[(0, 0), (0, 1)], epilogue=_ep_swiglu,
                group=tile_group, tile_rows=tile_rows, m_tiles=moe_tiles, tm=MOE_TILE, tn=1024,
                n_out=D_FF_EXPERT, out_dtype=BF16, name="moe_up")
            y_rows = _ws_matmul(
                [hidden], [w2s], pairs=[(0, 0)], epilogue=_ep_plain,
                group=tile_group, tile_rows=tile_rows, m_tiles=moe_tiles, tm=MOE_TILE, tn=512,
                n_out=D_MODEL, out_dtype=F32, name="moe_down")
            out = _combine(y_rows, pos, x_mid, route, mod, layer, 5, norm_f)

    return out.reshape(BATCH, SEQ, D_MODEL)
```

```python
import functools

import numpy as np
import jax
import jax.numpy as jnp
from jax import lax
from jax.experimental import pallas as pl
from jax.experimental.pallas import tpu as pltpu

D_MODEL = 2048
BATCH = 4
SEQ = 2048
DEPTH = 2
GRID_W = 64
CTX_LEN = 256
N_HEADS = 16
N_KV_HEADS = 4
GROUP = N_HEADS // N_KV_HEADS
HEAD_DIM = 128
WINDOW = 128
BLOCK = 128
ROPE_THETA = 10000.0
D_FF = 5632
N_EXPERTS = 8
TOP_K = 2
D_FF_EXPERT = 7168
RMS_EPS = 1e-6
NEG_INF = -1e30
LOG2_E = 1.4426950408889634
N_MOD = 6
Q_W = N_HEADS * HEAD_DIM
KV_W = N_KV_HEADS * HEAD_DIM
QKV_W = Q_W + 2 * KV_W
REST_W = 3 * D_MODEL + 2 * D_MODEL

N_LAT = BATCH * SEQ
N_CTX = BATCH * CTX_LEN
N_ALL = N_LAT + N_CTX
CTX_MOD_ROW = BATCH
MOD_ROWS = 8

LANES = 128
VMEM_LIMIT = 60 * 1024 * 1024
ROW_TILE = 1024
MOE_TILE = 512
MOE_ROWS = TOP_K * N_LAT + N_EXPERTS * MOE_TILE
PART_ROWS = 128

F32 = jnp.float32
BF16 = jnp.bfloat16


def _params(n_axes):
    return pltpu.CompilerParams(
        dimension_semantics=("arbitrary",) * n_axes, vmem_limit_bytes=VMEM_LIMIT)


def _sigmoid(v):
    return 1.0 / (1.0 + jnp.exp(-v))


def _mod_row(m_tile_start_row):
    return jnp.minimum(m_tile_start_row // SEQ, CTX_MOD_ROW)


def _modulation_kernel(c_ref, w_ref, b_ref, o_ref):
    cv = c_ref[...]
    act = (cv * _sigmoid(cv)).astype(BF16)
    o_ref[...] = jnp.dot(act, w_ref[...].astype(BF16), preferred_element_type=F32) + b_ref[...]


def _modulation(cvec, w_mod, b_mod):
    tn = 1024
    n_tiles = N_MOD * D_MODEL // tn
    return pl.pallas_call(
        _modulation_kernel,
        grid=(DEPTH, n_tiles),
        in_specs=[
            pl.BlockSpec((MOD_ROWS, D_MODEL), lambda l, n: (0, 0)),
            pl.BlockSpec((None, D_MODEL, tn), lambda l, n: (l, 0, n)),
            pl.BlockSpec((None, 1, tn), lambda l, n: (l, 0, n)),
        ],
        out_specs=pl.BlockSpec((None, MOD_ROWS, tn), lambda l, n: (l, 0, n)),
        out_shape=jax.ShapeDtypeStruct((DEPTH, MOD_ROWS, N_MOD * D_MODEL), F32),
        compiler_params=_params(2),
        name="modulation",
    )(cvec, w_mod, b_mod.reshape(DEPTH, 1, N_MOD * D_MODEL))


def _norm_kernel(x_ref, *refs, tm, tail_tiles):
    m = pl.program_id(0)
    row = _mod_row(m * tm)
    if tail_tiles is None:
        g_ref, sh_ref, sc_ref, o_ref = refs
        xf = x_ref[...]
    else:
        x_tail_ref, g_ref, sh_ref, sc_ref, o_ref = refs
        xf = jnp.where(m < tail_tiles, x_ref[...], x_tail_ref[...])
    inv = lax.rsqrt(jnp.mean(xf * xf, axis=-1, keepdims=True) + RMS_EPS)
    y = (xf * inv) * g_ref[...]
    y = y * (1.0 + sc_ref[pl.ds(row, 1), :]) + sh_ref[pl.ds(row, 1), :]
    o_ref[...] = y.astype(o_ref.dtype)


def _norm_mod(x_rows, gain, mod, layer, shift_chunk, scale_chunk, n_rows, out_dtype, x_tail=None):
    tm = 512
    xs = [x_rows]
    x_specs = [pl.BlockSpec((tm, D_MODEL), lambda m: (m, 0))]
    tail_tiles = None
    if x_tail is not None:
        tail_tiles = x_rows.shape[0] // tm
        xs.append(x_tail)
        x_specs = [pl.BlockSpec((tm, D_MODEL), lambda m: (jnp.minimum(m, tail_tiles - 1), 0)),
                   pl.BlockSpec((tm, D_MODEL), lambda m: (jnp.maximum(m - tail_tiles, 0), 0))]
    return pl.pallas_call(
        functools.partial(_norm_kernel, tm=tm, tail_tiles=tail_tiles),
        grid=(n_rows // tm,),
        in_specs=x_specs + [
            pl.BlockSpec((None, 1, D_MODEL), lambda m: (layer, 0, 0)),
            pl.BlockSpec((None, MOD_ROWS, D_MODEL), lambda m: (layer, 0, shift_chunk)),
            pl.BlockSpec((None, MOD_ROWS, D_MODEL), lambda m: (layer, 0, scale_chunk)),
        ],
        out_specs=pl.BlockSpec((tm, D_MODEL), lambda m: (m, 0)),
        out_shape=jax.ShapeDtypeStruct((n_rows, D_MODEL), out_dtype),
        compiler_params=_params(1),
        name="norm_mod",
    )(*xs, gain.reshape(DEPTH, 1, D_MODEL), mod, mod)


def _ws_kernel(group_ref, first_ref, *refs, n_lhs, n_w, n_extra, pairs, epilogue, tail_tiles, tm,
               n_sub, partial):
    del group_ref
    lhs_refs = refs[:n_lhs]
    w_refs = refs[n_lhs:n_lhs + n_w]
    ex_refs = refs[n_lhs + n_w:n_lhs + n_w + n_extra]
    out_ref = refs[n_lhs + n_w + n_extra]
    n = pl.program_id(0)
    m = pl.program_id(1)

    def lhs_tile(li, rs):
        if li == 0 and tail_tiles is not None:
            return jnp.where(m < tail_tiles, lhs_refs[0][rs, :], lhs_refs[n_lhs - 1][rs, :])
        return lhs_refs[li][rs, :]

    def compute(row_slices):
        w_bf = [w_ref[...].astype(BF16) for w_ref in w_refs]
        for rs in row_slices:
            dots = [jnp.dot(lhs_tile(li, rs), w_bf[wi], preferred_element_type=F32)
                    for li, wi in pairs]
            out_ref[rs, :] = epilogue(dots, ex_refs, n, m, rs).astype(out_ref.dtype)

    def slices(size):
        return [slice(r, r + size) for r in range(0, tm, size)]

    if not partial:
        compute(slices(tm // n_sub))
        return

    first = first_ref[m]

    @pl.when(first == 0)
    def _():
        compute(slices(tm // n_sub))

    for rs in slices(PART_ROWS):
        @pl.when((first > 0) & (rs.stop > first))
        def _():
            compute([rs])

        @pl.when((first > 0) & (rs.stop <= first))
        def _():
            out_ref[rs, :] = jnp.zeros((PART_ROWS, out_ref.shape[1]), out_ref.dtype)


def _ws_matmul(lhs, weights, *, pairs, epilogue, extras=(), group, tile_first=None, m_tiles, tm, tn,
               n_out, col_off=0, lhs_row_off=0, lhs0_tail=None, n_sub=1, out_dtype, name):
    k_dim = lhs[0].shape[1]
    col_blk = col_off
    row_blk = lhs_row_off
    partial = tile_first is not None
    if tile_first is None:
        tile_first = jnp.zeros((m_tiles,), jnp.int32)

    def wrap(fn):
        return lambda n, m, g, u: fn(n, m)

    in_specs = [pl.BlockSpec((tm, k_dim), lambda n, m, g, u: (m + row_blk, 0)) for _ in lhs]
    tail_tiles = None
    if lhs0_tail is not None:
        tail_tiles = lhs[0].shape[0] // tm
        in_specs[0] = pl.BlockSpec(
            (tm, k_dim), lambda n, m, g, u: (jnp.minimum(m, tail_tiles - 1), 0))
        in_specs.append(pl.BlockSpec(
            (tm, k_dim), lambda n, m, g, u: (jnp.maximum(m - tail_tiles, 0), 0)))
        lhs = list(lhs) + [lhs0_tail]
    in_specs += [pl.BlockSpec((None, k_dim, tn), lambda n, m, g, u: (g[m], 0, n + col_blk))
                 for _ in weights]
    in_specs += [pl.BlockSpec(shape, wrap(fn)) for _, shape, fn in extras]
    grid_spec = pltpu.PrefetchScalarGridSpec(
        num_scalar_prefetch=2,
        grid=(n_out // tn, m_tiles),
        in_specs=in_specs,
        out_specs=pl.BlockSpec((tm, tn), lambda n, m, g, u: (m, n)),
    )
    kern = functools.partial(
        _ws_kernel, n_lhs=len(lhs), n_w=len(weights), n_extra=len(extras), pairs=pairs,
        epilogue=epilogue, tail_tiles=tail_tiles, tm=tm, n_sub=n_sub, partial=partial)
    return pl.pallas_call(
        kern,
        grid_spec=grid_spec,
        out_shape=jax.ShapeDtypeStruct((m_tiles * tm, n_out), out_dtype),
        compiler_params=_params(2),
        name=name,
    )(group, tile_first, *lhs, *weights, *[e[0] for e in extras])


def _layer_group(layer, m_tiles):
    return jnp.full((m_tiles,), layer, jnp.int32)


def _ep_plain(dots, ex, n, m, rs):
    return dots[0]


def _ep_rope(dots, ex, n, m, rs):
    cos_ref, sa_ref, sb_ref = ex
    d = dots[0]
    cos, sa, sb = cos_ref[rs, :], sa_ref[rs, :], sb_ref[rs, :]
    quarter = HEAD_DIM // 4
    heads = []
    for h in range(d.shape[1] // HEAD_DIM):
        t = d[:, h * HEAD_DIM:(h + 1) * HEAD_DIM]
        heads.append(t * cos + pltpu.roll(t, HEAD_DIM - quarter, 1) * sa
                     + pltpu.roll(t, quarter, 1) * sb)
    return jnp.concatenate(heads, axis=1)


def _ep_merge(dots, ex, n, m, rs):
    ga_ref, gc_ref = ex
    return (_sigmoid(ga_ref[rs, :].astype(F32)) * dots[0]
            + _sigmoid(gc_ref[rs, :].astype(F32)) * dots[1])


def _ep_residual(dots, ex, n, m, rs, *, tm, tail_tiles=None):
    if tail_tiles is None:
        x_ref, gate_ref = ex
        x = x_ref[rs, :]
    else:
        x_ref, x_tail_ref, gate_ref = ex
        x = jnp.where(m < tail_tiles, x_ref[rs, :], x_tail_ref[rs, :])
    row = _mod_row(m * tm)
    return x + gate_ref[pl.ds(row, 1), :] * dots[0]


def _ep_swiglu(dots, ex, n, m, rs):
    a = dots[0]
    return (a * _sigmoid(a)) * dots[1]


def _rope_tables(tm):
    quarter = HEAD_DIM // 4
    inv_freq = np.power(np.float32(ROPE_THETA),
                        -np.arange(quarter, dtype=np.float32) / np.float32(quarter)).astype(np.float32)
    pos = np.arange(SEQ)
    rows = (pos // GRID_W).astype(np.float32)
    cols = (pos % GRID_W).astype(np.float32)
    ang = np.concatenate([rows[:, None] * inv_freq[None, :]] * 2
                         + [cols[:, None] * inv_freq[None, :]] * 2, axis=1).astype(np.float32)
    cos = np.cos(ang).astype(np.float32)
    sin = np.sin(ang).astype(np.float32)
    lane = np.arange(HEAD_DIM)
    first = ((lane // quarter) % 2 == 0)[None, :]
    sa = np.where(first, -sin, 0.0).astype(np.float32)
    sb = np.where(first, 0.0, sin).astype(np.float32)
    ones = np.ones((tm, HEAD_DIM), np.float32)
    zeros = np.zeros((tm, HEAD_DIM), np.float32)
    scale = np.float32(HEAD_DIM ** -0.5 * LOG2_E)

    def variants(tbl, ident):
        k = np.concatenate([tbl, ident], axis=0)
        ident_all = np.concatenate([np.broadcast_to(ident[:1], tbl.shape), ident], axis=0)
        return np.stack([k * scale, k, ident_all])

    return (jnp.asarray(variants(cos, ones)), jnp.asarray(variants(sa, zeros)),
            jnp.asarray(variants(sb, zeros)))


ATTN_ROWS = 2 * BLOCK
BAND = 3 * BLOCK


def _band_masks():
    r = np.arange(BLOCK)[:, None]
    c = np.arange(BAND)[None, :]
    masks = [(np.abs(off * BLOCK + r - c) <= WINDOW).astype(np.float32) for off in range(3)]
    return jnp.asarray(np.stack([np.tile(m, (GROUP, 1)) for m in masks]))


def _stack_heads(q):
    return jnp.concatenate([q[:, h * HEAD_DIM:(h + 1) * HEAD_DIM] for h in range(GROUP)], axis=0)


def _unstack_heads(o):
    return jnp.concatenate([o[h * BLOCK:(h + 1) * BLOCK, :] for h in range(GROUP)], axis=1)


def _sink_column(sink_ref, g):
    return LOG2_E * jnp.concatenate(
        [jnp.broadcast_to(sink_ref[pl.ds(g * GROUP + h, 1), 0:1], (BLOCK, 1)) for h in range(GROUP)],
        axis=0)


def _softmax_pv(s, sink, vals):
    mx = jnp.maximum(jnp.max(s, axis=-1, keepdims=True), sink)
    p = jnp.exp2(s - mx)
    denom = jnp.sum(p, axis=-1, keepdims=True) + jnp.exp2(sink - mx)
    return jnp.dot(p.astype(BF16), vals, preferred_element_type=F32) / denom


_CONTRACT_LAST = (((1,), (1,)), ((), ()))


def _attn_latent_kernel(sink_ref, mask_ref, q_ref, k_ref, v_ref, kx_ref, vx_ref, o_ref, s_even, s_odd):
    n_blocks = SEQ // BLOCK
    sink = _sink_column(sink_ref, pl.program_id(1))

    def band_start(i):
        first = jnp.clip(i - 1, 0, n_blocks - 3)
        return first, pl.multiple_of(first * BLOCK, BLOCK)

    def scores(i, s_ref):
        first, start = band_start(i)
        qs = _stack_heads(q_ref[pl.ds(pl.multiple_of(i * BLOCK, BLOCK), BLOCK), :])
        s_loc = lax.dot_general(qs, k_ref[pl.ds(start, BAND), :], _CONTRACT_LAST,
                                preferred_element_type=F32)
        s_ref[:, 0:BAND] = jnp.where(mask_ref[i - first] > 0.5, s_loc, NEG_INF)
        s_ref[:, BAND:] = lax.dot_general(qs, kx_ref[...], _CONTRACT_LAST,
                                          preferred_element_type=F32)

    def finish(i, s_ref):
        _, start = band_start(i)
        vals = jnp.concatenate([v_ref[pl.ds(start, BAND), :], vx_ref[...]], axis=0)
        o = _softmax_pv(s_ref[...], sink, vals)
        o_ref[pl.ds(pl.multiple_of(i * BLOCK, BLOCK), BLOCK), :] = _unstack_heads(o).astype(o_ref.dtype)

    scores(0, s_even)

    def pair(t, carry):
        i = 2 * t
        scores(i + 1, s_odd)
        finish(i, s_even)
        scores(jnp.minimum(i + 2, n_blocks - 1), s_even)
        finish(i + 1, s_odd)
        return carry

    lax.fori_loop(0, n_blocks // 2, pair, 0)


def _attn_ctx_kernel(sink_ref, q_ref, kx_ref, vx_ref, o_ref):
    sink = _sink_column(sink_ref, pl.program_id(1))
    for sub in range(CTX_LEN // BLOCK):
        rows = slice(sub * BLOCK, (sub + 1) * BLOCK)
        qs = _stack_heads(q_ref[rows, :])
        s = lax.dot_general(qs, kx_ref[...], _CONTRACT_LAST, preferred_element_type=F32)
        o_ref[rows, :] = _unstack_heads(_softmax_pv(s, sink, vx_ref[...])).astype(o_ref.dtype)


def _attention(qkv, kv_ctx, sink_tbl, masks, *, with_ctx_queries, ctx_row_blk, ctx_k_col, ctx_v_col):
    k_col = Q_W // HEAD_DIM
    v_col = (Q_W + KV_W) // HEAD_DIM
    width = GROUP * HEAD_DIM
    sink_spec = pl.BlockSpec((N_HEADS, LANES), lambda b, g: (0, 0))
    kx_spec = pl.BlockSpec((CTX_LEN, HEAD_DIM), lambda b, g: (ctx_row_blk + b, ctx_k_col + g))
    vx_spec = pl.BlockSpec((CTX_LEN, HEAD_DIM), lambda b, g: (ctx_row_blk + b, ctx_v_col + g))
    y = pl.pallas_call(
        _attn_latent_kernel,
        grid=(BATCH, N_KV_HEADS),
        in_specs=[
            sink_spec,
            pl.BlockSpec((3, GROUP * BLOCK, BAND), lambda b, g: (0, 0, 0)),
            pl.BlockSpec((SEQ, width), lambda b, g: (b, g)),
            pl.BlockSpec((SEQ, HEAD_DIM), lambda b, g: (b, k_col + g)),
            pl.BlockSpec((SEQ, HEAD_DIM), lambda b, g: (b, v_col + g)),
            kx_spec, vx_spec,
        ],
        out_specs=pl.BlockSpec((SEQ, width), lambda b, g: (b, g)),
        out_shape=jax.ShapeDtypeStruct((N_LAT, Q_W), BF16),
        scratch_shapes=[pltpu.VMEM((GROUP * BLOCK, BAND + CTX_LEN), F32),
                        pltpu.VMEM((GROUP * BLOCK, BAND + CTX_LEN), F32)],
        compiler_params=_params(2),
        name="attention",
    )(sink_tbl, masks, qkv, qkv, qkv, kv_ctx, kv_ctx)
    if not with_ctx_queries:
        return y, None
    ctx_blk = N_LAT // CTX_LEN
    y_ctx = pl.pallas_call(
        _attn_ctx_kernel,
        grid=(BATCH, N_KV_HEADS),
        in_specs=[
            sink_spec,
            pl.BlockSpec((CTX_LEN, width), lambda b, g: (ctx_blk + b, g)),
            kx_spec, vx_spec,
        ],
        out_specs=pl.BlockSpec((CTX_LEN, width), lambda b, g: (b, g)),
        out_shape=jax.ShapeDtypeStruct((N_CTX, Q_W), BF16),
        compiler_params=_params(2),
        name="attention_ctx",
    )(sink_tbl, qkv, kv_ctx, kv_ctx)
    return y, y_ctx


CONV_ROWS = 256
HALO = 16


def _conv_kernel(w_ref, u_ref, gb_ref, gc_ref, up_ref, gcp_ref, un_ref, gcn_ref, o_ref):
    r = pl.program_id(0)
    lat_blocks = N_LAT // CONV_ROWS
    per_seq = SEQ // CONV_ROWS
    is_ctx = r >= lat_blocks
    seq_start = is_ctx | (r % per_seq == 0)
    seq_end = is_ctx | (r % per_seq == per_seq - 1)
    z = gc_ref[...].astype(F32) * u_ref[...].astype(F32)
    z_before = (gcp_ref[HALO - 1:HALO, :].astype(F32) * up_ref[HALO - 1:HALO, :].astype(F32))
    z_after = gcn_ref[0:1, :].astype(F32) * un_ref[0:1, :].astype(F32)
    z_before = jnp.where(seq_start, 0.0, z_before)
    z_after = jnp.where(seq_end, 0.0, z_after)
    row = lax.broadcasted_iota(jnp.int32, z.shape, 0)
    z_prev = jnp.where(row == 0, z_before, pltpu.roll(z, 1, 0))
    z_next = jnp.where(row == CONV_ROWS - 1, z_after, pltpu.roll(z, CONV_ROWS - 1, 0))
    y = z_prev * w_ref[0:1, :] + z * w_ref[1:2, :] + z_next * w_ref[2:3, :]
    o_ref[...] = (gb_ref[...].astype(F32) * y).astype(o_ref.dtype)


def _short_conv(rest, conv_w, layer, n_rows):
    n_blocks = n_rows // CONV_ROWS
    tc = D_MODEL
    halo_per_block = CONV_ROWS // HALO
    last_halo = n_rows // HALO - 1

    def main(chunk):
        return pl.BlockSpec((CONV_ROWS, tc), lambda r: (r, chunk))

    def before(chunk):
        return pl.BlockSpec((HALO, tc), lambda r: (jnp.maximum(r * halo_per_block - 1, 0), chunk))

    def after(chunk):
        return pl.BlockSpec((HALO, tc),
                            lambda r: (jnp.minimum((r + 1) * halo_per_block, last_halo), chunk))

    return pl.pallas_call(
        _conv_kernel,
        grid=(n_blocks,),
        in_specs=[
            pl.BlockSpec((None, 8, tc), lambda r: (layer, 0, 0)),
            main(0), main(1), main(2), before(0), before(2), after(0), after(2),
        ],
        out_specs=pl.BlockSpec((CONV_ROWS, tc), lambda r: (r, 0)),
        out_shape=jax.ShapeDtypeStruct((n_rows, D_MODEL), BF16),
        compiler_params=_params(1),
        name="short_conv",
    )(conv_w, rest, rest, rest, rest, rest, rest, rest)


ROW_CHUNKS = D_MODEL // LANES


PACKED_CHUNKS = ROW_CHUNKS // 2


def _norm_route_kernel(x_ref, g_ref, sh_ref, sc_ref, w_ref, packed_ref, route_ref, *, tm):
    row = _mod_row(pl.program_id(0) * tm)
    xf = x_ref[...]
    inv = lax.rsqrt(jnp.mean(xf * xf, axis=-1, keepdims=True) + RMS_EPS)
    h = (xf * inv) * g_ref[...]
    h = h * (1.0 + sc_ref[pl.ds(row, 1), :]) + sh_ref[pl.ds(row, 1), :]
    bits = lax.bitcast_convert_type(h.astype(BF16).astype(F32), jnp.uint32)
    for j in range(PACKED_CHUNKS):
        low = bits[:, j * LANES:(j + 1) * LANES] >> 16
        high = bits[:, (j + PACKED_CHUNKS) * LANES:(j + PACKED_CHUNKS + 1) * LANES]
        packed_ref[pl.ds(j, tm, stride=PACKED_CHUNKS), :] = low | high
    route_ref[...] = _route(h, w_ref[...])


def _norm_route(x_rows, gain, mod, layer, shift_chunk, scale_chunk, w_router_padded, n_rows):
    tm = 512
    return pl.pallas_call(
        functools.partial(_norm_route_kernel, tm=tm),
        grid=(n_rows // tm,),
        in_specs=[
            pl.BlockSpec((tm, D_MODEL), lambda m: (m, 0)),
            pl.BlockSpec((None, 1, D_MODEL), lambda m: (layer, 0, 0)),
            pl.BlockSpec((None, MOD_ROWS, D_MODEL), lambda m: (layer, 0, shift_chunk)),
            pl.BlockSpec((None, MOD_ROWS, D_MODEL), lambda m: (layer, 0, scale_chunk)),
            pl.BlockSpec((D_MODEL, LANES), lambda m: (0, 0)),
        ],
        out_specs=[pl.BlockSpec((tm * PACKED_CHUNKS, LANES), lambda m: (m, 0)),
                   pl.BlockSpec((tm, LANES), lambda m: (m, 0))],
        out_shape=[jax.ShapeDtypeStruct((n_rows * PACKED_CHUNKS, LANES), jnp.uint32),
                   jax.ShapeDtypeStruct((n_rows, LANES), F32)],
        compiler_params=_params(1),
        name="norm_route",
    )(x_rows, gain.reshape(DEPTH, 1, D_MODEL), mod, mod, w_router_padded)


def _route(h, w):
    h_hi = h.astype(BF16)
    h_lo = (h - h_hi.astype(F32)).astype(BF16)
    w_hi = w.astype(BF16)
    w_lo = (w - w_hi.astype(F32)).astype(BF16)
    logits = (jnp.dot(h_hi, w_hi, preferred_element_type=F32)
              + jnp.dot(h_hi, w_lo, preferred_element_type=F32)
              + jnp.dot(h_lo, w_hi, preferred_element_type=F32))
    lane = lax.broadcasted_iota(jnp.int32, logits.shape, 1)
    lowest = float(jnp.finfo(F32).min)
    logits = jnp.where(lane < N_EXPERTS, logits, lowest)
    v1 = jnp.max(logits, axis=-1, keepdims=True)
    i1 = jnp.min(jnp.where(logits == v1, lane, LANES), axis=-1, keepdims=True)
    rest = jnp.where(lane == i1, lowest, logits)
    v2 = jnp.max(rest, axis=-1, keepdims=True)
    i2 = jnp.min(jnp.where(rest == v2, lane, LANES), axis=-1, keepdims=True)
    e2 = jnp.exp(v2 - v1)
    p1 = 1.0 / (1.0 + e2)
    p2 = e2 / (1.0 + e2)
    return jnp.where(lane == 0, i1.astype(F32),
                     jnp.where(lane == 1, i2.astype(F32),
                               jnp.where(lane == 2, p1, jnp.where(lane == 3, p2, 0.0))))


DISPATCH_TILE = 256


DMA_UNROLL = 8


def _dispatch_kernel(nused_ref, tok_ref, tok_next_ref, h_hbm, o_ref, buf, sem):
    t = pl.program_id(0)
    slot = t % 2
    n_active = nused_ref[0] * (MOE_TILE // DISPATCH_TILE)
    tile_words = DISPATCH_TILE * PACKED_CHUNKS

    def gather(idx_ref, dst_slot):
        def body(i, carry):
            for u in range(DMA_UNROLL):
                j = i * DMA_UNROLL + u
                src = pl.multiple_of(idx_ref[0, j] * PACKED_CHUNKS, PACKED_CHUNKS)
                pltpu.make_async_copy(
                    h_hbm.at[pl.ds(src, PACKED_CHUNKS), :],
                    buf.at[dst_slot, pl.ds(j * PACKED_CHUNKS, PACKED_CHUNKS), :],
                    sem.at[dst_slot]).start(priority=u % 2)
            return carry
        lax.fori_loop(0, DISPATCH_TILE // DMA_UNROLL, body, 0)

    @pl.when((t == 0) & (n_active > 0))
    def _():
        gather(tok_ref, 0)

    @pl.when(t + 1 < n_active)
    def _():
        gather(tok_next_ref, 1 - slot)

    @pl.when(t < n_active)
    def _():
        pltpu.make_async_copy(h_hbm.at[pl.ds(0, tile_words), :], buf.at[slot], sem.at[slot]).wait()
        words = [buf[slot, pl.ds(j, DISPATCH_TILE, stride=PACKED_CHUNKS), :]
                 for j in range(PACKED_CHUNKS)]
        low = [lax.bitcast_convert_type(w << 16, F32) for w in words]
        high = [lax.bitcast_convert_type(w & jnp.uint32(0xFFFF0000), F32) for w in words]
        o_ref[...] = jnp.concatenate(low + high, axis=1).astype(o_ref.dtype)

    @pl.when(t >= n_active)
    def _():
        o_ref[...] = jnp.zeros_like(o_ref)


def _dispatch(h_packed, tok_of_row, n_used):
    n_rows = tok_of_row.shape[0]
    n_steps = n_rows // DISPATCH_TILE
    tok_blocks = tok_of_row.reshape(n_steps, 1, DISPATCH_TILE)
    grid_spec = pltpu.PrefetchScalarGridSpec(
        num_scalar_prefetch=1,
        grid=(n_steps,),
        in_specs=[
            pl.BlockSpec((None, 1, DISPATCH_TILE), lambda t, u: (t, 0, 0), memory_space=pltpu.SMEM),
            pl.BlockSpec((None, 1, DISPATCH_TILE),
                         lambda t, u: (jnp.minimum(t + 1, n_steps - 1), 0, 0),
                         memory_space=pltpu.SMEM),
            pl.BlockSpec(memory_space=pl.ANY),
        ],
        out_specs=pl.BlockSpec((DISPATCH_TILE, D_MODEL), lambda t, u: (t, 0)),
        scratch_shapes=[pltpu.VMEM((2, DISPATCH_TILE * PACKED_CHUNKS, LANES), jnp.uint32),
                        pltpu.SemaphoreType.DMA((2,))],
    )
    return pl.pallas_call(
        _dispatch_kernel,
        grid_spec=grid_spec,
        out_shape=jax.ShapeDtypeStruct((n_rows, D_MODEL), BF16),
        compiler_params=_params(1),
        name="moe_dispatch",
    )(n_used, tok_blocks, tok_blocks, h_packed)


COMBINE_TILE = 256


def _combine_kernel(pos_ref, pos_next_ref, y_hbm, x_ref, route_ref, gate_ref, gain_ref, o_ref,
                    buf, sem):
    t = pl.program_id(0)
    n_steps = pl.num_programs(0)
    slot = t % 2

    def gather(idx_ref, dst_slot):
        def body(i, carry):
            for u in range(DMA_UNROLL):
                j = i * DMA_UNROLL + u
                for k in range(TOP_K):
                    pltpu.make_async_copy(
                        y_hbm.at[pl.ds(idx_ref[0, TOP_K * j + k], 1), :],
                        buf.at[dst_slot, k, pl.ds(j, 1), :],
                        sem.at[dst_slot]).start(priority=k)
            return carry
        lax.fori_loop(0, COMBINE_TILE // DMA_UNROLL, body, 0)

    @pl.when(t == 0)
    def _():
        gather(pos_ref, 0)

    @pl.when(t + 1 < n_steps)
    def _():
        gather(pos_next_ref, 1 - slot)

    for k in range(TOP_K):
        pltpu.make_async_copy(
            y_hbm.at[pl.ds(0, COMBINE_TILE), :], buf.at[slot, k], sem.at[slot]).wait()
    route = route_ref[...]
    moe = route[:, 2:3] * buf[slot, 0] + route[:, 3:4] * buf[slot, 1]
    row = _mod_row(t * COMBINE_TILE)
    xn = x_ref[...] + gate_ref[pl.ds(row, 1), :] * moe
    inv = lax.rsqrt(jnp.mean(xn * xn, axis=-1, keepdims=True) + RMS_EPS)
    o_ref[...] = (xn * inv) * gain_ref[...]


def _combine(y_rows, pos, x_rows, route, mod, layer, gate_chunk, norm_f):
    n_steps = N_LAT // COMBINE_TILE
    pos_blocks = pos.reshape(n_steps, 1, TOP_K * COMBINE_TILE)
    idx_shape = (None, 1, TOP_K * COMBINE_TILE)
    return pl.pallas_call(
        _combine_kernel,
        grid=(n_steps,),
        in_specs=[
            pl.BlockSpec(idx_shape, lambda t: (t, 0, 0), memory_space=pltpu.SMEM),
            pl.BlockSpec(idx_shape, lambda t: (jnp.minimum(t + 1, n_steps - 1), 0, 0),
                         memory_space=pltpu.SMEM),
            pl.BlockSpec(memory_space=pl.ANY),
            pl.BlockSpec((COMBINE_TILE, D_MODEL), lambda t: (t, 0)),
            pl.BlockSpec((COMBINE_TILE, LANES), lambda t: (t, 0)),
            pl.BlockSpec((None, MOD_ROWS, D_MODEL), lambda t: (layer, 0, gate_chunk)),
            pl.BlockSpec((1, D_MODEL), lambda t: (0, 0)),
        ],
        out_specs=pl.BlockSpec((COMBINE_TILE, D_MODEL), lambda t: (t, 0)),
        out_shape=jax.ShapeDtypeStruct((N_LAT, D_MODEL), F32),
        scratch_shapes=[pltpu.VMEM((2, TOP_K, COMBINE_TILE, D_MODEL), F32),
                        pltpu.SemaphoreType.DMA((2,))],
        compiler_params=_params(1),
        name="moe_combine",
    )(pos_blocks, pos_blocks, y_rows, x_rows, route, mod, norm_f.reshape(1, D_MODEL))


def _routing_tables(route):
    experts = route[:, :TOP_K].astype(jnp.int32).reshape(-1)
    onehot = (experts[:, None] == jnp.arange(N_EXPERTS, dtype=jnp.int32)[None, :]).astype(jnp.int32)
    rank = jnp.sum((jnp.cumsum(onehot, axis=0) - onehot) * onehot, axis=1)
    counts = jnp.sum(onehot, axis=0)
    padded = ((counts + MOE_TILE - 1) // MOE_TILE) * MOE_TILE
    ends = jnp.cumsum(padded)
    starts = ends - counts
    pos = (jnp.sum(onehot * starts[None, :], axis=1) + rank).astype(jnp.int32)
    tok_of_row = jnp.zeros((MOE_ROWS,), jnp.int32).at[pos].set(
        jnp.arange(TOP_K * N_LAT, dtype=jnp.int32) // TOP_K)
    n_tiles = MOE_ROWS // MOE_TILE
    n_used = (ends[-1] // MOE_TILE).astype(jnp.int32)
    tile_start = jnp.arange(n_tiles, dtype=jnp.int32) * MOE_TILE
    tile_expert = jnp.sum((tile_start[:, None] >= ends[None, :]).astype(jnp.int32), axis=1)
    last_expert = jnp.sum((tile_start[jnp.maximum(n_used - 1, 0)] >= ends).astype(jnp.int32))
    in_use = jnp.arange(n_tiles) < n_used
    tile_expert = jnp.where(in_use, tile_expert, last_expert)
    tile_expert = jnp.minimum(tile_expert, N_EXPERTS - 1).astype(jnp.int32)
    tile_first = jnp.clip(starts[tile_expert] - tile_start, 0, MOE_TILE)
    tile_first = jnp.where(in_use, tile_first, MOE_TILE).astype(jnp.int32)
    return pos, tok_of_row, tile_expert, tile_first, n_used.reshape(1)


def kernel(x, c, ctx, c_ctx, w_mod, b_mod, norm1, w_in, sink, conv_w, w_o_attn, w_o_conv, w_out,
           norm2, ffn_w1, ffn_w3, ffn_w2, router, moe_w1, moe_w3, moe_w2, norm_f):
    tm = ROW_TILE
    cvec = jnp.concatenate(
        [c, c_ctx[None, :], jnp.zeros((MOD_ROWS - BATCH - 1, D_MODEL), F32)], axis=0)
    mod = _modulation(cvec, w_mod, b_mod)
    x_main, x_tail = x.reshape(N_LAT, D_MODEL), ctx.reshape(N_CTX, D_MODEL)
    cos_t, sa_t, sb_t = _rope_tables(tm)
    conv_w8 = jnp.pad(conv_w, ((0, 0), (0, 8 - conv_w.shape[1]), (0, 0)))
    sink_tbl = jnp.broadcast_to(sink[:, :, None], (DEPTH, N_HEADS, LANES))
    masks = _band_masks()
    seq_tiles = SEQ // tm
    lat_tiles = N_LAT // tm
    out = None

    for layer in range(DEPTH):
        ctx_out = layer < DEPTH - 1
        n_rows = N_ALL if ctx_out else N_LAT
        m_tiles = n_rows // tm
        grp = _layer_group(layer, m_tiles)
        chunk = D_MODEL // 512

        h = _norm_mod(x_main, norm1, mod, layer, 0, 1, N_ALL, BF16, x_tail=x_tail)
        tn_qkv = 512

        def variant(n):
            return jnp.where(n < Q_W // tn_qkv, 0, jnp.where(n < (Q_W + KV_W) // tn_qkv, 1, 2))

        def tbl_map(n, m):
            return (variant(n), jnp.where(m < lat_tiles, m % seq_tiles, seq_tiles), 0)

        tbl_shape = (None, tm, HEAD_DIM)
        qkv = _ws_matmul(
            [h], [w_in], pairs=[(0, 0)], epilogue=_ep_rope,
            extras=[(cos_t, tbl_shape, tbl_map), (sa_t, tbl_shape, tbl_map),
                    (sb_t, tbl_shape, tbl_map)],
            group=grp, m_tiles=m_tiles, tm=tm, tn=tn_qkv, n_out=QKV_W, n_sub=2,
            out_dtype=BF16, name="in_proj_qkv")
        rest = _ws_matmul(
            [h], [w_in], pairs=[(0, 0)], epilogue=_ep_plain,
            group=grp, m_tiles=m_tiles, tm=tm, tn=1024, n_out=REST_W,
            col_off=QKV_W // 1024, out_dtype=BF16, name="in_proj_rest")
        if ctx_out:
            kv_ctx, ctx_row_blk = qkv, N_LAT // CTX_LEN
            ctx_k_col, ctx_v_col = Q_W // HEAD_DIM, (Q_W + KV_W) // HEAD_DIM
        else:
            kv_ctx = _ws_matmul(
                [h], [w_in], pairs=[(0, 0)], epilogue=_ep_plain,
                group=_layer_group(layer, N_CTX // tm), m_tiles=N_CTX // tm, tm=tm, tn=512,
                n_out=2 * KV_W, col_off=Q_W // 512, lhs_row_off=lat_tiles, out_dtype=BF16,
                name="in_proj_ctx_kv")
            ctx_row_blk, ctx_k_col, ctx_v_col = 0, 0, KV_W // HEAD_DIM
        y_attn, y_attn_ctx = _attention(
            qkv, kv_ctx, sink_tbl[layer], masks, with_ctx_queries=ctx_out,
            ctx_row_blk=ctx_row_blk, ctx_k_col=ctx_k_col, ctx_v_col=ctx_v_col)
        y_conv = _short_conv(rest, conv_w8, layer, n_rows)
        tn = 512
        gate_blk = D_MODEL // tn
        merged = _ws_matmul(
            [y_attn, y_conv], [w_o_attn, w_o_conv], pairs=[(0, 0), (1, 1)], epilogue=_ep_merge,
            extras=[(rest, (tm, tn), lambda n, m: (m, 3 * gate_blk + n)),
                    (rest, (tm, tn), lambda n, m: (m, 4 * gate_blk + n))],
            group=grp, m_tiles=m_tiles, tm=tm, tn=tn, n_out=D_MODEL,
            lhs0_tail=y_attn_ctx, out_dtype=BF16, name="merge")
        gate_spec = (mod, (None, MOD_ROWS, tn), lambda n, m: (layer, 0, 2 * chunk + n))
        if x_tail is None:
            res_extras = [(x_main, (tm, tn), lambda n, m: (m, n)), gate_spec]
            res_tail = None
        else:
            res_tail = x_main.shape[0] // tm
            res_extras = [
                (x_main, (tm, tn), lambda n, m: (jnp.minimum(m, res_tail - 1), n)),
                (x_tail, (tm, tn), lambda n, m: (jnp.maximum(m - res_tail, 0), n)),
                gate_spec]
        x_mid = _ws_matmul(
            [merged], [w_out], pairs=[(0, 0)],
            epilogue=functools.partial(_ep_residual, tm=tm, tail_tiles=res_tail),
            extras=res_extras, group=grp, m_tiles=m_tiles, tm=tm, tn=tn, n_out=D_MODEL,
            out_dtype=F32, name="out_proj")

        if layer % 2 == 0:
            i = layer // 2
            h2 = _norm_mod(x_mid, norm2, mod, layer, 3, 4, n_rows, BF16)
            hidden = _ws_matmul(
                [h2], [ffn_w1, ffn_w3], pairs=[(0, 0), (0, 1)], epilogue=_ep_swiglu,
                group=_layer_group(i, m_tiles), m_tiles=m_tiles, tm=tm, tn=512, n_out=D_FF,
                out_dtype=BF16, name="ffn_up")
            tm2, tn2 = 512, 512
            x_main, x_tail = _ws_matmul(
                [hidden], [ffn_w2], pairs=[(0, 0)],
                epilogue=functools.partial(_ep_residual, tm=tm2),
                extras=[(x_mid, (tm2, tn2), lambda n, m: (m, n)),
                        (mod, (None, MOD_ROWS, tn2),
                         lambda n, m: (layer, 0, 5 * (D_MODEL // tn2) + n))],
                group=_layer_group(i, n_rows // tm2), m_tiles=n_rows // tm2, tm=tm2, tn=tn2,
                n_out=D_MODEL, out_dtype=F32, name="ffn_down"), None
        else:
            i = layer // 2
            w_router = jnp.pad(router[i], ((0, 0), (0, LANES - N_EXPERTS)))
            h2_chunks, route = _norm_route(x_mid, norm2, mod, layer, 3, 4, w_router, n_rows)
            pos, tok_of_row, tile_expert, tile_first, n_used = _routing_tables(route)
            rows = _dispatch(h2_chunks, tok_of_row, n_used)
            moe_tiles = MOE_ROWS // MOE_TILE
            n_stack = moe_w1.shape[0] * N_EXPERTS
            w1s = moe_w1.reshape(n_stack, D_MODEL, D_FF_EXPERT)
            w3s = moe_w3.reshape(n_stack, D_MODEL, D_FF_EXPERT)
            w2s = moe_w2.reshape(n_stack, D_FF_EXPERT, D_MODEL)
            tile_group = tile_expert + i * N_EXPERTS
            hidden = _ws_matmul(
                [rows], [w1s, w3s], pairs=[(0, 0), (0, 1)], epilogue=_ep_swiglu,
                group=tile_group, tile_first=tile_first, m_tiles=moe_tiles, tm=MOE_TILE, tn=1024,
                n_out=D_FF_EXPERT, out_dtype=BF16, name="moe_up")
            y_rows = _ws_matmul(
                [hidden], [w2s], pairs=[(0, 0)], epilogue=_ep_plain,
                group=tile_group, tile_first=tile_first, m_tiles=moe_tiles, tm=MOE_TILE, tn=512,
                n_out=D_MODEL, out_dtype=F32, name="moe_down")
            out = _combine(y_rows, pos, x_mid, route, mod, layer, 5, norm_f)

    return out.reshape(BATCH, SEQ, D_MODEL)
```

```python
import functools

import numpy as np
import jax
import jax.numpy as jnp
from jax import lax
from jax.experimental import pallas as pl
from jax.experimental.pallas import tpu as pltpu

D_MODEL = 2048
BATCH = 4
SEQ = 2048
DEPTH = 2
GRID_W = 64
CTX_LEN = 256
N_HEADS = 16
N_KV_HEADS = 4
GROUP = N_HEADS // N_KV_HEADS
HEAD_DIM = 128
WINDOW = 128
BLOCK = 128
ROPE_THETA = 10000.0
D_FF = 5632
N_EXPERTS = 8
TOP_K = 2
D_FF_EXPERT = 7168
RMS_EPS = 1e-6
NEG_INF = -1e30
LOG2_E = 1.4426950408889634
N_MOD = 6
Q_W = N_HEADS * HEAD_DIM
KV_W = N_KV_HEADS * HEAD_DIM
QKV_W = Q_W + 2 * KV_W
REST_W = 3 * D_MODEL + 2 * D_MODEL

N_LAT = BATCH * SEQ
N_CTX = BATCH * CTX_LEN
N_ALL = N_LAT + N_CTX
CTX_MOD_ROW = BATCH
MOD_ROWS = 8

LANES = 128
VMEM_LIMIT = 60 * 1024 * 1024
ROW_TILE = 1024
MOE_TILE = 512
MOE_ROWS = TOP_K * N_LAT + N_EXPERTS * MOE_TILE
PART_ROWS = 128

F32 = jnp.float32
BF16 = jnp.bfloat16


def _params(n_axes):
    return pltpu.CompilerParams(
        dimension_semantics=("arbitrary",) * n_axes, vmem_limit_bytes=VMEM_LIMIT)


def _sigmoid(v):
    return 1.0 / (1.0 + jnp.exp(-v))


def _mod_row(m_tile_start_row):
    return jnp.minimum(m_tile_start_row // SEQ, CTX_MOD_ROW)


def _modulation_kernel(c_ref, w_ref, b_ref, o_ref):
    cv = c_ref[...]
    act = (cv * _sigmoid(cv)).astype(BF16)
    o_ref[...] = jnp.dot(act, w_ref[...].astype(BF16), preferred_element_type=F32) + b_ref[...]


def _modulation(cvec, w_mod, b_mod):
    tn = 1024
    n_tiles = N_MOD * D_MODEL // tn
    return pl.pallas_call(
        _modulation_kernel,
        grid=(DEPTH, n_tiles),
        in_specs=[
            pl.BlockSpec((MOD_ROWS, D_MODEL), lambda l, n: (0, 0)),
            pl.BlockSpec((None, D_MODEL, tn), lambda l, n: (l, 0, n)),
            pl.BlockSpec((None, 1, tn), lambda l, n: (l, 0, n)),
        ],
        out_specs=pl.BlockSpec((None, MOD_ROWS, tn), lambda l, n: (l, 0, n)),
        out_shape=jax.ShapeDtypeStruct((DEPTH, MOD_ROWS, N_MOD * D_MODEL), F32),
        compiler_params=_params(2),
        name="modulation",
    )(cvec, w_mod, b_mod.reshape(DEPTH, 1, N_MOD * D_MODEL))


def _norm_kernel(x_ref, *refs, tm, tail_tiles):
    m = pl.program_id(0)
    row = _mod_row(m * tm)
    if tail_tiles is None:
        g_ref, sh_ref, sc_ref, o_ref = refs
        xf = x_ref[...]
    else:
        x_tail_ref, g_ref, sh_ref, sc_ref, o_ref = refs
        xf = jnp.where(m < tail_tiles, x_ref[...], x_tail_ref[...])
    inv = lax.rsqrt(jnp.mean(xf * xf, axis=-1, keepdims=True) + RMS_EPS)
    y = (xf * inv) * g_ref[...]
    y = y * (1.0 + sc_ref[pl.ds(row, 1), :]) + sh_ref[pl.ds(row, 1), :]
    o_ref[...] = y.astype(o_ref.dtype)


def _norm_mod(x_rows, gain, mod, layer, shift_chunk, scale_chunk, n_rows, out_dtype, x_tail=None):
    tm = 512
    xs = [x_rows]
    x_specs = [pl.BlockSpec((tm, D_MODEL), lambda m: (m, 0))]
    tail_tiles = None
    if x_tail is not None:
        tail_tiles = x_rows.shape[0] // tm
        xs.append(x_tail)
        x_specs = [pl.BlockSpec((tm, D_MODEL), lambda m: (jnp.minimum(m, tail_tiles - 1), 0)),
                   pl.BlockSpec((tm, D_MODEL), lambda m: (jnp.maximum(m - tail_tiles, 0), 0))]
    return pl.pallas_call(
        functools.partial(_norm_kernel, tm=tm, tail_tiles=tail_tiles),
        grid=(n_rows // tm,),
        in_specs=x_specs + [
            pl.BlockSpec((None, 1, D_MODEL), lambda m: (layer, 0, 0)),
            pl.BlockSpec((None, MOD_ROWS, D_MODEL), lambda m: (layer, 0, shift_chunk)),
            pl.BlockSpec((None, MOD_ROWS, D_MODEL), lambda m: (layer, 0, scale_chunk)),
        ],
        out_specs=pl.BlockSpec((tm, D_MODEL), lambda m: (m, 0)),
        out_shape=jax.ShapeDtypeStruct((n_rows, D_MODEL), out_dtype),
        compiler_params=_params(1),
        name="norm_mod",
    )(*xs, gain.reshape(DEPTH, 1, D_MODEL), mod, mod)


def _ws_kernel(group_ref, first_ref, runs_ref, *refs, n_lhs, n_w, n_extra, pairs, epilogue,
               tail_tiles, tm, tn, n_sub, partial, manual):
    lhs_refs = refs[:n_lhs]
    w_refs = refs[n_lhs:n_lhs + n_w]
    ex_refs = refs[n_lhs + n_w:n_lhs + n_w + n_extra]
    out_ref = refs[n_lhs + n_w + n_extra]
    n = pl.program_id(0)
    m = pl.program_id(1)

    if manual:
        wbuf, sem, slot_ref = refs[n_lhs + n_w + n_extra + 1:]

        def weight_copies(col, grp, slot):
            cols = pl.ds(pl.multiple_of(col * tn, tn), tn)
            return [pltpu.make_async_copy(w_ref.at[grp, :, cols], wbuf.at[slot, wi], sem.at[slot, wi])
                    for wi, w_ref in enumerate(w_refs)]

        @pl.when((n == 0) & (m == 0))
        def _():
            slot_ref[0] = 0
            for copy in weight_copies(0, group_ref[0], 0):
                copy.start()

        cur = slot_ref[0]

        @pl.when(runs_ref[0, m] == 1)
        def _():
            for copy in weight_copies(n, group_ref[m], cur):
                copy.wait()
            next_col = n + runs_ref[3, m]

            @pl.when(next_col < pl.num_programs(0))
            def _():
                for copy in weight_copies(next_col, runs_ref[2, m], 1 - cur):
                    copy.start()

        def weight(wi):
            return wbuf[cur, wi]
    else:
        def weight(wi):
            return w_refs[wi][...]

    def lhs_tile(li, rs):
        if li == 0 and tail_tiles is not None:
            return jnp.where(m < tail_tiles, lhs_refs[0][rs, :], lhs_refs[n_lhs - 1][rs, :])
        return lhs_refs[li][rs, :]

    def compute(row_slices):
        w_bf = [weight(wi).astype(BF16) for wi in range(n_w)]
        for rs in row_slices:
            dots = [jnp.dot(lhs_tile(li, rs), w_bf[wi], preferred_element_type=F32)
                    for li, wi in pairs]
            out_ref[rs, :] = epilogue(dots, ex_refs, n, m, rs).astype(out_ref.dtype)

    def slices(size):
        return [slice(r, r + size) for r in range(0, tm, size)]

    if not partial:
        compute(slices(tm // n_sub))
    else:
        first = first_ref[m]

        @pl.when(first == 0)
        def _():
            compute(slices(tm // n_sub))

        for rs in slices(PART_ROWS):
            @pl.when((first > 0) & (rs.stop > first))
            def _():
                compute([rs])

            @pl.when((first > 0) & (rs.stop <= first))
            def _():
                out_ref[rs, :] = jnp.zeros((PART_ROWS, out_ref.shape[1]), out_ref.dtype)

    if manual:
        @pl.when(runs_ref[1, m] == 1)
        def _():
            slot_ref[0] = 1 - cur


def _ws_matmul(lhs, weights, *, pairs, epilogue, extras=(), group, tile_first=None, runs=None,
               m_tiles, tm, tn, n_out, col_off=0, lhs_row_off=0, lhs0_tail=None, n_sub=1,
               out_dtype, name):
    k_dim = lhs[0].shape[1]
    col_blk = col_off
    row_blk = lhs_row_off
    partial = tile_first is not None
    if tile_first is None:
        tile_first = jnp.zeros((m_tiles,), jnp.int32)
    manual = runs is not None
    if runs is None:
        runs = jnp.zeros((4, m_tiles), jnp.int32)

    def wrap(fn):
        return lambda n, m, g, u, r: fn(n, m)

    in_specs = [pl.BlockSpec((tm, k_dim), lambda n, m, g, u, r: (m + row_blk, 0)) for _ in lhs]
    tail_tiles = None
    if lhs0_tail is not None:
        tail_tiles = lhs[0].shape[0] // tm
        in_specs[0] = pl.BlockSpec(
            (tm, k_dim), lambda n, m, g, u, r: (jnp.minimum(m, tail_tiles - 1), 0))
        in_specs.append(pl.BlockSpec(
            (tm, k_dim), lambda n, m, g, u, r: (jnp.maximum(m - tail_tiles, 0), 0)))
        lhs = list(lhs) + [lhs0_tail]
    if manual:
        assert col_off == 0
        in_specs += [pl.BlockSpec(memory_space=pl.ANY) for _ in weights]
        scratch = [pltpu.VMEM((2, len(weights), k_dim, tn), F32),
                   pltpu.SemaphoreType.DMA((2, len(weights))),
                   pltpu.SMEM((1,), jnp.int32)]
    else:
        in_specs += [pl.BlockSpec((None, k_dim, tn),
                                  lambda n, m, g, u, r: (g[m], 0, n + col_blk)) for _ in weights]
        scratch = []
    in_specs += [pl.BlockSpec(shape, wrap(fn)) for _, shape, fn in extras]
    grid_spec = pltpu.PrefetchScalarGridSpec(
        num_scalar_prefetch=3,
        grid=(n_out // tn, m_tiles),
        in_specs=in_specs,
        out_specs=pl.BlockSpec((tm, tn), lambda n, m, g, u, r: (m, n)),
        scratch_shapes=scratch,
    )
    kern = functools.partial(
        _ws_kernel, n_lhs=len(lhs), n_w=len(weights), n_extra=len(extras), pairs=pairs,
        epilogue=epilogue, tail_tiles=tail_tiles, tm=tm, tn=tn, n_sub=n_sub, partial=partial,
        manual=manual)
    return pl.pallas_call(
        kern,
        grid_spec=grid_spec,
        out_shape=jax.ShapeDtypeStruct((m_tiles * tm, n_out), out_dtype),
        compiler_params=_params(2),
        name=name,
    )(group, tile_first, runs, *lhs, *weights, *[e[0] for e in extras])


def _run_table(group):
    m_tiles = group.shape[0]
    idx = jnp.arange(m_tiles, dtype=jnp.int32)
    start = group != jnp.concatenate([group[:1] - 1, group[:-1]])
    end = group != jnp.concatenate([group[1:], group[-1:] - 1])
    start_pos = jnp.where(start, idx, m_tiles)
    after = jnp.concatenate([start_pos[1:], jnp.full((1,), m_tiles, jnp.int32)])
    next_start = lax.cummin(after[::-1])[::-1]
    wraps = next_start >= m_tiles
    next_group = group[jnp.where(wraps, 0, next_start)]
    return jnp.stack([start, end, next_group, wraps]).astype(jnp.int32)


def _layer_group(layer, m_tiles):
    return jnp.full((m_tiles,), layer, jnp.int32)


def _ep_plain(dots, ex, n, m, rs):
    return dots[0]


def _ep_rope(dots, ex, n, m, rs):
    cos_ref, sa_ref, sb_ref = ex
    d = dots[0]
    cos, sa, sb = cos_ref[rs, :], sa_ref[rs, :], sb_ref[rs, :]
    quarter = HEAD_DIM // 4
    heads = []
    for h in range(d.shape[1] // HEAD_DIM):
        t = d[:, h * HEAD_DIM:(h + 1) * HEAD_DIM]
        heads.append(t * cos + pltpu.roll(t, HEAD_DIM - quarter, 1) * sa
                     + pltpu.roll(t, quarter, 1) * sb)
    return jnp.concatenate(heads, axis=1)


def _ep_merge(dots, ex, n, m, rs):
    ga_ref, gc_ref = ex
    return (_sigmoid(ga_ref[rs, :].astype(F32)) * dots[0]
            + _sigmoid(gc_ref[rs, :].astype(F32)) * dots[1])


def _ep_residual(dots, ex, n, m, rs, *, tm, tail_tiles=None):
    if tail_tiles is None:
        x_ref, gate_ref = ex
        x = x_ref[rs, :]
    else:
        x_ref, x_tail_ref, gate_ref = ex
        x = jnp.where(m < tail_tiles, x_ref[rs, :], x_tail_ref[rs, :])
    row = _mod_row(m * tm)
    return x + gate_ref[pl.ds(row, 1), :] * dots[0]


def _ep_swiglu(dots, ex, n, m, rs):
    a = dots[0]
    return (a * _sigmoid(a)) * dots[1]


def _rope_tables(tm):
    quarter = HEAD_DIM // 4
    inv_freq = np.power(np.float32(ROPE_THETA),
                        -np.arange(quarter, dtype=np.float32) / np.float32(quarter)).astype(np.float32)
    pos = np.arange(SEQ)
    rows = (pos // GRID_W).astype(np.float32)
    cols = (pos % GRID_W).astype(np.float32)
    ang = np.concatenate([rows[:, None] * inv_freq[None, :]] * 2
                         + [cols[:, None] * inv_freq[None, :]] * 2, axis=1).astype(np.float32)
    cos = np.cos(ang).astype(np.float32)
    sin = np.sin(ang).astype(np.float32)
    lane = np.arange(HEAD_DIM)
    first = ((lane // quarter) % 2 == 0)[None, :]
    sa = np.where(first, -sin, 0.0).astype(np.float32)
    sb = np.where(first, 0.0, sin).astype(np.float32)
    ones = np.ones((tm, HEAD_DIM), np.float32)
    zeros = np.zeros((tm, HEAD_DIM), np.float32)
    scale = np.float32(HEAD_DIM ** -0.5 * LOG2_E)

    def variants(tbl, ident):
        k = np.concatenate([tbl, ident], axis=0)
        ident_all = np.concatenate([np.broadcast_to(ident[:1], tbl.shape), ident], axis=0)
        return np.stack([k * scale, k, ident_all])

    return (jnp.asarray(variants(cos, ones)), jnp.asarray(variants(sa, zeros)),
            jnp.asarray(variants(sb, zeros)))


ATTN_ROWS = 2 * BLOCK
BAND = 3 * BLOCK


def _band_masks():
    r = np.arange(BLOCK)[:, None]
    c = np.arange(BAND)[None, :]
    masks = [(np.abs(off * BLOCK + r - c) <= WINDOW).astype(np.float32) for off in range(3)]
    return jnp.asarray(np.stack([np.tile(m, (GROUP, 1)) for m in masks]))


def _stack_heads(q):
    return jnp.concatenate([q[:, h * HEAD_DIM:(h + 1) * HEAD_DIM] for h in range(GROUP)], axis=0)


def _unstack_heads(o):
    return jnp.concatenate([o[h * BLOCK:(h + 1) * BLOCK, :] for h in range(GROUP)], axis=1)


def _sink_column(sink_ref, g):
    return LOG2_E * jnp.concatenate(
        [jnp.broadcast_to(sink_ref[pl.ds(g * GROUP + h, 1), 0:1], (BLOCK, 1)) for h in range(GROUP)],
        axis=0)


def _softmax_pv(s, sink, vals):
    mx = jnp.maximum(jnp.max(s, axis=-1, keepdims=True), sink)
    p = jnp.exp2(s - mx)
    denom = jnp.sum(p, axis=-1, keepdims=True) + jnp.exp2(sink - mx)
    return jnp.dot(p.astype(BF16), vals, preferred_element_type=F32) / denom


_CONTRACT_LAST = (((1,), (1,)), ((), ()))


def _attn_latent_kernel(sink_ref, mask_ref, q_ref, k_ref, v_ref, kx_ref, vx_ref, o_ref, s_even, s_odd):
    n_blocks = SEQ // BLOCK
    sink = _sink_column(sink_ref, pl.program_id(1))

    def band_start(i):
        first = jnp.clip(i - 1, 0, n_blocks - 3)
        return first, pl.multiple_of(first * BLOCK, BLOCK)

    def scores(i, s_ref):
        first, start = band_start(i)
        qs = _stack_heads(q_ref[pl.ds(pl.multiple_of(i * BLOCK, BLOCK), BLOCK), :])
        s_loc = lax.dot_general(qs, k_ref[pl.ds(start, BAND), :], _CONTRACT_LAST,
                                preferred_element_type=F32)
        s_ref[:, 0:BAND] = jnp.where(mask_ref[i - first] > 0.5, s_loc, NEG_INF)
        s_ref[:, BAND:] = lax.dot_general(qs, kx_ref[...], _CONTRACT_LAST,
                                          preferred_element_type=F32)

    def finish(i, s_ref):
        _, start = band_start(i)
        vals = jnp.concatenate([v_ref[pl.ds(start, BAND), :], vx_ref[...]], axis=0)
        o = _softmax_pv(s_ref[...], sink, vals)
        o_ref[pl.ds(pl.multiple_of(i * BLOCK, BLOCK), BLOCK), :] = _unstack_heads(o).astype(o_ref.dtype)

    scores(0, s_even)

    def pair(t, carry):
        i = 2 * t
        scores(i + 1, s_odd)
        finish(i, s_even)
        scores(jnp.minimum(i + 2, n_blocks - 1), s_even)
        finish(i + 1, s_odd)
        return carry

    lax.fori_loop(0, n_blocks // 2, pair, 0)


def _attn_ctx_kernel(sink_ref, q_ref, kx_ref, vx_ref, o_ref):
    sink = _sink_column(sink_ref, pl.program_id(1))
    for sub in range(CTX_LEN // BLOCK):
        rows = slice(sub * BLOCK, (sub + 1) * BLOCK)
        qs = _stack_heads(q_ref[rows, :])
        s = lax.dot_general(qs, kx_ref[...], _CONTRACT_LAST, preferred_element_type=F32)
        o_ref[rows, :] = _unstack_heads(_softmax_pv(s, sink, vx_ref[...])).astype(o_ref.dtype)


def _attention(qkv, kv_ctx, sink_tbl, masks, *, with_ctx_queries, ctx_row_blk, ctx_k_col, ctx_v_col):
    k_col = Q_W // HEAD_DIM
    v_col = (Q_W + KV_W) // HEAD_DIM
    width = GROUP * HEAD_DIM
    sink_spec = pl.BlockSpec((N_HEADS, LANES), lambda b, g: (0, 0))
    kx_spec = pl.BlockSpec((CTX_LEN, HEAD_DIM), lambda b, g: (ctx_row_blk + b, ctx_k_col + g))
    vx_spec = pl.BlockSpec((CTX_LEN, HEAD_DIM), lambda b, g: (ctx_row_blk + b, ctx_v_col + g))
    y = pl.pallas_call(
        _attn_latent_kernel,
        grid=(BATCH, N_KV_HEADS),
        in_specs=[
            sink_spec,
            pl.BlockSpec((3, GROUP * BLOCK, BAND), lambda b, g: (0, 0, 0)),
            pl.BlockSpec((SEQ, width), lambda b, g: (b, g)),
            pl.BlockSpec((SEQ, HEAD_DIM), lambda b, g: (b, k_col + g)),
            pl.BlockSpec((SEQ, HEAD_DIM), lambda b, g: (b, v_col + g)),
            kx_spec, vx_spec,
        ],
        out_specs=pl.BlockSpec((SEQ, width), lambda b, g: (b, g)),
        out_shape=jax.ShapeDtypeStruct((N_LAT, Q_W), BF16),
        scratch_shapes=[pltpu.VMEM((GROUP * BLOCK, BAND + CTX_LEN), F32),
                        pltpu.VMEM((GROUP * BLOCK, BAND + CTX_LEN), F32)],
        compiler_params=_params(2),
        name="attention",
    )(sink_tbl, masks, qkv, qkv, qkv, kv_ctx, kv_ctx)
    if not with_ctx_queries:
        return y, None
    ctx_blk = N_LAT // CTX_LEN
    y_ctx = pl.pallas_call(
        _attn_ctx_kernel,
        grid=(BATCH, N_KV_HEADS),
        in_specs=[
            sink_spec,
            pl.BlockSpec((CTX_LEN, width), lambda b, g: (ctx_blk + b, g)),
            kx_spec, vx_spec,
        ],
        out_specs=pl.BlockSpec((CTX_LEN, width), lambda b, g: (b, g)),
        out_shape=jax.ShapeDtypeStruct((N_CTX, Q_W), BF16),
        compiler_params=_params(2),
        name="attention_ctx",
    )(sink_tbl, qkv, kv_ctx, kv_ctx)
    return y, y_ctx


CONV_ROWS = 256
HALO = 16


def _conv_kernel(w_ref, u_ref, gb_ref, gc_ref, up_ref, gcp_ref, un_ref, gcn_ref, o_ref):
    r = pl.program_id(0)
    lat_blocks = N_LAT // CONV_ROWS
    per_seq = SEQ // CONV_ROWS
    is_ctx = r >= lat_blocks
    seq_start = is_ctx | (r % per_seq == 0)
    seq_end = is_ctx | (r % per_seq == per_seq - 1)
    z = gc_ref[...].astype(F32) * u_ref[...].astype(F32)
    z_before = (gcp_ref[HALO - 1:HALO, :].astype(F32) * up_ref[HALO - 1:HALO, :].astype(F32))
    z_after = gcn_ref[0:1, :].astype(F32) * un_ref[0:1, :].astype(F32)
    z_before = jnp.where(seq_start, 0.0, z_before)
    z_after = jnp.where(seq_end, 0.0, z_after)
    row = lax.broadcasted_iota(jnp.int32, z.shape, 0)
    z_prev = jnp.where(row == 0, z_before, pltpu.roll(z, 1, 0))
    z_next = jnp.where(row == CONV_ROWS - 1, z_after, pltpu.roll(z, CONV_ROWS - 1, 0))
    y = z_prev * w_ref[0:1, :] + z * w_ref[1:2, :] + z_next * w_ref[2:3, :]
    o_ref[...] = (gb_ref[...].astype(F32) * y).astype(o_ref.dtype)


def _short_conv(rest, conv_w, layer, n_rows):
    n_blocks = n_rows // CONV_ROWS
    tc = D_MODEL
    halo_per_block = CONV_ROWS // HALO
    last_halo = n_rows // HALO - 1

    def main(chunk):
        return pl.BlockSpec((CONV_ROWS, tc), lambda r: (r, chunk))

    def before(chunk):
        return pl.BlockSpec((HALO, tc), lambda r: (jnp.maximum(r * halo_per_block - 1, 0), chunk))

    def after(chunk):
        return pl.BlockSpec((HALO, tc),
                            lambda r: (jnp.minimum((r + 1) * halo_per_block, last_halo), chunk))

    return pl.pallas_call(
        _conv_kernel,
        grid=(n_blocks,),
        in_specs=[
            pl.BlockSpec((None, 8, tc), lambda r: (layer, 0, 0)),
            main(0), main(1), main(2), before(0), before(2), after(0), after(2),
        ],
        out_specs=pl.BlockSpec((CONV_ROWS, tc), lambda r: (r, 0)),
        out_shape=jax.ShapeDtypeStruct((n_rows, D_MODEL), BF16),
        compiler_params=_params(1),
        name="short_conv",
    )(conv_w, rest, rest, rest, rest, rest, rest, rest)


ROW_CHUNKS = D_MODEL // LANES


def _norm_route_kernel(x_ref, g_ref, sh_ref, sc_ref, w_ref, chunks_ref, route_ref, *, tm):
    row = _mod_row(pl.program_id(0) * tm)
    xf = x_ref[...]
    inv = lax.rsqrt(jnp.mean(xf * xf, axis=-1, keepdims=True) + RMS_EPS)
    h = (xf * inv) * g_ref[...]
    h = h * (1.0 + sc_ref[pl.ds(row, 1), :]) + sh_ref[pl.ds(row, 1), :]
    for j in range(ROW_CHUNKS):
        chunks_ref[pl.ds(j, tm, stride=ROW_CHUNKS), :] = h[:, j * LANES:(j + 1) * LANES]
    route_ref[...] = _route(h, w_ref[...])


def _norm_route(x_rows, gain, mod, layer, shift_chunk, scale_chunk, w_router_padded, n_rows):
    tm = 512
    return pl.pallas_call(
        functools.partial(_norm_route_kernel, tm=tm),
        grid=(n_rows // tm,),
        in_specs=[
            pl.BlockSpec((tm, D_MODEL), lambda m: (m, 0)),
            pl.BlockSpec((None, 1, D_MODEL), lambda m: (layer, 0, 0)),
            pl.BlockSpec((None, MOD_ROWS, D_MODEL), lambda m: (layer, 0, shift_chunk)),
            pl.BlockSpec((None, MOD_ROWS, D_MODEL), lambda m: (layer, 0, scale_chunk)),
            pl.BlockSpec((D_MODEL, LANES), lambda m: (0, 0)),
        ],
        out_specs=[pl.BlockSpec((tm * ROW_CHUNKS, LANES), lambda m: (m, 0)),
                   pl.BlockSpec((tm, LANES), lambda m: (m, 0))],
        out_shape=[jax.ShapeDtypeStruct((n_rows * ROW_CHUNKS, LANES), F32),
                   jax.ShapeDtypeStruct((n_rows, LANES), F32)],
        compiler_params=_params(1),
        name="norm_route",
    )(x_rows, gain.reshape(DEPTH, 1, D_MODEL), mod, mod, w_router_padded)


def _route(h, w):
    h_hi = h.astype(BF16)
    h_lo = (h - h_hi.astype(F32)).astype(BF16)
    w_hi = w.astype(BF16)
    w_lo = (w - w_hi.astype(F32)).astype(BF16)
    logits = (jnp.dot(h_hi, w_hi, preferred_element_type=F32)
              + jnp.dot(h_hi, w_lo, preferred_element_type=F32)
              + jnp.dot(h_lo, w_hi, preferred_element_type=F32))
    lane = lax.broadcasted_iota(jnp.int32, logits.shape, 1)
    lowest = float(jnp.finfo(F32).min)
    logits = jnp.where(lane < N_EXPERTS, logits, lowest)
    v1 = jnp.max(logits, axis=-1, keepdims=True)
    i1 = jnp.min(jnp.where(logits == v1, lane, LANES), axis=-1, keepdims=True)
    rest = jnp.where(lane == i1, lowest, logits)
    v2 = jnp.max(rest, axis=-1, keepdims=True)
    i2 = jnp.min(jnp.where(rest == v2, lane, LANES), axis=-1, keepdims=True)
    e2 = jnp.exp(v2 - v1)
    p1 = 1.0 / (1.0 + e2)
    p2 = e2 / (1.0 + e2)
    return jnp.where(lane == 0, i1.astype(F32),
                     jnp.where(lane == 1, i2.astype(F32),
                               jnp.where(lane == 2, p1, jnp.where(lane == 3, p2, 0.0))))


DISPATCH_TILE = 256


DMA_UNROLL = 8


def _dispatch_kernel(nused_ref, tok_ref, tok_next_ref, h_hbm, o_ref, buf, sem):
    t = pl.program_id(0)
    slot = t % 2
    n_active = nused_ref[0] * (MOE_TILE // DISPATCH_TILE)
    tile_chunks = DISPATCH_TILE * ROW_CHUNKS

    def gather(idx_ref, dst_slot):
        def body(i, carry):
            for u in range(DMA_UNROLL):
                j = i * DMA_UNROLL + u
                src = pl.multiple_of(idx_ref[0, j] * ROW_CHUNKS, ROW_CHUNKS)
                pltpu.make_async_copy(
                    h_hbm.at[pl.ds(src, ROW_CHUNKS), :],
                    buf.at[dst_slot, pl.ds(j * ROW_CHUNKS, ROW_CHUNKS), :],
                    sem.at[dst_slot]).start(priority=u % 2)
            return carry
        lax.fori_loop(0, DISPATCH_TILE // DMA_UNROLL, body, 0)

    @pl.when((t == 0) & (n_active > 0))
    def _():
        gather(tok_ref, 0)

    @pl.when(t + 1 < n_active)
    def _():
        gather(tok_next_ref, 1 - slot)

    @pl.when(t < n_active)
    def _():
        pltpu.make_async_copy(h_hbm.at[pl.ds(0, tile_chunks), :], buf.at[slot], sem.at[slot]).wait()
        cols = [buf[slot, pl.ds(j, DISPATCH_TILE, stride=ROW_CHUNKS), :] for j in range(ROW_CHUNKS)]
        o_ref[...] = jnp.concatenate(cols, axis=1).astype(o_ref.dtype)

    @pl.when(t >= n_active)
    def _():
        o_ref[...] = jnp.zeros_like(o_ref)


def _dispatch(h_chunks, tok_of_row, n_used):
    n_rows = tok_of_row.shape[0]
    n_steps = n_rows // DISPATCH_TILE
    tok_blocks = tok_of_row.reshape(n_steps, 1, DISPATCH_TILE)
    grid_spec = pltpu.PrefetchScalarGridSpec(
        num_scalar_prefetch=1,
        grid=(n_steps,),
        in_specs=[
            pl.BlockSpec((None, 1, DISPATCH_TILE), lambda t, u: (t, 0, 0), memory_space=pltpu.SMEM),
            pl.BlockSpec((None, 1, DISPATCH_TILE),
                         lambda t, u: (jnp.minimum(t + 1, n_steps - 1), 0, 0),
                         memory_space=pltpu.SMEM),
            pl.BlockSpec(memory_space=pl.ANY),
        ],
        out_specs=pl.BlockSpec((DISPATCH_TILE, D_MODEL), lambda t, u: (t, 0)),
        scratch_shapes=[pltpu.VMEM((2, DISPATCH_TILE * ROW_CHUNKS, LANES), F32),
                        pltpu.SemaphoreType.DMA((2,))],
    )
    return pl.pallas_call(
        _dispatch_kernel,
        grid_spec=grid_spec,
        out_shape=jax.ShapeDtypeStruct((n_rows, D_MODEL), BF16),
        compiler_params=_params(1),
        name="moe_dispatch",
    )(n_used, tok_blocks, tok_blocks, h_chunks)


COMBINE_TILE = 256


def _combine_kernel(pos_ref, pos_next_ref, y_hbm, x_ref, route_ref, gate_ref, gain_ref, o_ref,
                    buf, sem):
    t = pl.program_id(0)
    n_steps = pl.num_programs(0)
    slot = t % 2

    def gather(idx_ref, dst_slot):
        def body(i, carry):
            for u in range(DMA_UNROLL):
                j = i * DMA_UNROLL + u
                for k in range(TOP_K):
                    pltpu.make_async_copy(
                        y_hbm.at[pl.ds(idx_ref[0, TOP_K * j + k], 1), :],
                        buf.at[dst_slot, k, pl.ds(j, 1), :],
                        sem.at[dst_slot]).start(priority=k)
            return carry
        lax.fori_loop(0, COMBINE_TILE // DMA_UNROLL, body, 0)

    @pl.when(t == 0)
    def _():
        gather(pos_ref, 0)

    @pl.when(t + 1 < n_steps)
    def _():
        gather(pos_next_ref, 1 - slot)

    for k in range(TOP_K):
        pltpu.make_async_copy(
            y_hbm.at[pl.ds(0, COMBINE_TILE), :], buf.at[slot, k], sem.at[slot]).wait()
    route = route_ref[...]
    moe = route[:, 2:3] * buf[slot, 0] + route[:, 3:4] * buf[slot, 1]
    row = _mod_row(t * COMBINE_TILE)
    xn = x_ref[...] + gate_ref[pl.ds(row, 1), :] * moe
    inv = lax.rsqrt(jnp.mean(xn * xn, axis=-1, keepdims=True) + RMS_EPS)
    o_ref[...] = (xn * inv) * gain_ref[...]


def _combine(y_rows, pos, x_rows, route, mod, layer, gate_chunk, norm_f):
    n_steps = N_LAT // COMBINE_TILE
    pos_blocks = pos.reshape(n_steps, 1, TOP_K * COMBINE_TILE)
    idx_shape = (None, 1, TOP_K * COMBINE_TILE)
    return pl.pallas_call(
        _combine_kernel,
        grid=(n_steps,),
        in_specs=[
            pl.BlockSpec(idx_shape, lambda t: (t, 0, 0), memory_space=pltpu.SMEM),
            pl.BlockSpec(idx_shape, lambda t: (jnp.minimum(t + 1, n_steps - 1), 0, 0),
                         memory_space=pltpu.SMEM),
            pl.BlockSpec(memory_space=pl.ANY),
            pl.BlockSpec((COMBINE_TILE, D_MODEL), lambda t: (t, 0)),
            pl.BlockSpec((COMBINE_TILE, LANES), lambda t: (t, 0)),
            pl.BlockSpec((None, MOD_ROWS, D_MODEL), lambda t: (layer, 0, gate_chunk)),
            pl.BlockSpec((1, D_MODEL), lambda t: (0, 0)),
        ],
        out_specs=pl.BlockSpec((COMBINE_TILE, D_MODEL), lambda t: (t, 0)),
        out_shape=jax.ShapeDtypeStruct((N_LAT, D_MODEL), F32),
        scratch_shapes=[pltpu.VMEM((2, TOP_K, COMBINE_TILE, D_MODEL), F32),
                        pltpu.SemaphoreType.DMA((2,))],
        compiler_params=_params(1),
        name="moe_combine",
    )(pos_blocks, pos_blocks, y_rows, x_rows, route, mod, norm_f.reshape(1, D_MODEL))


def _routing_tables(route):
    experts = route[:, :TOP_K].astype(jnp.int32).reshape(-1)
    onehot = (experts[:, None] == jnp.arange(N_EXPERTS, dtype=jnp.int32)[None, :]).astype(jnp.int32)
    rank = jnp.sum((jnp.cumsum(onehot, axis=0) - onehot) * onehot, axis=1)
    counts = jnp.sum(onehot, axis=0)
    padded = ((counts + MOE_TILE - 1) // MOE_TILE) * MOE_TILE
    ends = jnp.cumsum(padded)
    starts = ends - counts
    pos = (jnp.sum(onehot * starts[None, :], axis=1) + rank).astype(jnp.int32)
    tok_of_row = jnp.zeros((MOE_ROWS,), jnp.int32).at[pos].set(
        jnp.arange(TOP_K * N_LAT, dtype=jnp.int32) // TOP_K)
    n_tiles = MOE_ROWS // MOE_TILE
    n_used = (ends[-1] // MOE_TILE).astype(jnp.int32)
    tile_start = jnp.arange(n_tiles, dtype=jnp.int32) * MOE_TILE
    tile_expert = jnp.sum((tile_start[:, None] >= ends[None, :]).astype(jnp.int32), axis=1)
    last_expert = jnp.sum((tile_start[jnp.maximum(n_used - 1, 0)] >= ends).astype(jnp.int32))
    in_use = jnp.arange(n_tiles) < n_used
    tile_expert = jnp.where(in_use, tile_expert, last_expert)
    tile_expert = jnp.minimum(tile_expert, N_EXPERTS - 1).astype(jnp.int32)
    tile_first = jnp.clip(starts[tile_expert] - tile_start, 0, MOE_TILE)
    tile_first = jnp.where(in_use, tile_first, MOE_TILE).astype(jnp.int32)
    return pos, tok_of_row, tile_expert, tile_first, n_used.reshape(1)


def kernel(x, c, ctx, c_ctx, w_mod, b_mod, norm1, w_in, sink, conv_w, w_o_attn, w_o_conv, w_out,
           norm2, ffn_w1, ffn_w3, ffn_w2, router, moe_w1, moe_w3, moe_w2, norm_f):
    tm = ROW_TILE
    cvec = jnp.concatenate(
        [c, c_ctx[None, :], jnp.zeros((MOD_ROWS - BATCH - 1, D_MODEL), F32)], axis=0)
    mod = _modulation(cvec, w_mod, b_mod)
    x_main, x_tail = x.reshape(N_LAT, D_MODEL), ctx.reshape(N_CTX, D_MODEL)
    cos_t, sa_t, sb_t = _rope_tables(tm)
    conv_w8 = jnp.pad(conv_w, ((0, 0), (0, 8 - conv_w.shape[1]), (0, 0)))
    sink_tbl = jnp.broadcast_to(sink[:, :, None], (DEPTH, N_HEADS, LANES))
    masks = _band_masks()
    seq_tiles = SEQ // tm
    lat_tiles = N_LAT // tm
    out = None

    for layer in range(DEPTH):
        ctx_out = layer < DEPTH - 1
        n_rows = N_ALL if ctx_out else N_LAT
        m_tiles = n_rows // tm
        grp = _layer_group(layer, m_tiles)
        chunk = D_MODEL // 512

        h = _norm_mod(x_main, norm1, mod, layer, 0, 1, N_ALL, BF16, x_tail=x_tail)
        tn_qkv = 512

        def variant(n):
            return jnp.where(n < Q_W // tn_qkv, 0, jnp.where(n < (Q_W + KV_W) // tn_qkv, 1, 2))

        def tbl_map(n, m):
            return (variant(n), jnp.where(m < lat_tiles, m % seq_tiles, seq_tiles), 0)

        tbl_shape = (None, tm, HEAD_DIM)
        qkv = _ws_matmul(
            [h], [w_in], pairs=[(0, 0)], epilogue=_ep_rope,
            extras=[(cos_t, tbl_shape, tbl_map), (sa_t, tbl_shape, tbl_map),
                    (sb_t, tbl_shape, tbl_map)],
            group=grp, m_tiles=m_tiles, tm=tm, tn=tn_qkv, n_out=QKV_W, n_sub=2,
            out_dtype=BF16, name="in_proj_qkv")
        rest = _ws_matmul(
            [h], [w_in], pairs=[(0, 0)], epilogue=_ep_plain,
            group=grp, m_tiles=m_tiles, tm=tm, tn=1024, n_out=REST_W,
            col_off=QKV_W // 1024, out_dtype=BF16, name="in_proj_rest")
        if ctx_out:
            kv_ctx, ctx_row_blk = qkv, N_LAT // CTX_LEN
            ctx_k_col, ctx_v_col = Q_W // HEAD_DIM, (Q_W + KV_W) // HEAD_DIM
        else:
            kv_ctx = _ws_matmul(
                [h], [w_in], pairs=[(0, 0)], epilogue=_ep_plain,
                group=_layer_group(layer, N_CTX // tm), m_tiles=N_CTX // tm, tm=tm, tn=512,
                n_out=2 * KV_W, col_off=Q_W // 512, lhs_row_off=lat_tiles, out_dtype=BF16,
                name="in_proj_ctx_kv")
            ctx_row_blk, ctx_k_col, ctx_v_col = 0, 0, KV_W // HEAD_DIM
        y_attn, y_attn_ctx = _attention(
            qkv, kv_ctx, sink_tbl[layer], masks, with_ctx_queries=ctx_out,
            ctx_row_blk=ctx_row_blk, ctx_k_col=ctx_k_col, ctx_v_col=ctx_v_col)
        y_conv = _short_conv(rest, conv_w8, layer, n_rows)
        tn = 512
        gate_blk = D_MODEL // tn
        merged = _ws_matmul(
            [y_attn, y_conv], [w_o_attn, w_o_conv], pairs=[(0, 0), (1, 1)], epilogue=_ep_merge,
            extras=[(rest, (tm, tn), lambda n, m: (m, 3 * gate_blk + n)),
                    (rest, (tm, tn), lambda n, m: (m, 4 * gate_blk + n))],
            group=grp, m_tiles=m_tiles, tm=tm, tn=tn, n_out=D_MODEL,
            lhs0_tail=y_attn_ctx, out_dtype=BF16, name="merge")
        gate_spec = (mod, (None, MOD_ROWS, tn), lambda n, m: (layer, 0, 2 * chunk + n))
        if x_tail is None:
            res_extras = [(x_main, (tm, tn), lambda n, m: (m, n)), gate_spec]
            res_tail = None
        else:
            res_tail = x_main.shape[0] // tm
            res_extras = [
                (x_main, (tm, tn), lambda n, m: (jnp.minimum(m, res_tail - 1), n)),
                (x_tail, (tm, tn), lambda n, m: (jnp.maximum(m - res_tail, 0), n)),
                gate_spec]
        x_mid = _ws_matmul(
            [merged], [w_out], pairs=[(0, 0)],
            epilogue=functools.partial(_ep_residual, tm=tm, tail_tiles=res_tail),
            extras=res_extras, group=grp, m_tiles=m_tiles, tm=tm, tn=tn, n_out=D_MODEL,
            out_dtype=F32, name="out_proj")

        if layer % 2 == 0:
            i = layer // 2
            h2 = _norm_mod(x_mid, norm2, mod, layer, 3, 4, n_rows, BF16)
            hidden = _ws_matmul(
                [h2], [ffn_w1, ffn_w3], pairs=[(0, 0), (0, 1)], epilogue=_ep_swiglu,
                group=_layer_group(i, m_tiles), m_tiles=m_tiles, tm=tm, tn=512, n_out=D_FF,
                out_dtype=BF16, name="ffn_up")
            tm2, tn2 = 512, 512
            x_main, x_tail = _ws_matmul(
                [hidden], [ffn_w2], pairs=[(0, 0)],
                epilogue=functools.partial(_ep_residual, tm=tm2),
                extras=[(x_mid, (tm2, tn2), lambda n, m: (m, n)),
                        (mod, (None, MOD_ROWS, tn2),
                         lambda n, m: (layer, 0, 5 * (D_MODEL // tn2) + n))],
                group=_layer_group(i, n_rows // tm2), m_tiles=n_rows // tm2, tm=tm2, tn=tn2,
                n_out=D_MODEL, out_dtype=F32, name="ffn_down"), None
        else:
            i = layer // 2
            w_router = jnp.pad(router[i], ((0, 0), (0, LANES - N_EXPERTS)))
            h2_chunks, route = _norm_route(x_mid, norm2, mod, layer, 3, 4, w_router, n_rows)
            pos, tok_of_row, tile_expert, tile_first, n_used = _routing_tables(route)
            rows = _dispatch(h2_chunks, tok_of_row, n_used)
            moe_tiles = MOE_ROWS // MOE_TILE
            n_stack = moe_w1.shape[0] * N_EXPERTS
            w1s = moe_w1.reshape(n_stack, D_MODEL, D_FF_EXPERT)
            w3s = moe_w3.reshape(n_stack, D_MODEL, D_FF_EXPERT)
            w2s = moe_w2.reshape(n_stack, D_FF_EXPERT, D_MODEL)
            tile_group = tile_expert + i * N_EXPERTS
            runs = _run_table(tile_group)
            hidden = _ws_matmul(
                [rows], [w1s, w3s], pairs=[(0, 0), (0, 1)], epilogue=_ep_swiglu,
                group=tile_group, tile_first=tile_first, runs=runs, m_tiles=moe_tiles,
                tm=MOE_TILE, tn=1024, n_out=D_FF_EXPERT, out_dtype=BF16, name="moe_up")
            y_rows = _ws_matmul(
                [hidden], [w2s], pairs=[(0, 0)], epilogue=_ep_plain,
                group=tile_group, tile_first=tile_first, runs=runs, m_tiles=moe_tiles,
                tm=MOE_TILE, tn=512, n_out=D_MODEL, out_dtype=F32, name="moe_down")
            out = _combine(y_rows, pos, x_mid, route, mod, layer, 5, norm_f)

    return out.reshape(BATCH, SEQ, D_MODEL)
```

```python
import functools

import numpy as np
import jax
import jax.numpy as jnp
from jax import lax
from jax.experimental import pallas as pl
from jax.experimental.pallas import tpu as pltpu

D_MODEL = 2048
BATCH = 4
SEQ = 2048
DEPTH = 2
GRID_W = 64
CTX_LEN = 256
N_HEADS = 16
N_KV_HEADS = 4
GROUP = N_HEADS // N_KV_HEADS
HEAD_DIM = 128
WINDOW = 128
BLOCK = 128
ROPE_THETA = 10000.0
D_FF = 5632
N_EXPERTS = 8
TOP_K = 2
D_FF_EXPERT = 7168
RMS_EPS = 1e-6
NEG_INF = -1e30
LOG2_E = 1.4426950408889634
N_MOD = 6
Q_W = N_HEADS * HEAD_DIM
KV_W = N_KV_HEADS * HEAD_DIM
QKV_W = Q_W + 2 * KV_W
REST_W = 3 * D_MODEL + 2 * D_MODEL

N_LAT = BATCH * SEQ
N_CTX = BATCH * CTX_LEN
N_ALL = N_LAT + N_CTX
CTX_MOD_ROW = BATCH
MOD_ROWS = 8

LANES = 128
VMEM_LIMIT = 60 * 1024 * 1024
ROW_TILE = 1024
MOE_TILE = 512
MOE_ROWS = TOP_K * N_LAT + N_EXPERTS * MOE_TILE
PART_ROWS = 128

F32 = jnp.float32
BF16 = jnp.bfloat16


def _params(n_axes):
    return pltpu.CompilerParams(
        dimension_semantics=("arbitrary",) * n_axes, vmem_limit_bytes=VMEM_LIMIT)


def _sigmoid(v):
    return 1.0 / (1.0 + jnp.exp(-v))


def _mod_row(m_tile_start_row):
    return jnp.minimum(m_tile_start_row // SEQ, CTX_MOD_ROW)


def _modulation_kernel(c_ref, w_ref, b_ref, o_ref):
    cv = c_ref[...]
    act = (cv * _sigmoid(cv)).astype(BF16)
    o_ref[...] = jnp.dot(act, w_ref[...].astype(BF16), preferred_element_type=F32) + b_ref[...]


def _modulation(cvec, w_mod, b_mod):
    tn = 1024
    n_tiles = N_MOD * D_MODEL // tn
    return pl.pallas_call(
        _modulation_kernel,
        grid=(DEPTH, n_tiles),
        in_specs=[
            pl.BlockSpec((MOD_ROWS, D_MODEL), lambda l, n: (0, 0)),
            pl.BlockSpec((None, D_MODEL, tn), lambda l, n: (l, 0, n)),
            pl.BlockSpec((None, 1, tn), lambda l, n: (l, 0, n)),
        ],
        out_specs=pl.BlockSpec((None, MOD_ROWS, tn), lambda l, n: (l, 0, n)),
        out_shape=jax.ShapeDtypeStruct((DEPTH, MOD_ROWS, N_MOD * D_MODEL), F32),
        compiler_params=_params(2),
        name="modulation",
    )(cvec, w_mod, b_mod.reshape(DEPTH, 1, N_MOD * D_MODEL))


def _norm_kernel(x_ref, *refs, tm, tail_tiles):
    m = pl.program_id(0)
    row = _mod_row(m * tm)
    if tail_tiles is None:
        g_ref, sh_ref, sc_ref, o_ref = refs
        xf = x_ref[...]
    else:
        x_tail_ref, g_ref, sh_ref, sc_ref, o_ref = refs
        xf = jnp.where(m < tail_tiles, x_ref[...], x_tail_ref[...])
    inv = lax.rsqrt(jnp.mean(xf * xf, axis=-1, keepdims=True) + RMS_EPS)
    y = (xf * inv) * g_ref[...]
    y = y * (1.0 + sc_ref[pl.ds(row, 1), :]) + sh_ref[pl.ds(row, 1), :]
    o_ref[...] = y.astype(o_ref.dtype)


def _norm_mod(x_rows, gain, mod, layer, shift_chunk, scale_chunk, n_rows, out_dtype, x_tail=None):
    tm = 512
    xs = [x_rows]
    x_specs = [pl.BlockSpec((tm, D_MODEL), lambda m: (m, 0))]
    tail_tiles = None
    if x_tail is not None:
        tail_tiles = x_rows.shape[0] // tm
        xs.append(x_tail)
        x_specs = [pl.BlockSpec((tm, D_MODEL), lambda m: (jnp.minimum(m, tail_tiles - 1), 0)),
                   pl.BlockSpec((tm, D_MODEL), lambda m: (jnp.maximum(m - tail_tiles, 0), 0))]
    return pl.pallas_call(
        functools.partial(_norm_kernel, tm=tm, tail_tiles=tail_tiles),
        grid=(n_rows // tm,),
        in_specs=x_specs + [
            pl.BlockSpec((None, 1, D_MODEL), lambda m: (layer, 0, 0)),
            pl.BlockSpec((None, MOD_ROWS, D_MODEL), lambda m: (layer, 0, shift_chunk)),
            pl.BlockSpec((None, MOD_ROWS, D_MODEL), lambda m: (layer, 0, scale_chunk)),
        ],
        out_specs=pl.BlockSpec((tm, D_MODEL), lambda m: (m, 0)),
        out_shape=jax.ShapeDtypeStruct((n_rows, D_MODEL), out_dtype),
        compiler_params=_params(1),
        name="norm_mod",
    )(*xs, gain.reshape(DEPTH, 1, D_MODEL), mod, mod)


def _ws_kernel(group_ref, first_ref, runs_ref, *refs, n_lhs, n_w, n_extra, pairs, epilogue,
               tail_tiles, tm, tn, col_blk, n_sub, partial):
    lhs_refs = refs[:n_lhs]
    w_refs = refs[n_lhs:n_lhs + n_w]
    ex_refs = refs[n_lhs + n_w:n_lhs + n_w + n_extra]
    out_ref, wbuf, sem, slot_ref = refs[n_lhs + n_w + n_extra:]
    n = pl.program_id(0)
    m = pl.program_id(1)

    def weight_copies(col, grp, slot):
        cols = pl.ds(pl.multiple_of((col + col_blk) * tn, tn), tn)
        return [pltpu.make_async_copy(w_ref.at[grp, :, cols], wbuf.at[slot, wi], sem.at[slot, wi])
                for wi, w_ref in enumerate(w_refs)]

    @pl.when((n == 0) & (m == 0))
    def _():
        slot_ref[0] = 0
        for copy in weight_copies(0, group_ref[0], 0):
            copy.start()

    cur = slot_ref[0]

    @pl.when(runs_ref[0, m] == 1)
    def _():
        for copy in weight_copies(n, group_ref[m], cur):
            copy.wait()
        next_col = n + runs_ref[3, m]

        @pl.when(next_col < pl.num_programs(0))
        def _():
            for copy in weight_copies(next_col, runs_ref[2, m], 1 - cur):
                copy.start()

    def lhs_tile(li, rs):
        if li == 0 and tail_tiles is not None:
            return jnp.where(m < tail_tiles, lhs_refs[0][rs, :], lhs_refs[n_lhs - 1][rs, :])
        return lhs_refs[li][rs, :]

    def compute(row_slices):
        w_bf = [wbuf[cur, wi].astype(BF16) for wi in range(n_w)]
        for rs in row_slices:
            dots = [jnp.dot(lhs_tile(li, rs), w_bf[wi], preferred_element_type=F32)
                    for li, wi in pairs]
            out_ref[rs, :] = epilogue(dots, ex_refs, n, m, rs).astype(out_ref.dtype)

    def slices(size):
        return [slice(r, r + size) for r in range(0, tm, size)]

    if not partial:
        compute(slices(tm // n_sub))
    else:
        first = first_ref[m]

        @pl.when(first == 0)
        def _():
            compute(slices(tm // n_sub))

        for rs in slices(PART_ROWS):
            @pl.when((first > 0) & (rs.stop > first))
            def _():
                compute([rs])

            @pl.when((first > 0) & (rs.stop <= first))
            def _():
                out_ref[rs, :] = jnp.zeros((PART_ROWS, out_ref.shape[1]), out_ref.dtype)

    @pl.when(runs_ref[1, m] == 1)
    def _():
        slot_ref[0] = 1 - cur


def _ws_matmul(lhs, weights, *, pairs, epilogue, extras=(), group, tile_first=None,
               m_tiles, tm, tn, n_out, col_off=0, lhs_row_off=0, lhs0_tail=None, n_sub=1,
               out_dtype, name):
    k_dim = lhs[0].shape[1]
    col_blk = col_off
    row_blk = lhs_row_off
    partial = tile_first is not None
    if tile_first is None:
        tile_first = jnp.zeros((m_tiles,), jnp.int32)
    runs = _run_table(group)

    def wrap(fn):
        return lambda n, m, g, u, r: fn(n, m)

    in_specs = [pl.BlockSpec((tm, k_dim), lambda n, m, g, u, r: (m + row_blk, 0)) for _ in lhs]
    tail_tiles = None
    if lhs0_tail is not None:
        tail_tiles = lhs[0].shape[0] // tm
        in_specs[0] = pl.BlockSpec(
            (tm, k_dim), lambda n, m, g, u, r: (jnp.minimum(m, tail_tiles - 1), 0))
        in_specs.append(pl.BlockSpec(
            (tm, k_dim), lambda n, m, g, u, r: (jnp.maximum(m - tail_tiles, 0), 0)))
        lhs = list(lhs) + [lhs0_tail]
    in_specs += [pl.BlockSpec(memory_space=pl.ANY) for _ in weights]
    scratch = [pltpu.VMEM((2, len(weights), k_dim, tn), F32),
               pltpu.SemaphoreType.DMA((2, len(weights))),
               pltpu.SMEM((1,), jnp.int32)]
    in_specs += [pl.BlockSpec(shape, wrap(fn)) for _, shape, fn in extras]
    grid_spec = pltpu.PrefetchScalarGridSpec(
        num_scalar_prefetch=3,
        grid=(n_out // tn, m_tiles),
        in_specs=in_specs,
        out_specs=pl.BlockSpec((tm, tn), lambda n, m, g, u, r: (m, n)),
        scratch_shapes=scratch,
    )
    kern = functools.partial(
        _ws_kernel, n_lhs=len(lhs), n_w=len(weights), n_extra=len(extras), pairs=pairs,
        epilogue=epilogue, tail_tiles=tail_tiles, tm=tm, tn=tn, col_blk=col_blk, n_sub=n_sub,
        partial=partial)
    return pl.pallas_call(
        kern,
        grid_spec=grid_spec,
        out_shape=jax.ShapeDtypeStruct((m_tiles * tm, n_out), out_dtype),
        compiler_params=_params(2),
        name=name,
    )(group, tile_first, runs, *lhs, *weights, *[e[0] for e in extras])


def _run_table(group):
    m_tiles = group.shape[0]
    idx = jnp.arange(m_tiles, dtype=jnp.int32)
    start = group != jnp.concatenate([group[:1] - 1, group[:-1]])
    end = group != jnp.concatenate([group[1:], group[-1:] - 1])
    start_pos = jnp.where(start, idx, m_tiles)
    after = jnp.concatenate([start_pos[1:], jnp.full((1,), m_tiles, jnp.int32)])
    next_start = lax.cummin(after[::-1])[::-1]
    wraps = next_start >= m_tiles
    next_group = group[jnp.where(wraps, 0, next_start)]
    return jnp.stack([start, end, next_group, wraps]).astype(jnp.int32)


def _layer_group(layer, m_tiles):
    return jnp.full((m_tiles,), layer, jnp.int32)


def _ep_plain(dots, ex, n, m, rs):
    return dots[0]


def _ep_rope(dots, ex, n, m, rs):
    cos_ref, sa_ref, sb_ref = ex
    d = dots[0]
    cos, sa, sb = cos_ref[rs, :], sa_ref[rs, :], sb_ref[rs, :]
    quarter = HEAD_DIM // 4
    heads = []
    for h in range(d.shape[1] // HEAD_DIM):
        t = d[:, h * HEAD_DIM:(h + 1) * HEAD_DIM]
        heads.append(t * cos + pltpu.roll(t, HEAD_DIM - quarter, 1) * sa
                     + pltpu.roll(t, quarter, 1) * sb)
    return jnp.concatenate(heads, axis=1)


def _ep_merge(dots, ex, n, m, rs):
    ga_ref, gc_ref = ex
    return (_sigmoid(ga_ref[rs, :].astype(F32)) * dots[0]
            + _sigmoid(gc_ref[rs, :].astype(F32)) * dots[1])


def _ep_residual(dots, ex, n, m, rs, *, tm, tail_tiles=None):
    if tail_tiles is None:
        x_ref, gate_ref = ex
        x = x_ref[rs, :]
    else:
        x_ref, x_tail_ref, gate_ref = ex
        x = jnp.where(m < tail_tiles, x_ref[rs, :], x_tail_ref[rs, :])
    row = _mod_row(m * tm)
    return x + gate_ref[pl.ds(row, 1), :] * dots[0]


def _ep_swiglu(dots, ex, n, m, rs):
    a = dots[0]
    return (a * _sigmoid(a)) * dots[1]


def _rope_tables(tm):
    quarter = HEAD_DIM // 4
    inv_freq = np.power(np.float32(ROPE_THETA),
                        -np.arange(quarter, dtype=np.float32) / np.float32(quarter)).astype(np.float32)
    pos = np.arange(SEQ)
    rows = (pos // GRID_W).astype(np.float32)
    cols = (pos % GRID_W).astype(np.float32)
    ang = np.concatenate([rows[:, None] * inv_freq[None, :]] * 2
                         + [cols[:, None] * inv_freq[None, :]] * 2, axis=1).astype(np.float32)
    cos = np.cos(ang).astype(np.float32)
    sin = np.sin(ang).astype(np.float32)
    lane = np.arange(HEAD_DIM)
    first = ((lane // quarter) % 2 == 0)[None, :]
    sa = np.where(first, -sin, 0.0).astype(np.float32)
    sb = np.where(first, 0.0, sin).astype(np.float32)
    ones = np.ones((tm, HEAD_DIM), np.float32)
    zeros = np.zeros((tm, HEAD_DIM), np.float32)
    scale = np.float32(HEAD_DIM ** -0.5 * LOG2_E)

    def variants(tbl, ident):
        k = np.concatenate([tbl, ident], axis=0)
        ident_all = np.concatenate([np.broadcast_to(ident[:1], tbl.shape), ident], axis=0)
        return np.stack([k * scale, k, ident_all])

    return (jnp.asarray(variants(cos, ones)), jnp.asarray(variants(sa, zeros)),
            jnp.asarray(variants(sb, zeros)))


ATTN_ROWS = 2 * BLOCK
BAND = 3 * BLOCK


def _band_masks():
    r = np.arange(BLOCK)[:, None]
    c = np.arange(BAND)[None, :]
    masks = [(np.abs(off * BLOCK + r - c) <= WINDOW).astype(np.float32) for off in range(3)]
    return jnp.asarray(np.stack([np.tile(m, (GROUP, 1)) for m in masks]))


def _stack_heads(q):
    return jnp.concatenate([q[:, h * HEAD_DIM:(h + 1) * HEAD_DIM] for h in range(GROUP)], axis=0)


def _unstack_heads(o):
    return jnp.concatenate([o[h * BLOCK:(h + 1) * BLOCK, :] for h in range(GROUP)], axis=1)


def _sink_column(sink_ref, g):
    return LOG2_E * jnp.concatenate(
        [jnp.broadcast_to(sink_ref[pl.ds(g * GROUP + h, 1), 0:1], (BLOCK, 1)) for h in range(GROUP)],
        axis=0)


def _softmax_pv(s, sink, vals):
    mx = jnp.maximum(jnp.max(s, axis=-1, keepdims=True), sink)
    p = jnp.exp2(s - mx)
    denom = jnp.sum(p, axis=-1, keepdims=True) + jnp.exp2(sink - mx)
    return jnp.dot(p.astype(BF16), vals, preferred_element_type=F32) / denom


_CONTRACT_LAST = (((1,), (1,)), ((), ()))


def _attn_latent_kernel(sink_ref, mask_ref, q_ref, k_ref, v_ref, kx_ref, vx_ref, o_ref, s_even, s_odd):
    n_blocks = SEQ // BLOCK
    sink = _sink_column(sink_ref, pl.program_id(1))

    def band_start(i):
        first = jnp.clip(i - 1, 0, n_blocks - 3)
        return first, pl.multiple_of(first * BLOCK, BLOCK)

    def scores(i, s_ref):
        first, start = band_start(i)
        qs = _stack_heads(q_ref[pl.ds(pl.multiple_of(i * BLOCK, BLOCK), BLOCK), :])
        s_loc = lax.dot_general(qs, k_ref[pl.ds(start, BAND), :], _CONTRACT_LAST,
                                preferred_element_type=F32)
        s_ref[:, 0:BAND] = jnp.where(mask_ref[i - first] > 0.5, s_loc, NEG_INF)
        s_ref[:, BAND:] = lax.dot_general(qs, kx_ref[...], _CONTRACT_LAST,
                                          preferred_element_type=F32)

    def finish(i, s_ref):
        _, start = band_start(i)
        vals = jnp.concatenate([v_ref[pl.ds(start, BAND), :], vx_ref[...]], axis=0)
        o = _softmax_pv(s_ref[...], sink, vals)
        o_ref[pl.ds(pl.multiple_of(i * BLOCK, BLOCK), BLOCK), :] = _unstack_heads(o).astype(o_ref.dtype)

    scores(0, s_even)

    def pair(t, carry):
        i = 2 * t
        scores(i + 1, s_odd)
        finish(i, s_even)
        scores(jnp.minimum(i + 2, n_blocks - 1), s_even)
        finish(i + 1, s_odd)
        return carry

    lax.fori_loop(0, n_blocks // 2, pair, 0)


def _attn_ctx_kernel(sink_ref, q_ref, kx_ref, vx_ref, o_ref):
    width = GROUP * HEAD_DIM
    for g in range(N_KV_HEADS):
        sink = _sink_column(sink_ref, g)
        kx = kx_ref[:, g * HEAD_DIM:(g + 1) * HEAD_DIM]
        vx = vx_ref[:, g * HEAD_DIM:(g + 1) * HEAD_DIM]
        for sub in range(CTX_LEN // BLOCK):
            rows = slice(sub * BLOCK, (sub + 1) * BLOCK)
            cols = slice(g * width, (g + 1) * width)
            qs = _stack_heads(q_ref[rows, cols])
            s = lax.dot_general(qs, kx, _CONTRACT_LAST, preferred_element_type=F32)
            o_ref[rows, cols] = _unstack_heads(_softmax_pv(s, sink, vx)).astype(o_ref.dtype)


def _attention(qkv, kv_ctx, sink_tbl, masks, *, with_ctx_queries, ctx_row_blk, ctx_k_col, ctx_v_col):
    k_col = Q_W // HEAD_DIM
    v_col = (Q_W + KV_W) // HEAD_DIM
    width = GROUP * HEAD_DIM
    sink_spec = pl.BlockSpec((N_HEADS, LANES), lambda b, g: (0, 0))
    kx_spec = pl.BlockSpec((CTX_LEN, HEAD_DIM), lambda b, g: (ctx_row_blk + b, ctx_k_col + g))
    vx_spec = pl.BlockSpec((CTX_LEN, HEAD_DIM), lambda b, g: (ctx_row_blk + b, ctx_v_col + g))
    y = pl.pallas_call(
        _attn_latent_kernel,
        grid=(BATCH, N_KV_HEADS),
        in_specs=[
            sink_spec,
            pl.BlockSpec((3, GROUP * BLOCK, BAND), lambda b, g: (0, 0, 0)),
            pl.BlockSpec((SEQ, width), lambda b, g: (b, g)),
            pl.BlockSpec((SEQ, HEAD_DIM), lambda b, g: (b, k_col + g)),
            pl.BlockSpec((SEQ, HEAD_DIM), lambda b, g: (b, v_col + g)),
            kx_spec, vx_spec,
        ],
        out_specs=pl.BlockSpec((SEQ, width), lambda b, g: (b, g)),
        out_shape=jax.ShapeDtypeStruct((N_LAT, Q_W), BF16),
        scratch_shapes=[pltpu.VMEM((GROUP * BLOCK, BAND + CTX_LEN), F32),
                        pltpu.VMEM((GROUP * BLOCK, BAND + CTX_LEN), F32)],
        compiler_params=_params(2),
        name="attention",
    )(sink_tbl, masks, qkv, qkv, qkv, kv_ctx, kv_ctx)
    if not with_ctx_queries:
        return y, None
    ctx_blk = N_LAT // CTX_LEN
    kv_blk = KV_W // HEAD_DIM
    y_ctx = pl.pallas_call(
        _attn_ctx_kernel,
        grid=(BATCH,),
        in_specs=[
            pl.BlockSpec((N_HEADS, LANES), lambda b: (0, 0)),
            pl.BlockSpec((CTX_LEN, Q_W), lambda b: (ctx_blk + b, 0)),
            pl.BlockSpec((CTX_LEN, KV_W), lambda b: (ctx_row_blk + b, ctx_k_col // kv_blk)),
            pl.BlockSpec((CTX_LEN, KV_W), lambda b: (ctx_row_blk + b, ctx_v_col // kv_blk)),
        ],
        out_specs=pl.BlockSpec((CTX_LEN, Q_W), lambda b: (b, 0)),
        out_shape=jax.ShapeDtypeStruct((N_CTX, Q_W), BF16),
        compiler_params=_params(1),
        name="attention_ctx",
    )(sink_tbl, qkv, kv_ctx, kv_ctx)
    return y, y_ctx


CONV_ROWS = 256
HALO = 16


def _conv_kernel(w_ref, u_ref, gb_ref, gc_ref, up_ref, gcp_ref, un_ref, gcn_ref, o_ref):
    r = pl.program_id(0)
    lat_blocks = N_LAT // CONV_ROWS
    per_seq = SEQ // CONV_ROWS
    is_ctx = r >= lat_blocks
    seq_start = is_ctx | (r % per_seq == 0)
    seq_end = is_ctx | (r % per_seq == per_seq - 1)
    z = gc_ref[...].astype(F32) * u_ref[...].astype(F32)
    z_before = (gcp_ref[HALO - 1:HALO, :].astype(F32) * up_ref[HALO - 1:HALO, :].astype(F32))
    z_after = gcn_ref[0:1, :].astype(F32) * un_ref[0:1, :].astype(F32)
    z_before = jnp.where(seq_start, 0.0, z_before)
    z_after = jnp.where(seq_end, 0.0, z_after)
    row = lax.broadcasted_iota(jnp.int32, z.shape, 0)
    z_prev = jnp.where(row == 0, z_before, pltpu.roll(z, 1, 0))
    z_next = jnp.where(row == CONV_ROWS - 1, z_after, pltpu.roll(z, CONV_ROWS - 1, 0))
    y = z_prev * w_ref[0:1, :] + z * w_ref[1:2, :] + z_next * w_ref[2:3, :]
    o_ref[...] = (gb_ref[...].astype(F32) * y).astype(o_ref.dtype)


def _short_conv(rest, conv_w, layer, n_rows):
    n_blocks = n_rows // CONV_ROWS
    tc = D_MODEL
    halo_per_block = CONV_ROWS // HALO
    last_halo = n_rows // HALO - 1

    def main(chunk):
        return pl.BlockSpec((CONV_ROWS, tc), lambda r: (r, chunk))

    def before(chunk):
        return pl.BlockSpec((HALO, tc), lambda r: (jnp.maximum(r * halo_per_block - 1, 0), chunk))

    def after(chunk):
        return pl.BlockSpec((HALO, tc),
                            lambda r: (jnp.minimum((r + 1) * halo_per_block, last_halo), chunk))

    return pl.pallas_call(
        _conv_kernel,
        grid=(n_blocks,),
        in_specs=[
            pl.BlockSpec((None, 8, tc), lambda r: (layer, 0, 0)),
            main(0), main(1), main(2), before(0), before(2), after(0), after(2),
        ],
        out_specs=pl.BlockSpec((CONV_ROWS, tc), lambda r: (r, 0)),
        out_shape=jax.ShapeDtypeStruct((n_rows, D_MODEL), BF16),
        compiler_params=_params(1),
        name="short_conv",
    )(conv_w, rest, rest, rest, rest, rest, rest, rest)


ROW_CHUNKS = D_MODEL // LANES


def _norm_route_kernel(x_ref, g_ref, sh_ref, sc_ref, w_ref, chunks_ref, route_ref, *, tm):
    row = _mod_row(pl.program_id(0) * tm)
    xf = x_ref[...]
    inv = lax.rsqrt(jnp.mean(xf * xf, axis=-1, keepdims=True) + RMS_EPS)
    h = (xf * inv) * g_ref[...]
    h = h * (1.0 + sc_ref[pl.ds(row, 1), :]) + sh_ref[pl.ds(row, 1), :]
    for j in range(ROW_CHUNKS):
        chunks_ref[pl.ds(j, tm, stride=ROW_CHUNKS), :] = h[:, j * LANES:(j + 1) * LANES]
    route_ref[...] = _route(h, w_ref[...])


def _norm_route(x_rows, gain, mod, layer, shift_chunk, scale_chunk, w_router_padded, n_rows):
    tm = 512
    return pl.pallas_call(
        functools.partial(_norm_route_kernel, tm=tm),
        grid=(n_rows // tm,),
        in_specs=[
            pl.BlockSpec((tm, D_MODEL), lambda m: (m, 0)),
            pl.BlockSpec((None, 1, D_MODEL), lambda m: (layer, 0, 0)),
            pl.BlockSpec((None, MOD_ROWS, D_MODEL), lambda m: (layer, 0, shift_chunk)),
            pl.BlockSpec((None, MOD_ROWS, D_MODEL), lambda m: (layer, 0, scale_chunk)),
            pl.BlockSpec((D_MODEL, LANES), lambda m: (0, 0)),
        ],
        out_specs=[pl.BlockSpec((tm * ROW_CHUNKS, LANES), lambda m: (m, 0)),
                   pl.BlockSpec((tm, LANES), lambda m: (m, 0))],
        out_shape=[jax.ShapeDtypeStruct((n_rows * ROW_CHUNKS, LANES), F32),
                   jax.ShapeDtypeStruct((n_rows, LANES), F32)],
        compiler_params=_params(1),
        name="norm_route",
    )(x_rows, gain.reshape(DEPTH, 1, D_MODEL), mod, mod, w_router_padded)


def _route(h, w):
    h_hi = h.astype(BF16)
    h_lo = (h - h_hi.astype(F32)).astype(BF16)
    w_hi = w.astype(BF16)
    w_lo = (w - w_hi.astype(F32)).astype(BF16)
    logits = (jnp.dot(h_hi, w_hi, preferred_element_type=F32)
              + jnp.dot(h_hi, w_lo, preferred_element_type=F32)
              + jnp.dot(h_lo, w_hi, preferred_element_type=F32))
    lane = lax.broadcasted_iota(jnp.int32, logits.shape, 1)
    lowest = float(jnp.finfo(F32).min)
    logits = jnp.where(lane < N_EXPERTS, logits, lowest)
    v1 = jnp.max(logits, axis=-1, keepdims=True)
    i1 = jnp.min(jnp.where(logits == v1, lane, LANES), axis=-1, keepdims=True)
    rest = jnp.where(lane == i1, lowest, logits)
    v2 = jnp.max(rest, axis=-1, keepdims=True)
    i2 = jnp.min(jnp.where(rest == v2, lane, LANES), axis=-1, keepdims=True)
    e2 = jnp.exp(v2 - v1)
    p1 = 1.0 / (1.0 + e2)
    p2 = e2 / (1.0 + e2)
    return jnp.where(lane == 0, i1.astype(F32),
                     jnp.where(lane == 1, i2.astype(F32),
                               jnp.where(lane == 2, p1, jnp.where(lane == 3, p2, 0.0))))


DISPATCH_TILE = 256


DMA_UNROLL = 8


def _dispatch_kernel(nused_ref, tok_ref, tok_next_ref, h_hbm, o_ref, buf, sem):
    t = pl.program_id(0)
    slot = t % 2
    n_active = nused_ref[0] * (MOE_TILE // DISPATCH_TILE)
    tile_chunks = DISPATCH_TILE * ROW_CHUNKS

    def gather(idx_ref, dst_slot):
        def body(i, carry):
            for u in range(DMA_UNROLL):
                j = i * DMA_UNROLL + u
                src = pl.multiple_of(idx_ref[0, j] * ROW_CHUNKS, ROW_CHUNKS)
                pltpu.make_async_copy(
                    h_hbm.at[pl.ds(src, ROW_CHUNKS), :],
                    buf.at[dst_slot, pl.ds(j * ROW_CHUNKS, ROW_CHUNKS), :],
                    sem.at[dst_slot]).start(priority=u % 2)
            return carry
        lax.fori_loop(0, DISPATCH_TILE // DMA_UNROLL, body, 0)

    @pl.when((t == 0) & (n_active > 0))
    def _():
        gather(tok_ref, 0)

    @pl.when(t + 1 < n_active)
    def _():
        gather(tok_next_ref, 1 - slot)

    @pl.when(t < n_active)
    def _():
        pltpu.make_async_copy(h_hbm.at[pl.ds(0, tile_chunks), :], buf.at[slot], sem.at[slot]).wait()
        cols = [buf[slot, pl.ds(j, DISPATCH_TILE, stride=ROW_CHUNKS), :] for j in range(ROW_CHUNKS)]
        o_ref[...] = jnp.concatenate(cols, axis=1).astype(o_ref.dtype)

    @pl.when(t >= n_active)
    def _():
        o_ref[...] = jnp.zeros_like(o_ref)


def _dispatch(h_chunks, tok_of_row, n_used):
    n_rows = tok_of_row.shape[0]
    n_steps = n_rows // DISPATCH_TILE
    tok_blocks = tok_of_row.reshape(n_steps, 1, DISPATCH_TILE)
    grid_spec = pltpu.PrefetchScalarGridSpec(
        num_scalar_prefetch=1,
        grid=(n_steps,),
        in_specs=[
            pl.BlockSpec((None, 1, DISPATCH_TILE), lambda t, u: (t, 0, 0), memory_space=pltpu.SMEM),
            pl.BlockSpec((None, 1, DISPATCH_TILE),
                         lambda t, u: (jnp.minimum(t + 1, n_steps - 1), 0, 0),
                         memory_space=pltpu.SMEM),
            pl.BlockSpec(memory_space=pl.ANY),
        ],
        out_specs=pl.BlockSpec((DISPATCH_TILE, D_MODEL), lambda t, u: (t, 0)),
        scratch_shapes=[pltpu.VMEM((2, DISPATCH_TILE * ROW_CHUNKS, LANES), F32),
                        pltpu.SemaphoreType.DMA((2,))],
    )
    return pl.pallas_call(
        _dispatch_kernel,
        grid_spec=grid_spec,
        out_shape=jax.ShapeDtypeStruct((n_rows, D_MODEL), BF16),
        compiler_params=_params(1),
        name="moe_dispatch",
    )(n_used, tok_blocks, tok_blocks, h_chunks)


COMBINE_TILE = 256


def _combine_kernel(pos_ref, pos_next_ref, y_hbm, x_ref, route_ref, gate_ref, gain_ref, o_ref,
                    buf, sem):
    t = pl.program_id(0)
    n_steps = pl.num_programs(0)
    slot = t % 2

    def gather(idx_ref, dst_slot):
        def body(i, carry):
            for u in range(DMA_UNROLL):
                j = i * DMA_UNROLL + u
                for k in range(TOP_K):
                    pltpu.make_async_copy(
                        y_hbm.at[pl.ds(idx_ref[0, TOP_K * j + k], 1), :],
                        buf.at[dst_slot, k, pl.ds(j, 1), :],
                        sem.at[dst_slot]).start(priority=k)
            return carry
        lax.fori_loop(0, COMBINE_TILE // DMA_UNROLL, body, 0)

    @pl.when(t == 0)
    def _():
        gather(pos_ref, 0)

    @pl.when(t + 1 < n_steps)
    def _():
        gather(pos_next_ref, 1 - slot)

    for k in range(TOP_K):
        pltpu.make_async_copy(
            y_hbm.at[pl.ds(0, COMBINE_TILE), :], buf.at[slot, k], sem.at[slot]).wait()
    route = route_ref[...]
    moe = route[:, 2:3] * buf[slot, 0] + route[:, 3:4] * buf[slot, 1]
    row = _mod_row(t * COMBINE_TILE)
    xn = x_ref[...] + gate_ref[pl.ds(row, 1), :] * moe
    inv = lax.rsqrt(jnp.mean(xn * xn, axis=-1, keepdims=True) + RMS_EPS)
    o_ref[...] = (xn * inv) * gain_ref[...]


def _combine(y_rows, pos, x_rows, route, mod, layer, gate_chunk, norm_f):
    n_steps = N_LAT // COMBINE_TILE
    pos_blocks = pos.reshape(n_steps, 1, TOP_K * COMBINE_TILE)
    idx_shape = (None, 1, TOP_K * COMBINE_TILE)
    return pl.pallas_call(
        _combine_kernel,
        grid=(n_steps,),
        in_specs=[
            pl.BlockSpec(idx_shape, lambda t: (t, 0, 0), memory_space=pltpu.SMEM),
            pl.BlockSpec(idx_shape, lambda t: (jnp.minimum(t + 1, n_steps - 1), 0, 0),
                         memory_space=pltpu.SMEM),
            pl.BlockSpec(memory_space=pl.ANY),
            pl.BlockSpec((COMBINE_TILE, D_MODEL), lambda t: (t, 0)),
            pl.BlockSpec((COMBINE_TILE, LANES), lambda t: (t, 0)),
            pl.BlockSpec((None, MOD_ROWS, D_MODEL), lambda t: (layer, 0, gate_chunk)),
            pl.BlockSpec((1, D_MODEL), lambda t: (0, 0)),
        ],
        out_specs=pl.BlockSpec((COMBINE_TILE, D_MODEL), lambda t: (t, 0)),
        out_shape=jax.ShapeDtypeStruct((N_LAT, D_MODEL), F32),
        scratch_shapes=[pltpu.VMEM((2, TOP_K, COMBINE_TILE, D_MODEL), F32),
                        pltpu.SemaphoreType.DMA((2,))],
        compiler_params=_params(1),
        name="moe_combine",
    )(pos_blocks, pos_blocks, y_rows, x_rows, route, mod, norm_f.reshape(1, D_MODEL))


def _routing_tables(route):
    experts = route[:, :TOP_K].astype(jnp.int32).reshape(-1)
    onehot = (experts[:, None] == jnp.arange(N_EXPERTS, dtype=jnp.int32)[None, :]).astype(jnp.int32)
    rank = jnp.sum((jnp.cumsum(onehot, axis=0) - onehot) * onehot, axis=1)
    counts = jnp.sum(onehot, axis=0)
    padded = ((counts + MOE_TILE - 1) // MOE_TILE) * MOE_TILE
    ends = jnp.cumsum(padded)
    starts = ends - counts
    pos = (jnp.sum(onehot * starts[None, :], axis=1) + rank).astype(jnp.int32)
    tok_of_row = jnp.zeros((MOE_ROWS,), jnp.int32).at[pos].set(
        jnp.arange(TOP_K * N_LAT, dtype=jnp.int32) // TOP_K)
    n_tiles = MOE_ROWS // MOE_TILE
    n_used = (ends[-1] // MOE_TILE).astype(jnp.int32)
    tile_start = jnp.arange(n_tiles, dtype=jnp.int32) * MOE_TILE
    tile_expert = jnp.sum((tile_start[:, None] >= ends[None, :]).astype(jnp.int32), axis=1)
    last_expert = jnp.sum((tile_start[jnp.maximum(n_used - 1, 0)] >= ends).astype(jnp.int32))
    in_use = jnp.arange(n_tiles) < n_used
    tile_expert = jnp.where(in_use, tile_expert, last_expert)
    tile_expert = jnp.minimum(tile_expert, N_EXPERTS - 1).astype(jnp.int32)
    tile_first = jnp.clip(starts[tile_expert] - tile_start, 0, MOE_TILE)
    tile_first = jnp.where(in_use, tile_first, MOE_TILE).astype(jnp.int32)
    return pos, tok_of_row, tile_expert, tile_first, n_used.reshape(1)


def kernel(x, c, ctx, c_ctx, w_mod, b_mod, norm1, w_in, sink, conv_w, w_o_attn, w_o_conv, w_out,
           norm2, ffn_w1, ffn_w3, ffn_w2, router, moe_w1, moe_w3, moe_w2, norm_f):
    tm = ROW_TILE
    cvec = jnp.concatenate(
        [c, c_ctx[None, :], jnp.zeros((MOD_ROWS - BATCH - 1, D_MODEL), F32)], axis=0)
    mod = _modulation(cvec, w_mod, b_mod)
    x_main, x_tail = x.reshape(N_LAT, D_MODEL), ctx.reshape(N_CTX, D_MODEL)
    cos_t, sa_t, sb_t = _rope_tables(tm)
    conv_w8 = jnp.pad(conv_w, ((0, 0), (0, 8 - conv_w.shape[1]), (0, 0)))
    sink_tbl = jnp.broadcast_to(sink[:, :, None], (DEPTH, N_HEADS, LANES))
    masks = _band_masks()
    seq_tiles = SEQ // tm
    lat_tiles = N_LAT // tm
    out = None

    for layer in range(DEPTH):
        ctx_out = layer < DEPTH - 1
        n_rows = N_ALL if ctx_out else N_LAT
        m_tiles = n_rows // tm
        grp = _layer_group(layer, m_tiles)
        chunk = D_MODEL // 512

        h = _norm_mod(x_main, norm1, mod, layer, 0, 1, N_ALL, BF16, x_tail=x_tail)
        tn_qkv = 512

        def variant(n):
            return jnp.where(n < Q_W // tn_qkv, 0, jnp.where(n < (Q_W + KV_W) // tn_qkv, 1, 2))

        def tbl_map(n, m):
            return (variant(n), jnp.where(m < lat_tiles, m % seq_tiles, seq_tiles), 0)

        tbl_shape = (None, tm, HEAD_DIM)
        qkv = _ws_matmul(
            [h], [w_in], pairs=[(0, 0)], epilogue=_ep_rope,
            extras=[(cos_t, tbl_shape, tbl_map), (sa_t, tbl_shape, tbl_map),
                    (sb_t, tbl_shape, tbl_map)],
            group=grp, m_tiles=m_tiles, tm=tm, tn=tn_qkv, n_out=QKV_W, n_sub=2,
            out_dtype=BF16, name="in_proj_qkv")
        rest = _ws_matmul(
            [h], [w_in], pairs=[(0, 0)], epilogue=_ep_plain,
            group=grp, m_tiles=m_tiles, tm=tm, tn=1024, n_out=REST_W,
            col_off=QKV_W // 1024, out_dtype=BF16, name="in_proj_rest")
        if ctx_out:
            kv_ctx, ctx_row_blk = qkv, N_LAT // CTX_LEN
            ctx_k_col, ctx_v_col = Q_W // HEAD_DIM, (Q_W + KV_W) // HEAD_DIM
        else:
            kv_ctx = _ws_matmul(
                [h], [w_in], pairs=[(0, 0)], epilogue=_ep_plain,
                group=_layer_group(layer, N_CTX // tm), m_tiles=N_CTX // tm, tm=tm, tn=512,
                n_out=2 * KV_W, col_off=Q_W // 512, lhs_row_off=lat_tiles, out_dtype=BF16,
                name="in_proj_ctx_kv")
            ctx_row_blk, ctx_k_col, ctx_v_col = 0, 0, KV_W // HEAD_DIM
        y_attn, y_attn_ctx = _attention(
            qkv, kv_ctx, sink_tbl[layer], masks, with_ctx_queries=ctx_out,
            ctx_row_blk=ctx_row_blk, ctx_k_col=ctx_k_col, ctx_v_col=ctx_v_col)
        y_conv = _short_conv(rest, conv_w8, layer, n_rows)
        tn = 512
        gate_blk = D_MODEL // tn
        merged = _ws_matmul(
            [y_attn, y_conv], [w_o_attn, w_o_conv], pairs=[(0, 0), (1, 1)], epilogue=_ep_merge,
            extras=[(rest, (tm, tn), lambda n, m: (m, 3 * gate_blk + n)),
                    (rest, (tm, tn), lambda n, m: (m, 4 * gate_blk + n))],
            group=grp, m_tiles=m_tiles, tm=tm, tn=tn, n_out=D_MODEL,
            lhs0_tail=y_attn_ctx, out_dtype=BF16, name="merge")
        gate_spec = (mod, (None, MOD_ROWS, tn), lambda n, m: (layer, 0, 2 * chunk + n))
        if x_tail is None:
            res_extras = [(x_main, (tm, tn), lambda n, m: (m, n)), gate_spec]
            res_tail = None
        else:
            res_tail = x_main.shape[0] // tm
            res_extras = [
                (x_main, (tm, tn), lambda n, m: (jnp.minimum(m, res_tail - 1), n)),
                (x_tail, (tm, tn), lambda n, m: (jnp.maximum(m - res_tail, 0), n)),
                gate_spec]
        x_mid = _ws_matmul(
            [merged], [w_out], pairs=[(0, 0)],
            epilogue=functools.partial(_ep_residual, tm=tm, tail_tiles=res_tail),
            extras=res_extras, group=grp, m_tiles=m_tiles, tm=tm, tn=tn, n_out=D_MODEL,
            out_dtype=F32, name="out_proj")

        if layer % 2 == 0:
            i = layer // 2
            h2 = _norm_mod(x_mid, norm2, mod, layer, 3, 4, n_rows, BF16)
            hidden = _ws_matmul(
                [h2], [ffn_w1, ffn_w3], pairs=[(0, 0), (0, 1)], epilogue=_ep_swiglu,
                group=_layer_group(i, m_tiles), m_tiles=m_tiles, tm=tm, tn=512, n_out=D_FF,
                out_dtype=BF16, name="ffn_up")
            tm2, tn2 = 512, 512
            x_main, x_tail = _ws_matmul(
                [hidden], [ffn_w2], pairs=[(0, 0)],
                epilogue=functools.partial(_ep_residual, tm=tm2),
                extras=[(x_mid, (tm2, tn2), lambda n, m: (m, n)),
                        (mod, (None, MOD_ROWS, tn2),
                         lambda n, m: (layer, 0, 5 * (D_MODEL // tn2) + n))],
                group=_layer_group(i, n_rows // tm2), m_tiles=n_rows // tm2, tm=tm2, tn=tn2,
                n_out=D_MODEL, out_dtype=F32, name="ffn_down"), None
        else:
            i = layer // 2
            w_router = jnp.pad(router[i], ((0, 0), (0, LANES - N_EXPERTS)))
            h2_chunks, route = _norm_route(x_mid, norm2, mod, layer, 3, 4, w_router, n_rows)
            pos, tok_of_row, tile_expert, tile_first, n_used = _routing_tables(route)
            rows = _dispatch(h2_chunks, tok_of_row, n_used)
            moe_tiles = MOE_ROWS // MOE_TILE
            n_stack = moe_w1.shape[0] * N_EXPERTS
            w1s = moe_w1.reshape(n_stack, D_MODEL, D_FF_EXPERT)
            w3s = moe_w3.reshape(n_stack, D_MODEL, D_FF_EXPERT)
            w2s = moe_w2.reshape(n_stack, D_FF_EXPERT, D_MODEL)
            tile_group = tile_expert + i * N_EXPERTS
            hidden = _ws_matmul(
                [rows], [w1s, w3s], pairs=[(0, 0), (0, 1)], epilogue=_ep_swiglu,
                group=tile_group, tile_first=tile_first, m_tiles=moe_tiles,
                tm=MOE_TILE, tn=1024, n_out=D_FF_EXPERT, out_dtype=BF16, name="moe_up")
            y_rows = _ws_matmul(
                [hidden], [w2s], pairs=[(0, 0)], epilogue=_ep_plain,
                group=tile_group, tile_first=tile_first, m_tiles=moe_tiles,
                tm=MOE_TILE, tn=512, n_out=D_MODEL, out_dtype=F32, name="moe_down")
            out = _combine(y_rows, pos, x_mid, route, mod, layer, 5, norm_f)

    return out.reshape(BATCH, SEQ, D_MODEL)
```

```python
import functools

import numpy as np
import jax
import jax.numpy as jnp
from jax import lax
from jax.experimental import pallas as pl
from jax.experimental.pallas import tpu as pltpu

D_MODEL = 2048
BATCH = 4
SEQ = 2048
DEPTH = 2
GRID_W = 64
CTX_LEN = 256
N_HEADS = 16
N_KV_HEADS = 4
GROUP = N_HEADS // N_KV_HEADS
HEAD_DIM = 128
WINDOW = 128
BLOCK = 128
ROPE_THETA = 10000.0
D_FF = 5632
N_EXPERTS = 8
TOP_K = 2
D_FF_EXPERT = 7168
RMS_EPS = 1e-6
NEG_INF = -1e30
LOG2_E = 1.4426950408889634
N_MOD = 6
Q_W = N_HEADS * HEAD_DIM
KV_W = N_KV_HEADS * HEAD_DIM
QKV_W = Q_W + 2 * KV_W
REST_W = 3 * D_MODEL + 2 * D_MODEL

N_LAT = BATCH * SEQ
N_CTX = BATCH * CTX_LEN
N_ALL = N_LAT + N_CTX
CTX_MOD_ROW = BATCH
MOD_ROWS = 8

LANES = 128
VMEM_LIMIT = 60 * 1024 * 1024
ROW_TILE = 1024
MOE_TILE = 512
MOE_ROWS = TOP_K * N_LAT + N_EXPERTS * MOE_TILE
PART_ROWS = 128

F32 = jnp.float32
BF16 = jnp.bfloat16


def _params(n_axes):
    return pltpu.CompilerParams(
        dimension_semantics=("arbitrary",) * n_axes, vmem_limit_bytes=VMEM_LIMIT)


def _sigmoid(v):
    return 1.0 / (1.0 + jnp.exp(-v))


def _mod_row(m_tile_start_row):
    return jnp.minimum(m_tile_start_row // SEQ, CTX_MOD_ROW)


def _modulation_kernel(c_ref, w_ref, b_ref, o_ref):
    cv = c_ref[...]
    act = (cv * _sigmoid(cv)).astype(BF16)
    o_ref[...] = jnp.dot(act, w_ref[...].astype(BF16), preferred_element_type=F32) + b_ref[...]


def _modulation(cvec, w_mod, b_mod):
    tn = 1024
    n_tiles = N_MOD * D_MODEL // tn
    return pl.pallas_call(
        _modulation_kernel,
        grid=(DEPTH, n_tiles),
        in_specs=[
            pl.BlockSpec((MOD_ROWS, D_MODEL), lambda l, n: (0, 0)),
            pl.BlockSpec((None, D_MODEL, tn), lambda l, n: (l, 0, n)),
            pl.BlockSpec((None, 1, tn), lambda l, n: (l, 0, n)),
        ],
        out_specs=pl.BlockSpec((None, MOD_ROWS, tn), lambda l, n: (l, 0, n)),
        out_shape=jax.ShapeDtypeStruct((DEPTH, MOD_ROWS, N_MOD * D_MODEL), F32),
        compiler_params=_params(2),
        name="modulation",
    )(cvec, w_mod, b_mod.reshape(DEPTH, 1, N_MOD * D_MODEL))


def _norm_kernel(x_ref, *refs, tm, tail_tiles):
    m = pl.program_id(0)
    row = _mod_row(m * tm)
    if tail_tiles is None:
        g_ref, sh_ref, sc_ref, o_ref = refs
        xf = x_ref[...]
    else:
        x_tail_ref, g_ref, sh_ref, sc_ref, o_ref = refs
        xf = jnp.where(m < tail_tiles, x_ref[...], x_tail_ref[...])
    inv = lax.rsqrt(jnp.mean(xf * xf, axis=-1, keepdims=True) + RMS_EPS)
    y = (xf * inv) * g_ref[...]
    y = y * (1.0 + sc_ref[pl.ds(row, 1), :]) + sh_ref[pl.ds(row, 1), :]
    o_ref[...] = y.astype(o_ref.dtype)


def _norm_mod(x_rows, gain, mod, layer, shift_chunk, scale_chunk, n_rows, out_dtype, x_tail=None):
    tm = 512
    xs = [x_rows]
    x_specs = [pl.BlockSpec((tm, D_MODEL), lambda m: (m, 0))]
    tail_tiles = None
    if x_tail is not None:
        tail_tiles = x_rows.shape[0] // tm
        xs.append(x_tail)
        x_specs = [pl.BlockSpec((tm, D_MODEL), lambda m: (jnp.minimum(m, tail_tiles - 1), 0)),
                   pl.BlockSpec((tm, D_MODEL), lambda m: (jnp.maximum(m - tail_tiles, 0), 0))]
    return pl.pallas_call(
        functools.partial(_norm_kernel, tm=tm, tail_tiles=tail_tiles),
        grid=(n_rows // tm,),
        in_specs=x_specs + [
            pl.BlockSpec((None, 1, D_MODEL), lambda m: (layer, 0, 0)),
            pl.BlockSpec((None, MOD_ROWS, D_MODEL), lambda m: (layer, 0, shift_chunk)),
            pl.BlockSpec((None, MOD_ROWS, D_MODEL), lambda m: (layer, 0, scale_chunk)),
        ],
        out_specs=pl.BlockSpec((tm, D_MODEL), lambda m: (m, 0)),
        out_shape=jax.ShapeDtypeStruct((n_rows, D_MODEL), out_dtype),
        compiler_params=_params(1),
        name="norm_mod",
    )(*xs, gain.reshape(DEPTH, 1, D_MODEL), mod, mod)


def _ws_kernel(group_ref, first_ref, runs_ref, *refs, n_lhs, n_w, n_extra, pairs, epilogue,
               tail_tiles, tm, tn, col_blk, n_sub, partial):
    lhs_refs = refs[:n_lhs]
    w_refs = refs[n_lhs:n_lhs + n_w]
    ex_refs = refs[n_lhs + n_w:n_lhs + n_w + n_extra]
    out_ref, wbuf, sem, slot_ref = refs[n_lhs + n_w + n_extra:]
    n = pl.program_id(0)
    m = pl.program_id(1)

    def weight_copies(col, grp, slot):
        cols = pl.ds(pl.multiple_of((col + col_blk) * tn, tn), tn)
        return [pltpu.make_async_copy(w_ref.at[grp, :, cols], wbuf.at[slot, wi], sem.at[slot, wi])
                for wi, w_ref in enumerate(w_refs)]

    @pl.when((n == 0) & (m == 0))
    def _():
        slot_ref[0] = 0
        for copy in weight_copies(0, group_ref[0], 0):
            copy.start()

    cur = slot_ref[0]

    @pl.when(runs_ref[0, m] == 1)
    def _():
        for copy in weight_copies(n, group_ref[m], cur):
            copy.wait()
        next_col = n + runs_ref[3, m]

        @pl.when(next_col < pl.num_programs(0))
        def _():
            for copy in weight_copies(next_col, runs_ref[2, m], 1 - cur):
                copy.start()

    def lhs_tile(li, rs):
        if li == 0 and tail_tiles is not None:
            return jnp.where(m < tail_tiles, lhs_refs[0][rs, :], lhs_refs[n_lhs - 1][rs, :])
        return lhs_refs[li][rs, :]

    def compute(row_slices):
        w_bf = [wbuf[cur, wi].astype(BF16) for wi in range(n_w)]
        for rs in row_slices:
            dots = [jnp.dot(lhs_tile(li, rs), w_bf[wi], preferred_element_type=F32)
                    for li, wi in pairs]
            out_ref[rs, :] = epilogue(dots, ex_refs, n, m, rs).astype(out_ref.dtype)

    def slices(size):
        return [slice(r, r + size) for r in range(0, tm, size)]

    if not partial:
        compute(slices(tm // n_sub))
    else:
        first = first_ref[m]

        @pl.when(first == 0)
        def _():
            compute(slices(tm // n_sub))

        for rs in slices(PART_ROWS):
            @pl.when((first > 0) & (rs.stop > first))
            def _():
                compute([rs])

            @pl.when((first > 0) & (rs.stop <= first))
            def _():
                out_ref[rs, :] = jnp.zeros((PART_ROWS, out_ref.shape[1]), out_ref.dtype)

    @pl.when(runs_ref[1, m] == 1)
    def _():
        slot_ref[0] = 1 - cur


def _ws_matmul(lhs, weights, *, pairs, epilogue, extras=(), group, tile_first=None,
               m_tiles, tm, tn, n_out, col_off=0, lhs_row_off=0, lhs0_tail=None, n_sub=1,
               out_dtype, name):
    k_dim = lhs[0].shape[1]
    col_blk = col_off
    row_blk = lhs_row_off
    partial = tile_first is not None
    if tile_first is None:
        tile_first = jnp.zeros((m_tiles,), jnp.int32)
    runs = _run_table(group)

    def wrap(fn):
        return lambda n, m, g, u, r: fn(n, m)

    in_specs = [pl.BlockSpec((tm, k_dim), lambda n, m, g, u, r: (m + row_blk, 0)) for _ in lhs]
    tail_tiles = None
    if lhs0_tail is not None:
        tail_tiles = lhs[0].shape[0] // tm
        in_specs[0] = pl.BlockSpec(
            (tm, k_dim), lambda n, m, g, u, r: (jnp.minimum(m, tail_tiles - 1), 0))
        in_specs.append(pl.BlockSpec(
            (tm, k_dim), lambda n, m, g, u, r: (jnp.maximum(m - tail_tiles, 0), 0)))
        lhs = list(lhs) + [lhs0_tail]
    in_specs += [pl.BlockSpec(memory_space=pl.ANY) for _ in weights]
    scratch = [pltpu.VMEM((2, len(weights), k_dim, tn), F32),
               pltpu.SemaphoreType.DMA((2, len(weights))),
               pltpu.SMEM((1,), jnp.int32)]
    in_specs += [pl.BlockSpec(shape, wrap(fn)) for _, shape, fn in extras]
    grid_spec = pltpu.PrefetchScalarGridSpec(
        num_scalar_prefetch=3,
        grid=(n_out // tn, m_tiles),
        in_specs=in_specs,
        out_specs=pl.BlockSpec((tm, tn), lambda n, m, g, u, r: (m, n)),
        scratch_shapes=scratch,
    )
    kern = functools.partial(
        _ws_kernel, n_lhs=len(lhs), n_w=len(weights), n_extra=len(extras), pairs=pairs,
        epilogue=epilogue, tail_tiles=tail_tiles, tm=tm, tn=tn, col_blk=col_blk, n_sub=n_sub,
        partial=partial)
    return pl.pallas_call(
        kern,
        grid_spec=grid_spec,
        out_shape=jax.ShapeDtypeStruct((m_tiles * tm, n_out), out_dtype),
        compiler_params=_params(2),
        name=name,
    )(group, tile_first, runs, *lhs, *weights, *[e[0] for e in extras])


def _run_table(group):
    m_tiles = group.shape[0]
    idx = jnp.arange(m_tiles, dtype=jnp.int32)
    start = group != jnp.concatenate([group[:1] - 1, group[:-1]])
    end = group != jnp.concatenate([group[1:], group[-1:] - 1])
    start_pos = jnp.where(start, idx, m_tiles)
    after = jnp.concatenate([start_pos[1:], jnp.full((1,), m_tiles, jnp.int32)])
    next_start = lax.cummin(after[::-1])[::-1]
    wraps = next_start >= m_tiles
    next_group = group[jnp.where(wraps, 0, next_start)]
    return jnp.stack([start, end, next_group, wraps]).astype(jnp.int32)


def _layer_group(layer, m_tiles):
    return jnp.full((m_tiles,), layer, jnp.int32)


def _ep_plain(dots, ex, n, m, rs):
    return dots[0]


def _ep_rope(dots, ex, n, m, rs):
    cos_ref, sa_ref, sb_ref = ex
    d = dots[0]
    cos, sa, sb = cos_ref[rs, :], sa_ref[rs, :], sb_ref[rs, :]
    quarter = HEAD_DIM // 4
    heads = []
    for h in range(d.shape[1] // HEAD_DIM):
        t = d[:, h * HEAD_DIM:(h + 1) * HEAD_DIM]
        heads.append(t * cos + pltpu.roll(t, HEAD_DIM - quarter, 1) * sa
                     + pltpu.roll(t, quarter, 1) * sb)
    return jnp.concatenate(heads, axis=1)


def _ep_merge(dots, ex, n, m, rs):
    ga_ref, gc_ref = ex
    return (_sigmoid(ga_ref[rs, :].astype(F32)) * dots[0]
            + _sigmoid(gc_ref[rs, :].astype(F32)) * dots[1])


def _ep_residual(dots, ex, n, m, rs, *, tm, tail_tiles=None):
    if tail_tiles is None:
        x_ref, gate_ref = ex
        x = x_ref[rs, :]
    else:
        x_ref, x_tail_ref, gate_ref = ex
        x = jnp.where(m < tail_tiles, x_ref[rs, :], x_tail_ref[rs, :])
    row = _mod_row(m * tm)
    return x + gate_ref[pl.ds(row, 1), :] * dots[0]


def _ep_swiglu(dots, ex, n, m, rs):
    a = dots[0]
    return (a * _sigmoid(a)) * dots[1]


def _rope_tables(tm):
    quarter = HEAD_DIM // 4
    inv_freq = np.power(np.float32(ROPE_THETA),
                        -np.arange(quarter, dtype=np.float32) / np.float32(quarter)).astype(np.float32)
    pos = np.arange(SEQ)
    rows = (pos // GRID_W).astype(np.float32)
    cols = (pos % GRID_W).astype(np.float32)
    ang = np.concatenate([rows[:, None] * inv_freq[None, :]] * 2
                         + [cols[:, None] * inv_freq[None, :]] * 2, axis=1).astype(np.float32)
    cos = np.cos(ang).astype(np.float32)
    sin = np.sin(ang).astype(np.float32)
    lane = np.arange(HEAD_DIM)
    first = ((lane // quarter) % 2 == 0)[None, :]
    sa = np.where(first, -sin, 0.0).astype(np.float32)
    sb = np.where(first, 0.0, sin).astype(np.float32)
    ones = np.ones((tm, HEAD_DIM), np.float32)
    zeros = np.zeros((tm, HEAD_DIM), np.float32)
    scale = np.float32(HEAD_DIM ** -0.5 * LOG2_E)

    def variants(tbl, ident):
        k = np.concatenate([tbl, ident], axis=0)
        ident_all = np.concatenate([np.broadcast_to(ident[:1], tbl.shape), ident], axis=0)
        return np.stack([k * scale, k, ident_all])

    return (jnp.asarray(variants(cos, ones)), jnp.asarray(variants(sa, zeros)),
            jnp.asarray(variants(sb, zeros)))


ATTN_ROWS = 2 * BLOCK
BAND = 3 * BLOCK


def _band_masks():
    r = np.arange(BLOCK)[:, None]
    c = np.arange(BAND)[None, :]
    masks = [(np.abs(off * BLOCK + r - c) <= WINDOW).astype(np.float32) for off in range(3)]
    return jnp.asarray(np.stack([np.tile(m, (GROUP, 1)) for m in masks]))


def _stack_heads(q):
    return jnp.concatenate([q[:, h * HEAD_DIM:(h + 1) * HEAD_DIM] for h in range(GROUP)], axis=0)


def _unstack_heads(o):
    return jnp.concatenate([o[h * BLOCK:(h + 1) * BLOCK, :] for h in range(GROUP)], axis=1)


def _sink_column(sink_ref, g):
    return LOG2_E * jnp.concatenate(
        [jnp.broadcast_to(sink_ref[pl.ds(g * GROUP + h, 1), 0:1], (BLOCK, 1)) for h in range(GROUP)],
        axis=0)


def _softmax_pv(s, sink, vals):
    mx = jnp.maximum(jnp.max(s, axis=-1, keepdims=True), sink)
    p = jnp.exp2(s - mx)
    denom = jnp.sum(p, axis=-1, keepdims=True) + jnp.exp2(sink - mx)
    return jnp.dot(p.astype(BF16), vals, preferred_element_type=F32) / denom


_CONTRACT_LAST = (((1,), (1,)), ((), ()))


def _attn_latent_kernel(sink_ref, mask_ref, q_ref, k_ref, v_ref, kx_ref, vx_ref, o_ref, s_even, s_odd):
    n_blocks = SEQ // BLOCK
    sink = _sink_column(sink_ref, pl.program_id(1))

    def band_start(i):
        first = jnp.clip(i - 1, 0, n_blocks - 3)
        return first, pl.multiple_of(first * BLOCK, BLOCK)

    def scores(i, s_ref):
        first, start = band_start(i)
        qs = _stack_heads(q_ref[pl.ds(pl.multiple_of(i * BLOCK, BLOCK), BLOCK), :])
        s_loc = lax.dot_general(qs, k_ref[pl.ds(start, BAND), :], _CONTRACT_LAST,
                                preferred_element_type=F32)
        s_ref[:, 0:BAND] = jnp.where(mask_ref[i - first] > 0.5, s_loc, NEG_INF)
        s_ref[:, BAND:] = lax.dot_general(qs, kx_ref[...], _CONTRACT_LAST,
                                          preferred_element_type=F32)

    def finish(i, s_ref):
        _, start = band_start(i)
        vals = jnp.concatenate([v_ref[pl.ds(start, BAND), :], vx_ref[...]], axis=0)
        o = _softmax_pv(s_ref[...], sink, vals)
        o_ref[pl.ds(pl.multiple_of(i * BLOCK, BLOCK), BLOCK), :] = _unstack_heads(o).astype(o_ref.dtype)

    scores(0, s_even)

    def pair(t, carry):
        i = 2 * t
        scores(i + 1, s_odd)
        finish(i, s_even)
        scores(jnp.minimum(i + 2, n_blocks - 1), s_even)
        finish(i + 1, s_odd)
        return carry

    lax.fori_loop(0, n_blocks // 2, pair, 0)


def _attn_ctx_kernel(sink_ref, q_ref, kx_ref, vx_ref, o_ref):
    width = GROUP * HEAD_DIM
    for g in range(N_KV_HEADS):
        sink = _sink_column(sink_ref, g)
        kx = kx_ref[:, g * HEAD_DIM:(g + 1) * HEAD_DIM]
        vx = vx_ref[:, g * HEAD_DIM:(g + 1) * HEAD_DIM]
        for sub in range(CTX_LEN // BLOCK):
            rows = slice(sub * BLOCK, (sub + 1) * BLOCK)
            cols = slice(g * width, (g + 1) * width)
            qs = _stack_heads(q_ref[rows, cols])
            s = lax.dot_general(qs, kx, _CONTRACT_LAST, preferred_element_type=F32)
            o_ref[rows, cols] = _unstack_heads(_softmax_pv(s, sink, vx)).astype(o_ref.dtype)


def _attention(qkv, kv_ctx, sink_tbl, masks, *, with_ctx_queries, ctx_row_blk, ctx_k_col, ctx_v_col):
    k_col = Q_W // HEAD_DIM
    v_col = (Q_W + KV_W) // HEAD_DIM
    width = GROUP * HEAD_DIM
    sink_spec = pl.BlockSpec((N_HEADS, LANES), lambda b, g: (0, 0))
    kx_spec = pl.BlockSpec((CTX_LEN, HEAD_DIM), lambda b, g: (ctx_row_blk + b, ctx_k_col + g))
    vx_spec = pl.BlockSpec((CTX_LEN, HEAD_DIM), lambda b, g: (ctx_row_blk + b, ctx_v_col + g))
    y = pl.pallas_call(
        _attn_latent_kernel,
        grid=(BATCH, N_KV_HEADS),
        in_specs=[
            sink_spec,
            pl.BlockSpec((3, GROUP * BLOCK, BAND), lambda b, g: (0, 0, 0)),
            pl.BlockSpec((SEQ, width), lambda b, g: (b, g)),
            pl.BlockSpec((SEQ, HEAD_DIM), lambda b, g: (b, k_col + g)),
            pl.BlockSpec((SEQ, HEAD_DIM), lambda b, g: (b, v_col + g)),
            kx_spec, vx_spec,
        ],
        out_specs=pl.BlockSpec((SEQ, width), lambda b, g: (b, g)),
        out_shape=jax.ShapeDtypeStruct((N_LAT, Q_W), BF16),
        scratch_shapes=[pltpu.VMEM((GROUP * BLOCK, BAND + CTX_LEN), F32),
                        pltpu.VMEM((GROUP * BLOCK, BAND + CTX_LEN), F32)],
        compiler_params=_params(2),
        name="attention",
    )(sink_tbl, masks, qkv, qkv, qkv, kv_ctx, kv_ctx)
    if not with_ctx_queries:
        return y, None
    ctx_blk = N_LAT // CTX_LEN
    kv_blk = KV_W // HEAD_DIM
    y_ctx = pl.pallas_call(
        _attn_ctx_kernel,
        grid=(BATCH,),
        in_specs=[
            pl.BlockSpec((N_HEADS, LANES), lambda b: (0, 0)),
            pl.BlockSpec((CTX_LEN, Q_W), lambda b: (ctx_blk + b, 0)),
            pl.BlockSpec((CTX_LEN, KV_W), lambda b: (ctx_row_blk + b, ctx_k_col // kv_blk)),
            pl.BlockSpec((CTX_LEN, KV_W), lambda b: (ctx_row_blk + b, ctx_v_col // kv_blk)),
        ],
        out_specs=pl.BlockSpec((CTX_LEN, Q_W), lambda b: (b, 0)),
        out_shape=jax.ShapeDtypeStruct((N_CTX, Q_W), BF16),
        compiler_params=_params(1),
        name="attention_ctx",
    )(sink_tbl, qkv, kv_ctx, kv_ctx)
    return y, y_ctx


CONV_ROWS = CTX_LEN
CONV_ROWS_LATENT = 512
HALO = 16


def _conv_kernel(w_ref, u_ref, gb_ref, gc_ref, up_ref, gcp_ref, un_ref, gcn_ref, o_ref, *, rows):
    r = pl.program_id(0)
    lat_blocks = N_LAT // rows
    per_seq = SEQ // rows
    is_ctx = r >= lat_blocks
    seq_start = is_ctx | (r % per_seq == 0)
    seq_end = is_ctx | (r % per_seq == per_seq - 1)
    z = gc_ref[...].astype(F32) * u_ref[...].astype(F32)
    z_before = (gcp_ref[HALO - 1:HALO, :].astype(F32) * up_ref[HALO - 1:HALO, :].astype(F32))
    z_after = gcn_ref[0:1, :].astype(F32) * un_ref[0:1, :].astype(F32)
    z_before = jnp.where(seq_start, 0.0, z_before)
    z_after = jnp.where(seq_end, 0.0, z_after)
    row = lax.broadcasted_iota(jnp.int32, z.shape, 0)
    z_prev = jnp.where(row == 0, z_before, pltpu.roll(z, 1, 0))
    z_next = jnp.where(row == rows - 1, z_after, pltpu.roll(z, rows - 1, 0))
    y = z_prev * w_ref[0:1, :] + z * w_ref[1:2, :] + z_next * w_ref[2:3, :]
    o_ref[...] = (gb_ref[...].astype(F32) * y).astype(o_ref.dtype)


def _short_conv(rest, conv_w, layer, n_rows):
    rows = CONV_ROWS_LATENT if n_rows == N_LAT else CONV_ROWS
    n_blocks = n_rows // rows
    tc = D_MODEL
    halo_per_block = rows // HALO
    last_halo = n_rows // HALO - 1

    def main(chunk):
        return pl.BlockSpec((rows, tc), lambda r: (r, chunk))

    def before(chunk):
        return pl.BlockSpec((HALO, tc), lambda r: (jnp.maximum(r * halo_per_block - 1, 0), chunk))

    def after(chunk):
        return pl.BlockSpec((HALO, tc),
                            lambda r: (jnp.minimum((r + 1) * halo_per_block, last_halo), chunk))

    return pl.pallas_call(
        functools.partial(_conv_kernel, rows=rows),
        grid=(n_blocks,),
        in_specs=[
            pl.BlockSpec((None, 8, tc), lambda r: (layer, 0, 0)),
            main(0), main(1), main(2), before(0), before(2), after(0), after(2),
        ],
        out_specs=pl.BlockSpec((rows, tc), lambda r: (r, 0)),
        out_shape=jax.ShapeDtypeStruct((n_rows, D_MODEL), BF16),
        compiler_params=_params(1),
        name="short_conv",
    )(conv_w, rest, rest, rest, rest, rest, rest, rest)


ROW_CHUNKS = D_MODEL // LANES


def _norm_route_kernel(x_ref, g_ref, sh_ref, sc_ref, w_ref, chunks_ref, route_ref, *, tm):
    row = _mod_row(pl.program_id(0) * tm)
    xf = x_ref[...]
    inv = lax.rsqrt(jnp.mean(xf * xf, axis=-1, keepdims=True) + RMS_EPS)
    h = (xf * inv) * g_ref[...]
    h = h * (1.0 + sc_ref[pl.ds(row, 1), :]) + sh_ref[pl.ds(row, 1), :]
    for j in range(ROW_CHUNKS):
        chunks_ref[pl.ds(j, tm, stride=ROW_CHUNKS), :] = h[:, j * LANES:(j + 1) * LANES]
    route_ref[...] = _route(h, w_ref[...])


def _norm_route(x_rows, gain, mod, layer, shift_chunk, scale_chunk, w_router_padded, n_rows):
    tm = 512
    return pl.pallas_call(
        functools.partial(_norm_route_kernel, tm=tm),
        grid=(n_rows // tm,),
        in_specs=[
            pl.BlockSpec((tm, D_MODEL), lambda m: (m, 0)),
            pl.BlockSpec((None, 1, D_MODEL), lambda m: (layer, 0, 0)),
            pl.BlockSpec((None, MOD_ROWS, D_MODEL), lambda m: (layer, 0, shift_chunk)),
            pl.BlockSpec((None, MOD_ROWS, D_MODEL), lambda m: (layer, 0, scale_chunk)),
            pl.BlockSpec((D_MODEL, LANES), lambda m: (0, 0)),
        ],
        out_specs=[pl.BlockSpec((tm * ROW_CHUNKS, LANES), lambda m: (m, 0)),
                   pl.BlockSpec((tm, LANES), lambda m: (m, 0))],
        out_shape=[jax.ShapeDtypeStruct((n_rows * ROW_CHUNKS, LANES), F32),
                   jax.ShapeDtypeStruct((n_rows, LANES), F32)],
        compiler_params=_params(1),
        name="norm_route",
    )(x_rows, gain.reshape(DEPTH, 1, D_MODEL), mod, mod, w_router_padded)


def _route(h, w):
    h_hi = h.astype(BF16)
    h_lo = (h - h_hi.astype(F32)).astype(BF16)
    w_hi = w.astype(BF16)
    w_lo = (w - w_hi.astype(F32)).astype(BF16)
    logits = (jnp.dot(h_hi, w_hi, preferred_element_type=F32)
              + jnp.dot(h_hi, w_lo, preferred_element_type=F32)
              + jnp.dot(h_lo, w_hi, preferred_element_type=F32))
    lane = lax.broadcasted_iota(jnp.int32, logits.shape, 1)
    lowest = float(jnp.finfo(F32).min)
    logits = jnp.where(lane < N_EXPERTS, logits, lowest)
    v1 = jnp.max(logits, axis=-1, keepdims=True)
    i1 = jnp.min(jnp.where(logits == v1, lane, LANES), axis=-1, keepdims=True)
    rest = jnp.where(lane == i1, lowest, logits)
    v2 = jnp.max(rest, axis=-1, keepdims=True)
    i2 = jnp.min(jnp.where(rest == v2, lane, LANES), axis=-1, keepdims=True)
    e2 = jnp.exp(v2 - v1)
    p1 = 1.0 / (1.0 + e2)
    p2 = e2 / (1.0 + e2)
    return jnp.where(lane == 0, i1.astype(F32),
                     jnp.where(lane == 1, i2.astype(F32),
                               jnp.where(lane == 2, p1, jnp.where(lane == 3, p2, 0.0))))


DISPATCH_TILE = 512


DMA_UNROLL = 8


def _dispatch_kernel(nused_ref, tok_ref, tok_next_ref, h_hbm, o_ref, buf, sem):
    t = pl.program_id(0)
    slot = t % 2
    n_active = nused_ref[0] * (MOE_TILE // DISPATCH_TILE)
    tile_chunks = DISPATCH_TILE * ROW_CHUNKS

    def gather(idx_ref, dst_slot):
        def body(i, carry):
            for u in range(DMA_UNROLL):
                j = i * DMA_UNROLL + u
                src = pl.multiple_of(idx_ref[0, j] * ROW_CHUNKS, ROW_CHUNKS)
                pltpu.make_async_copy(
                    h_hbm.at[pl.ds(src, ROW_CHUNKS), :],
                    buf.at[dst_slot, pl.ds(j * ROW_CHUNKS, ROW_CHUNKS), :],
                    sem.at[dst_slot]).start(priority=u % 2)
            return carry
        lax.fori_loop(0, DISPATCH_TILE // DMA_UNROLL, body, 0)

    @pl.when((t == 0) & (n_active > 0))
    def _():
        gather(tok_ref, 0)

    @pl.when(t + 1 < n_active)
    def _():
        gather(tok_next_ref, 1 - slot)

    @pl.when(t < n_active)
    def _():
        pltpu.make_async_copy(h_hbm.at[pl.ds(0, tile_chunks), :], buf.at[slot], sem.at[slot]).wait()
        cols = [buf[slot, pl.ds(j, DISPATCH_TILE, stride=ROW_CHUNKS), :] for j in range(ROW_CHUNKS)]
        o_ref[...] = jnp.concatenate(cols, axis=1).astype(o_ref.dtype)

    @pl.when(t >= n_active)
    def _():
        o_ref[...] = jnp.zeros_like(o_ref)


def _dispatch(h_chunks, tok_of_row, n_used):
    n_rows = tok_of_row.shape[0]
    n_steps = n_rows // DISPATCH_TILE
    tok_blocks = tok_of_row.reshape(n_steps, 1, DISPATCH_TILE)
    grid_spec = pltpu.PrefetchScalarGridSpec(
        num_scalar_prefetch=1,
        grid=(n_steps,),
        in_specs=[
            pl.BlockSpec((None, 1, DISPATCH_TILE), lambda t, u: (t, 0, 0), memory_space=pltpu.SMEM),
            pl.BlockSpec((None, 1, DISPATCH_TILE),
                         lambda t, u: (jnp.minimum(t + 1, n_steps - 1), 0, 0),
                         memory_space=pltpu.SMEM),
            pl.BlockSpec(memory_space=pl.ANY),
        ],
        out_specs=pl.BlockSpec((DISPATCH_TILE, D_MODEL), lambda t, u: (t, 0)),
        scratch_shapes=[pltpu.VMEM((2, DISPATCH_TILE * ROW_CHUNKS, LANES), F32),
                        pltpu.SemaphoreType.DMA((2,))],
    )
    return pl.pallas_call(
        _dispatch_kernel,
        grid_spec=grid_spec,
        out_shape=jax.ShapeDtypeStruct((n_rows, D_MODEL), BF16),
        compiler_params=_params(1),
        name="moe_dispatch",
    )(n_used, tok_blocks, tok_blocks, h_chunks)


COMBINE_TILE = 512


def _combine_kernel(pos_ref, pos_next_ref, y_hbm, x_ref, route_ref, gate_ref, gain_ref, o_ref,
                    buf, sem):
    t = pl.program_id(0)
    n_steps = pl.num_programs(0)
    slot = t % 2

    def gather(idx_ref, dst_slot):
        def body(i, carry):
            for u in range(DMA_UNROLL):
                j = i * DMA_UNROLL + u
                for k in range(TOP_K):
                    pltpu.make_async_copy(
                        y_hbm.at[pl.ds(idx_ref[0, TOP_K * j + k], 1), :],
                        buf.at[dst_slot, k, pl.ds(j, 1), :],
                        sem.at[dst_slot]).start(priority=k)
            return carry
        lax.fori_loop(0, COMBINE_TILE // DMA_UNROLL, body, 0)

    @pl.when(t == 0)
    def _():
        gather(pos_ref, 0)

    @pl.when(t + 1 < n_steps)
    def _():
        gather(pos_next_ref, 1 - slot)

    for k in range(TOP_K):
        pltpu.make_async_copy(
            y_hbm.at[pl.ds(0, COMBINE_TILE), :], buf.at[slot, k], sem.at[slot]).wait()
    route = route_ref[...]
    moe = route[:, 2:3] * buf[slot, 0] + route[:, 3:4] * buf[slot, 1]
    row = _mod_row(t * COMBINE_TILE)
    xn = x_ref[...] + gate_ref[pl.ds(row, 1), :] * moe
    inv = lax.rsqrt(jnp.mean(xn * xn, axis=-1, keepdims=True) + RMS_EPS)
    o_ref[...] = (xn * inv) * gain_ref[...]


def _combine(y_rows, pos, x_rows, route, mod, layer, gate_chunk, norm_f):
    n_steps = N_LAT // COMBINE_TILE
    pos_blocks = pos.reshape(n_steps, 1, TOP_K * COMBINE_TILE)
    idx_shape = (None, 1, TOP_K * COMBINE_TILE)
    return pl.pallas_call(
        _combine_kernel,
        grid=(n_steps,),
        in_specs=[
            pl.BlockSpec(idx_shape, lambda t: (t, 0, 0), memory_space=pltpu.SMEM),
            pl.BlockSpec(idx_shape, lambda t: (jnp.minimum(t + 1, n_steps - 1), 0, 0),
                         memory_space=pltpu.SMEM),
            pl.BlockSpec(memory_space=pl.ANY),
            pl.BlockSpec((COMBINE_TILE, D_MODEL), lambda t: (t, 0)),
            pl.BlockSpec((COMBINE_TILE, LANES), lambda t: (t, 0)),
            pl.BlockSpec((None, MOD_ROWS, D_MODEL), lambda t: (layer, 0, gate_chunk)),
            pl.BlockSpec((1, D_MODEL), lambda t: (0, 0)),
        ],
        out_specs=pl.BlockSpec((COMBINE_TILE, D_MODEL), lambda t: (t, 0)),
        out_shape=jax.ShapeDtypeStruct((N_LAT, D_MODEL), F32),
        scratch_shapes=[pltpu.VMEM((2, TOP_K, COMBINE_TILE, D_MODEL), F32),
                        pltpu.SemaphoreType.DMA((2,))],
        compiler_params=_params(1),
        name="moe_combine",
    )(pos_blocks, pos_blocks, y_rows, x_rows, route, mod, norm_f.reshape(1, D_MODEL))


def _routing_tables(route):
    experts = route[:, :TOP_K].astype(jnp.int32).reshape(-1)
    onehot = (experts[:, None] == jnp.arange(N_EXPERTS, dtype=jnp.int32)[None, :]).astype(jnp.int32)
    rank = jnp.sum((jnp.cumsum(onehot, axis=0) - onehot) * onehot, axis=1)
    counts = jnp.sum(onehot, axis=0)
    padded = ((counts + MOE_TILE - 1) // MOE_TILE) * MOE_TILE
    ends = jnp.cumsum(padded)
    starts = ends - counts
    pos = (jnp.sum(onehot * starts[None, :], axis=1) + rank).astype(jnp.int32)
    tok_of_row = jnp.zeros((MOE_ROWS,), jnp.int32).at[pos].set(
        jnp.arange(TOP_K * N_LAT, dtype=jnp.int32) // TOP_K)
    n_tiles = MOE_ROWS // MOE_TILE
    n_used = (ends[-1] // MOE_TILE).astype(jnp.int32)
    tile_start = jnp.arange(n_tiles, dtype=jnp.int32) * MOE_TILE
    tile_expert = jnp.sum((tile_start[:, None] >= ends[None, :]).astype(jnp.int32), axis=1)
    last_expert = jnp.sum((tile_start[jnp.maximum(n_used - 1, 0)] >= ends).astype(jnp.int32))
    in_use = jnp.arange(n_tiles) < n_used
    tile_expert = jnp.where(in_use, tile_expert, last_expert)
    tile_expert = jnp.minimum(tile_expert, N_EXPERTS - 1).astype(jnp.int32)
    tile_first = jnp.clip(starts[tile_expert] - tile_start, 0, MOE_TILE)
    tile_first = jnp.where(in_use, tile_first, MOE_TILE).astype(jnp.int32)
    return pos, tok_of_row, tile_expert, tile_first, n_used.reshape(1)


def kernel(x, c, ctx, c_ctx, w_mod, b_mod, norm1, w_in, sink, conv_w, w_o_attn, w_o_conv, w_out,
           norm2, ffn_w1, ffn_w3, ffn_w2, router, moe_w1, moe_w3, moe_w2, norm_f):
    tm = ROW_TILE
    cvec = jnp.concatenate(
        [c, c_ctx[None, :], jnp.zeros((MOD_ROWS - BATCH - 1, D_MODEL), F32)], axis=0)
    mod = _modulation(cvec, w_mod, b_mod)
    x_main, x_tail = x.reshape(N_LAT, D_MODEL), ctx.reshape(N_CTX, D_MODEL)
    cos_t, sa_t, sb_t = _rope_tables(tm)
    conv_w8 = jnp.pad(conv_w, ((0, 0), (0, 8 - conv_w.shape[1]), (0, 0)))
    sink_tbl = jnp.broadcast_to(sink[:, :, None], (DEPTH, N_HEADS, LANES))
    masks = _band_masks()
    seq_tiles = SEQ // tm
    lat_tiles = N_LAT // tm
    out = None

    for layer in range(DEPTH):
        ctx_out = layer < DEPTH - 1
        n_rows = N_ALL if ctx_out else N_LAT
        m_tiles = n_rows // tm
        grp = _layer_group(layer, m_tiles)
        chunk = D_MODEL // 512

        h = _norm_mod(x_main, norm1, mod, layer, 0, 1, N_ALL, BF16, x_tail=x_tail)
        tn_qkv = 512

        def variant(n):
            return jnp.where(n < Q_W // tn_qkv, 0, jnp.where(n < (Q_W + KV_W) // tn_qkv, 1, 2))

        def tbl_map(n, m):
            return (variant(n), jnp.where(m < lat_tiles, m % seq_tiles, seq_tiles), 0)

        tbl_shape = (None, tm, HEAD_DIM)
        qkv = _ws_matmul(
            [h], [w_in], pairs=[(0, 0)], epilogue=_ep_rope,
            extras=[(cos_t, tbl_shape, tbl_map), (sa_t, tbl_shape, tbl_map),
                    (sb_t, tbl_shape, tbl_map)],
            group=grp, m_tiles=m_tiles, tm=tm, tn=tn_qkv, n_out=QKV_W, n_sub=2,
            out_dtype=BF16, name="in_proj_qkv")
        rest = _ws_matmul(
            [h], [w_in], pairs=[(0, 0)], epilogue=_ep_plain,
            group=grp, m_tiles=m_tiles, tm=tm, tn=1024, n_out=REST_W,
            col_off=QKV_W // 1024, out_dtype=BF16, name="in_proj_rest")
        if ctx_out:
            kv_ctx, ctx_row_blk = qkv, N_LAT // CTX_LEN
            ctx_k_col, ctx_v_col = Q_W // HEAD_DIM, (Q_W + KV_W) // HEAD_DIM
        else:
            kv_ctx = _ws_matmul(
                [h], [w_in], pairs=[(0, 0)], epilogue=_ep_plain,
                group=_layer_group(layer, N_CTX // tm), m_tiles=N_CTX // tm, tm=tm, tn=512,
                n_out=2 * KV_W, col_off=Q_W // 512, lhs_row_off=lat_tiles, out_dtype=BF16,
                name="in_proj_ctx_kv")
            ctx_row_blk, ctx_k_col, ctx_v_col = 0, 0, KV_W // HEAD_DIM
        y_attn, y_attn_ctx = _attention(
            qkv, kv_ctx, sink_tbl[layer], masks, with_ctx_queries=ctx_out,
            ctx_row_blk=ctx_row_blk, ctx_k_col=ctx_k_col, ctx_v_col=ctx_v_col)
        y_conv = _short_conv(rest, conv_w8, layer, n_rows)
        tn = 512
        gate_blk = D_MODEL // tn
        merged = _ws_matmul(
            [y_attn, y_conv], [w_o_attn, w_o_conv], pairs=[(0, 0), (1, 1)], epilogue=_ep_merge,
            extras=[(rest, (tm, tn), lambda n, m: (m, 3 * gate_blk + n)),
                    (rest, (tm, tn), lambda n, m: (m, 4 * gate_blk + n))],
            group=grp, m_tiles=m_tiles, tm=tm, tn=tn, n_out=D_MODEL,
            lhs0_tail=y_attn_ctx, out_dtype=BF16, name="merge")
        gate_spec = (mod, (None, MOD_ROWS, tn), lambda n, m: (layer, 0, 2 * chunk + n))
        if x_tail is None:
            res_extras = [(x_main, (tm, tn), lambda n, m: (m, n)), gate_spec]
            res_tail = None
        else:
            res_tail = x_main.shape[0] // tm
            res_extras = [
                (x_main, (tm, tn), lambda n, m: (jnp.minimum(m, res_tail - 1), n)),
                (x_tail, (tm, tn), lambda n, m: (jnp.maximum(m - res_tail, 0), n)),
                gate_spec]
        x_mid = _ws_matmul(
            [merged], [w_out], pairs=[(0, 0)],
            epilogue=functools.partial(_ep_residual, tm=tm, tail_tiles=res_tail),
            extras=res_extras, group=grp, m_tiles=m_tiles, tm=tm, tn=tn, n_out=D_MODEL,
            out_dtype=F32, name="out_proj")

        if layer % 2 == 0:
            i = layer // 2
            h2 = _norm_mod(x_mid, norm2, mod, layer, 3, 4, n_rows, BF16)
            hidden = _ws_matmul(
                [h2], [ffn_w1, ffn_w3], pairs=[(0, 0), (0, 1)], epilogue=_ep_swiglu,
                group=_layer_group(i, m_tiles), m_tiles=m_tiles, tm=tm, tn=512, n_out=D_FF,
                out_dtype=BF16, name="ffn_up")
            tm2, tn2 = 512, 512
            x_main, x_tail = _ws_matmul(
                [hidden], [ffn_w2], pairs=[(0, 0)],
                epilogue=functools.partial(_ep_residual, tm=tm2),
                extras=[(x_mid, (tm2, tn2), lambda n, m: (m, n)),
                        (mod, (None, MOD_ROWS, tn2),
                         lambda n, m: (layer, 0, 5 * (D_MODEL // tn2) + n))],
                group=_layer_group(i, n_rows // tm2), m_tiles=n_rows // tm2, tm=tm2, tn=tn2,
                n_out=D_MODEL, out_dtype=F32, name="ffn_down"), None
        else:
            i = layer // 2
            w_router = jnp.pad(router[i], ((0, 0), (0, LANES - N_EXPERTS)))
            h2_chunks, route = _norm_route(x_mid, norm2, mod, layer, 3, 4, w_router, n_rows)
            pos, tok_of_row, tile_expert, tile_first, n_used = _routing_tables(route)
            rows = _dispatch(h2_chunks, tok_of_row, n_used)
            moe_tiles = MOE_ROWS // MOE_TILE
            n_stack = moe_w1.shape[0] * N_EXPERTS
            w1s = moe_w1.reshape(n_stack, D_MODEL, D_FF_EXPERT)
            w3s = moe_w3.reshape(n_stack, D_MODEL, D_FF_EXPERT)
            w2s = moe_w2.reshape(n_stack, D_FF_EXPERT, D_MODEL)
            tile_group = tile_expert + i * N_EXPERTS
            hidden = _ws_matmul(
                [rows], [w1s, w3s], pairs=[(0, 0), (0, 1)], epilogue=_ep_swiglu,
                group=tile_group, tile_first=tile_first, m_tiles=moe_tiles,
                tm=MOE_TILE, tn=1024, n_out=D_FF_EXPERT, out_dtype=BF16, name="moe_up")
            y_rows = _ws_matmul(
                [hidden], [w2s], pairs=[(0, 0)], epilogue=_ep_plain,
                group=tile_group, tile_first=tile_first, m_tiles=moe_tiles,
                tm=MOE_TILE, tn=512, n_out=D_MODEL, out_dtype=F32, name="moe_down")
            out = _combine(y_rows, pos, x_mid, route, mod, layer, 5, norm_f)

    return out.reshape(BATCH, SEQ, D_MODEL)
```

```python
import functools

import numpy as np
import jax
import jax.numpy as jnp
from jax import lax
from jax.experimental import pallas as pl
from jax.experimental.pallas import tpu as pltpu

D_MODEL = 2048
BATCH = 4
SEQ = 2048
DEPTH = 2
GRID_W = 64
CTX_LEN = 256
N_HEADS = 16
N_KV_HEADS = 4
GROUP = N_HEADS // N_KV_HEADS
HEAD_DIM = 128
WINDOW = 128
BLOCK = 128
ROPE_THETA = 10000.0
D_FF = 5632
N_EXPERTS = 8
TOP_K = 2
D_FF_EXPERT = 7168
RMS_EPS = 1e-6
NEG_INF = -1e30
LOG2_E = 1.4426950408889634
N_MOD = 6
Q_W = N_HEADS * HEAD_DIM
KV_W = N_KV_HEADS * HEAD_DIM
QKV_W = Q_W + 2 * KV_W
REST_W = 3 * D_MODEL + 2 * D_MODEL

N_LAT = BATCH * SEQ
N_CTX = BATCH * CTX_LEN
N_ALL = N_LAT + N_CTX
CTX_MOD_ROW = BATCH
MOD_ROWS = 8

LANES = 128
VMEM_LIMIT = 60 * 1024 * 1024
ROW_TILE = 1024
MOE_TILE = 512
MOE_ROWS = TOP_K * N_LAT + N_EXPERTS * MOE_TILE
PART_ROWS = 128

F32 = jnp.float32
BF16 = jnp.bfloat16


def _params(n_axes):
    return pltpu.CompilerParams(
        dimension_semantics=("arbitrary",) * n_axes, vmem_limit_bytes=VMEM_LIMIT)


def _sigmoid(v):
    return 1.0 / (1.0 + jnp.exp(-v))


def _mod_row(m_tile_start_row):
    return jnp.minimum(m_tile_start_row // SEQ, CTX_MOD_ROW)


def _modulation_kernel(c_ref, w_ref, b_ref, o_ref):
    cv = c_ref[...]
    act = (cv * _sigmoid(cv)).astype(BF16)
    o_ref[...] = jnp.dot(act, w_ref[...].astype(BF16), preferred_element_type=F32) + b_ref[...]


def _modulation(cvec, w_mod, b_mod):
    tn = 1024
    n_tiles = N_MOD * D_MODEL // tn
    return pl.pallas_call(
        _modulation_kernel,
        grid=(DEPTH, n_tiles),
        in_specs=[
            pl.BlockSpec((MOD_ROWS, D_MODEL), lambda l, n: (0, 0)),
            pl.BlockSpec((None, D_MODEL, tn), lambda l, n: (l, 0, n)),
            pl.BlockSpec((None, 1, tn), lambda l, n: (l, 0, n)),
        ],
        out_specs=pl.BlockSpec((None, MOD_ROWS, tn), lambda l, n: (l, 0, n)),
        out_shape=jax.ShapeDtypeStruct((DEPTH, MOD_ROWS, N_MOD * D_MODEL), F32),
        compiler_params=_params(2),
        name="modulation",
    )(cvec, w_mod, b_mod.reshape(DEPTH, 1, N_MOD * D_MODEL))


def _norm_kernel(x_ref, *refs, tm, tail_tiles):
    m = pl.program_id(0)
    row = _mod_row(m * tm)
    if tail_tiles is None:
        g_ref, sh_ref, sc_ref, o_ref = refs
        xf = x_ref[...]
    else:
        x_tail_ref, g_ref, sh_ref, sc_ref, o_ref = refs
        xf = jnp.where(m < tail_tiles, x_ref[...], x_tail_ref[...])
    inv = lax.rsqrt(jnp.mean(xf * xf, axis=-1, keepdims=True) + RMS_EPS)
    y = (xf * inv) * g_ref[...]
    y = y * (1.0 + sc_ref[pl.ds(row, 1), :]) + sh_ref[pl.ds(row, 1), :]
    o_ref[...] = y.astype(o_ref.dtype)


def _norm_mod(x_rows, gain, mod, layer, shift_chunk, scale_chunk, n_rows, out_dtype, x_tail=None):
    tm = 512
    xs = [x_rows]
    x_specs = [pl.BlockSpec((tm, D_MODEL), lambda m: (m, 0))]
    tail_tiles = None
    if x_tail is not None:
        tail_tiles = x_rows.shape[0] // tm
        xs.append(x_tail)
        x_specs = [pl.BlockSpec((tm, D_MODEL), lambda m: (jnp.minimum(m, tail_tiles - 1), 0)),
                   pl.BlockSpec((tm, D_MODEL), lambda m: (jnp.maximum(m - tail_tiles, 0), 0))]
    return pl.pallas_call(
        functools.partial(_norm_kernel, tm=tm, tail_tiles=tail_tiles),
        grid=(n_rows // tm,),
        in_specs=x_specs + [
            pl.BlockSpec((None, 1, D_MODEL), lambda m: (layer, 0, 0)),
            pl.BlockSpec((None, MOD_ROWS, D_MODEL), lambda m: (layer, 0, shift_chunk)),
            pl.BlockSpec((None, MOD_ROWS, D_MODEL), lambda m: (layer, 0, scale_chunk)),
        ],
        out_specs=pl.BlockSpec((tm, D_MODEL), lambda m: (m, 0)),
        out_shape=jax.ShapeDtypeStruct((n_rows, D_MODEL), out_dtype),
        compiler_params=_params(1),
        name="norm_mod",
    )(*xs, gain.reshape(DEPTH, 1, D_MODEL), mod, mod)


def _ws_kernel(group_ref, first_ref, runs_ref, *refs, n_lhs, n_w, n_extra, pairs, epilogue,
               tail_tiles, tm, tn, col_blk, n_sub, partial):
    lhs_refs = refs[:n_lhs]
    w_refs = refs[n_lhs:n_lhs + n_w]
    ex_refs = refs[n_lhs + n_w:n_lhs + n_w + n_extra]
    out_ref, wbuf, sem, slot_ref = refs[n_lhs + n_w + n_extra:]
    n = pl.program_id(0)
    m = pl.program_id(1)

    def weight_copies(col, grp, slot):
        cols = pl.ds(pl.multiple_of((col + col_blk) * tn, tn), tn)
        return [pltpu.make_async_copy(w_ref.at[grp, :, cols], wbuf.at[slot, wi], sem.at[slot, wi])
                for wi, w_ref in enumerate(w_refs)]

    @pl.when((n == 0) & (m == 0))
    def _():
        slot_ref[0] = 0
        for copy in weight_copies(0, group_ref[0], 0):
            copy.start()

    cur = slot_ref[0]

    @pl.when(runs_ref[0, m] == 1)
    def _():
        for copy in weight_copies(n, group_ref[m], cur):
            copy.wait()
        next_col = n + runs_ref[3, m]

        @pl.when(next_col < pl.num_programs(0))
        def _():
            for copy in weight_copies(next_col, runs_ref[2, m], 1 - cur):
                copy.start()

    def lhs_tile(li, rs):
        if li == 0 and tail_tiles is not None:
            return jnp.where(m < tail_tiles, lhs_refs[0][rs, :], lhs_refs[n_lhs - 1][rs, :])
        return lhs_refs[li][rs, :]

    def compute(row_slices):
        w_bf = [wbuf[cur, wi].astype(BF16) for wi in range(n_w)]
        for rs in row_slices:
            dots = [jnp.dot(lhs_tile(li, rs), w_bf[wi], preferred_element_type=F32)
                    for li, wi in pairs]
            out_ref[rs, :] = epilogue(dots, ex_refs, n, m, rs).astype(out_ref.dtype)

    def slices(size):
        return [slice(r, r + size) for r in range(0, tm, size)]

    if not partial:
        compute(slices(tm // n_sub))
    else:
        first = first_ref[m]

        @pl.when(first == 0)
        def _():
            compute(slices(tm // n_sub))

        for rs in slices(PART_ROWS):
            @pl.when((first > 0) & (rs.stop > first))
            def _():
                compute([rs])

            @pl.when((first > 0) & (rs.stop <= first))
            def _():
                out_ref[rs, :] = jnp.zeros((PART_ROWS, out_ref.shape[1]), out_ref.dtype)

    @pl.when(runs_ref[1, m] == 1)
    def _():
        slot_ref[0] = 1 - cur


def _ws_matmul(lhs, weights, *, pairs, epilogue, extras=(), group, tile_first=None,
               m_tiles, tm, tn, n_out, col_off=0, lhs_row_off=0, lhs0_tail=None, n_sub=1,
               out_dtype, name):
    k_dim = lhs[0].shape[1]
    col_blk = col_off
    row_blk = lhs_row_off
    partial = tile_first is not None
    if tile_first is None:
        tile_first = jnp.zeros((m_tiles,), jnp.int32)
    tile_idx = jnp.arange(m_tiles, dtype=jnp.int32)
    has_data = tile_first < tm
    last_used = jnp.max(jnp.where(has_data, tile_idx, 0))
    lhs_idx = jnp.where(has_data, tile_idx, last_used).astype(jnp.int32)
    runs = jnp.concatenate([_run_table(group), lhs_idx[None, :]], axis=0)

    def wrap(fn):
        return lambda n, m, g, u, r: fn(n, m)

    in_specs = [pl.BlockSpec((tm, k_dim), lambda n, m, g, u, r: (r[4, m] + row_blk, 0))
                for _ in lhs]
    tail_tiles = None
    if lhs0_tail is not None:
        tail_tiles = lhs[0].shape[0] // tm
        in_specs[0] = pl.BlockSpec(
            (tm, k_dim), lambda n, m, g, u, r: (jnp.minimum(m, tail_tiles - 1), 0))
        in_specs.append(pl.BlockSpec(
            (tm, k_dim), lambda n, m, g, u, r: (jnp.maximum(m - tail_tiles, 0), 0)))
        lhs = list(lhs) + [lhs0_tail]
    in_specs += [pl.BlockSpec(memory_space=pl.ANY) for _ in weights]
    scratch = [pltpu.VMEM((2, len(weights), k_dim, tn), F32),
               pltpu.SemaphoreType.DMA((2, len(weights))),
               pltpu.SMEM((1,), jnp.int32)]
    in_specs += [pl.BlockSpec(shape, wrap(fn)) for _, shape, fn in extras]
    grid_spec = pltpu.PrefetchScalarGridSpec(
        num_scalar_prefetch=3,
        grid=(n_out // tn, m_tiles),
        in_specs=in_specs,
        out_specs=pl.BlockSpec((tm, tn), lambda n, m, g, u, r: (m, n)),
        scratch_shapes=scratch,
    )
    kern = functools.partial(
        _ws_kernel, n_lhs=len(lhs), n_w=len(weights), n_extra=len(extras), pairs=pairs,
        epilogue=epilogue, tail_tiles=tail_tiles, tm=tm, tn=tn, col_blk=col_blk, n_sub=n_sub,
        partial=partial)
    return pl.pallas_call(
        kern,
        grid_spec=grid_spec,
        out_shape=jax.ShapeDtypeStruct((m_tiles * tm, n_out), out_dtype),
        compiler_params=_params(2),
        name=name,
    )(group, tile_first, runs, *lhs, *weights, *[e[0] for e in extras])


def _run_table(group):
    m_tiles = group.shape[0]
    idx = jnp.arange(m_tiles, dtype=jnp.int32)
    start = group != jnp.concatenate([group[:1] - 1, group[:-1]])
    end = group != jnp.concatenate([group[1:], group[-1:] - 1])
    start_pos = jnp.where(start, idx, m_tiles)
    after = jnp.concatenate([start_pos[1:], jnp.full((1,), m_tiles, jnp.int32)])
    next_start = lax.cummin(after[::-1])[::-1]
    wraps = next_start >= m_tiles
    next_group = group[jnp.where(wraps, 0, next_start)]
    return jnp.stack([start, end, next_group, wraps]).astype(jnp.int32)


def _layer_group(layer, m_tiles):
    return jnp.full((m_tiles,), layer, jnp.int32)


def _ep_plain(dots, ex, n, m, rs):
    return dots[0]


def _ep_rope(dots, ex, n, m, rs):
    cos_ref, sa_ref, sb_ref = ex
    d = dots[0]
    cos, sa, sb = cos_ref[rs, :], sa_ref[rs, :], sb_ref[rs, :]
    quarter = HEAD_DIM // 4
    heads = []
    for h in range(d.shape[1] // HEAD_DIM):
        t = d[:, h * HEAD_DIM:(h + 1) * HEAD_DIM]
        heads.append(t * cos + pltpu.roll(t, HEAD_DIM - quarter, 1) * sa
                     + pltpu.roll(t, quarter, 1) * sb)
    return jnp.concatenate(heads, axis=1)


def _ep_merge(dots, ex, n, m, rs):
    ga_ref, gc_ref = ex
    return (_sigmoid(ga_ref[rs, :].astype(F32)) * dots[0]
            + _sigmoid(gc_ref[rs, :].astype(F32)) * dots[1])


def _ep_residual(dots, ex, n, m, rs, *, tm, tail_tiles=None):
    if tail_tiles is None:
        x_ref, gate_ref = ex
        x = x_ref[rs, :]
    else:
        x_ref, x_tail_ref, gate_ref = ex
        x = jnp.where(m < tail_tiles, x_ref[rs, :], x_tail_ref[rs, :])
    row = _mod_row(m * tm)
    return x + gate_ref[pl.ds(row, 1), :] * dots[0]


def _ep_swiglu(dots, ex, n, m, rs):
    a = dots[0]
    return (a * _sigmoid(a)) * dots[1]


def _rope_tables(tm):
    quarter = HEAD_DIM // 4
    inv_freq = np.power(np.float32(ROPE_THETA),
                        -np.arange(quarter, dtype=np.float32) / np.float32(quarter)).astype(np.float32)
    pos = np.arange(SEQ)
    rows = (pos // GRID_W).astype(np.float32)
    cols = (pos % GRID_W).astype(np.float32)
    ang = np.concatenate([rows[:, None] * inv_freq[None, :]] * 2
                         + [cols[:, None] * inv_freq[None, :]] * 2, axis=1).astype(np.float32)
    cos = np.cos(ang).astype(np.float32)
    sin = np.sin(ang).astype(np.float32)
    lane = np.arange(HEAD_DIM)
    first = ((lane // quarter) % 2 == 0)[None, :]
    sa = np.where(first, -sin, 0.0).astype(np.float32)
    sb = np.where(first, 0.0, sin).astype(np.float32)
    ones = np.ones((tm, HEAD_DIM), np.float32)
    zeros = np.zeros((tm, HEAD_DIM), np.float32)
    scale = np.float32(HEAD_DIM ** -0.5 * LOG2_E)

    def variants(tbl, ident):
        k = np.concatenate([tbl, ident], axis=0)
        ident_all = np.concatenate([np.broadcast_to(ident[:1], tbl.shape), ident], axis=0)
        return np.stack([k * scale, k, ident_all])

    return (jnp.asarray(variants(cos, ones)), jnp.asarray(variants(sa, zeros)),
            jnp.asarray(variants(sb, zeros)))


ATTN_ROWS = 2 * BLOCK
BAND = 3 * BLOCK


def _band_masks():
    r = np.arange(BLOCK)[:, None]
    c = np.arange(BAND)[None, :]
    masks = [(np.abs(off * BLOCK + r - c) <= WINDOW).astype(np.float32) for off in range(3)]
    return jnp.asarray(np.stack([np.tile(m, (GROUP, 1)) for m in masks]))


def _stack_heads(q):
    return jnp.concatenate([q[:, h * HEAD_DIM:(h + 1) * HEAD_DIM] for h in range(GROUP)], axis=0)


def _unstack_heads(o):
    return jnp.concatenate([o[h * BLOCK:(h + 1) * BLOCK, :] for h in range(GROUP)], axis=1)


def _sink_column(sink_ref, g):
    return LOG2_E * jnp.concatenate(
        [jnp.broadcast_to(sink_ref[pl.ds(g * GROUP + h, 1), 0:1], (BLOCK, 1)) for h in range(GROUP)],
        axis=0)


def _softmax_pv(s, sink, vals):
    mx = jnp.maximum(jnp.max(s, axis=-1, keepdims=True), sink)
    p = jnp.exp2(s - mx)
    denom = jnp.sum(p, axis=-1, keepdims=True) + jnp.exp2(sink - mx)
    return jnp.dot(p.astype(BF16), vals, preferred_element_type=F32) / denom


_CONTRACT_LAST = (((1,), (1,)), ((), ()))


def _attn_latent_kernel(sink_ref, mask_ref, q_ref, k_ref, v_ref, kx_ref, vx_ref, o_ref, s_even, s_odd):
    n_blocks = SEQ // BLOCK
    sink = _sink_column(sink_ref, pl.program_id(1))

    def band_start(i):
        first = jnp.clip(i - 1, 0, n_blocks - 3)
        return first, pl.multiple_of(first * BLOCK, BLOCK)

    def scores(i, s_ref):
        first, start = band_start(i)
        qs = _stack_heads(q_ref[pl.ds(pl.multiple_of(i * BLOCK, BLOCK), BLOCK), :])
        s_loc = lax.dot_general(qs, k_ref[pl.ds(start, BAND), :], _CONTRACT_LAST,
                                preferred_element_type=F32)
        s_ref[:, 0:BAND] = jnp.where(mask_ref[i - first] > 0.5, s_loc, NEG_INF)
        s_ref[:, BAND:] = lax.dot_general(qs, kx_ref[...], _CONTRACT_LAST,
                                          preferred_element_type=F32)

    def finish(i, s_ref):
        _, start = band_start(i)
        vals = jnp.concatenate([v_ref[pl.ds(start, BAND), :], vx_ref[...]], axis=0)
        o = _softmax_pv(s_ref[...], sink, vals)
        o_ref[pl.ds(pl.multiple_of(i * BLOCK, BLOCK), BLOCK), :] = _unstack_heads(o).astype(o_ref.dtype)

    scores(0, s_even)

    def pair(t, carry):
        i = 2 * t
        scores(i + 1, s_odd)
        finish(i, s_even)
        scores(jnp.minimum(i + 2, n_blocks - 1), s_even)
        finish(i + 1, s_odd)
        return carry

    lax.fori_loop(0, n_blocks // 2, pair, 0)


def _attn_ctx_kernel(sink_ref, q_ref, kx_ref, vx_ref, o_ref):
    width = GROUP * HEAD_DIM
    for g in range(N_KV_HEADS):
        sink = _sink_column(sink_ref, g)
        kx = kx_ref[:, g * HEAD_DIM:(g + 1) * HEAD_DIM]
        vx = vx_ref[:, g * HEAD_DIM:(g + 1) * HEAD_DIM]
        for sub in range(CTX_LEN // BLOCK):
            rows = slice(sub * BLOCK, (sub + 1) * BLOCK)
            cols = slice(g * width, (g + 1) * width)
            qs = _stack_heads(q_ref[rows, cols])
            s = lax.dot_general(qs, kx, _CONTRACT_LAST, preferred_element_type=F32)
            o_ref[rows, cols] = _unstack_heads(_softmax_pv(s, sink, vx)).astype(o_ref.dtype)


def _attention(qkv, kv_ctx, sink_tbl, masks, *, with_ctx_queries, ctx_row_blk, ctx_k_col, ctx_v_col):
    k_col = Q_W // HEAD_DIM
    v_col = (Q_W + KV_W) // HEAD_DIM
    width = GROUP * HEAD_DIM
    sink_spec = pl.BlockSpec((N_HEADS, LANES), lambda b, g: (0, 0))
    kx_spec = pl.BlockSpec((CTX_LEN, HEAD_DIM), lambda b, g: (ctx_row_blk + b, ctx_k_col + g))
    vx_spec = pl.BlockSpec((CTX_LEN, HEAD_DIM), lambda b, g: (ctx_row_blk + b, ctx_v_col + g))
    y = pl.pallas_call(
        _attn_latent_kernel,
        grid=(BATCH, N_KV_HEADS),
        in_specs=[
            sink_spec,
            pl.BlockSpec((3, GROUP * BLOCK, BAND), lambda b, g: (0, 0, 0)),
            pl.BlockSpec((SEQ, width), lambda b, g: (b, g)),
            pl.BlockSpec((SEQ, HEAD_DIM), lambda b, g: (b, k_col + g)),
            pl.BlockSpec((SEQ, HEAD_DIM), lambda b, g: (b, v_col + g)),
            kx_spec, vx_spec,
        ],
        out_specs=pl.BlockSpec((SEQ, width), lambda b, g: (b, g)),
        out_shape=jax.ShapeDtypeStruct((N_LAT, Q_W), BF16),
        scratch_shapes=[pltpu.VMEM((GROUP * BLOCK, BAND + CTX_LEN), F32),
                        pltpu.VMEM((GROUP * BLOCK, BAND + CTX_LEN), F32)],
        compiler_params=_params(2),
        name="attention",
    )(sink_tbl, masks, qkv, qkv, qkv, kv_ctx, kv_ctx)
    if not with_ctx_queries:
        return y, None
    ctx_blk = N_LAT // CTX_LEN
    kv_blk = KV_W // HEAD_DIM
    y_ctx = pl.pallas_call(
        _attn_ctx_kernel,
        grid=(BATCH,),
        in_specs=[
            pl.BlockSpec((N_HEADS, LANES), lambda b: (0, 0)),
            pl.BlockSpec((CTX_LEN, Q_W), lambda b: (ctx_blk + b, 0)),
            pl.BlockSpec((CTX_LEN, KV_W), lambda b: (ctx_row_blk + b, ctx_k_col // kv_blk)),
            pl.BlockSpec((CTX_LEN, KV_W), lambda b: (ctx_row_blk + b, ctx_v_col // kv_blk)),
        ],
        out_specs=pl.BlockSpec((CTX_LEN, Q_W), lambda b: (b, 0)),
        out_shape=jax.ShapeDtypeStruct((N_CTX, Q_W), BF16),
        compiler_params=_params(1),
        name="attention_ctx",
    )(sink_tbl, qkv, kv_ctx, kv_ctx)
    return y, y_ctx


CONV_ROWS = CTX_LEN
CONV_ROWS_LATENT = 512
HALO = 16


def _conv_kernel(w_ref, u_ref, gb_ref, gc_ref, up_ref, gcp_ref, un_ref, gcn_ref, o_ref, *, rows):
    r = pl.program_id(0)
    lat_blocks = N_LAT // rows
    per_seq = SEQ // rows
    is_ctx = r >= lat_blocks
    seq_start = is_ctx | (r % per_seq == 0)
    seq_end = is_ctx | (r % per_seq == per_seq - 1)
    z = gc_ref[...].astype(F32) * u_ref[...].astype(F32)
    z_before = (gcp_ref[HALO - 1:HALO, :].astype(F32) * up_ref[HALO - 1:HALO, :].astype(F32))
    z_after = gcn_ref[0:1, :].astype(F32) * un_ref[0:1, :].astype(F32)
    z_before = jnp.where(seq_start, 0.0, z_before)
    z_after = jnp.where(seq_end, 0.0, z_after)
    row = lax.broadcasted_iota(jnp.int32, z.shape, 0)
    z_prev = jnp.where(row == 0, z_before, pltpu.roll(z, 1, 0))
    z_next = jnp.where(row == rows - 1, z_after, pltpu.roll(z, rows - 1, 0))
    y = z_prev * w_ref[0:1, :] + z * w_ref[1:2, :] + z_next * w_ref[2:3, :]
    o_ref[...] = (gb_ref[...].astype(F32) * y).astype(o_ref.dtype)


def _short_conv(rest, conv_w, layer, n_rows):
    rows = CONV_ROWS_LATENT if n_rows == N_LAT else CONV_ROWS
    n_blocks = n_rows // rows
    tc = D_MODEL
    halo_per_block = rows // HALO
    last_halo = n_rows // HALO - 1

    def main(chunk):
        return pl.BlockSpec((rows, tc), lambda r: (r, chunk))

    def before(chunk):
        return pl.BlockSpec((HALO, tc), lambda r: (jnp.maximum(r * halo_per_block - 1, 0), chunk))

    def after(chunk):
        return pl.BlockSpec((HALO, tc),
                            lambda r: (jnp.minimum((r + 1) * halo_per_block, last_halo), chunk))

    return pl.pallas_call(
        functools.partial(_conv_kernel, rows=rows),
        grid=(n_blocks,),
        in_specs=[
            pl.BlockSpec((None, 8, tc), lambda r: (layer, 0, 0)),
            main(0), main(1), main(2), before(0), before(2), after(0), after(2),
        ],
        out_specs=pl.BlockSpec((rows, tc), lambda r: (r, 0)),
        out_shape=jax.ShapeDtypeStruct((n_rows, D_MODEL), BF16),
        compiler_params=_params(1),
        name="short_conv",
    )(conv_w, rest, rest, rest, rest, rest, rest, rest)


ROW_CHUNKS = D_MODEL // LANES


def _norm_route_kernel(x_ref, g_ref, sh_ref, sc_ref, w_ref, chunks_ref, route_ref, *, tm):
    row = _mod_row(pl.program_id(0) * tm)
    xf = x_ref[...]
    inv = lax.rsqrt(jnp.mean(xf * xf, axis=-1, keepdims=True) + RMS_EPS)
    h = (xf * inv) * g_ref[...]
    h = h * (1.0 + sc_ref[pl.ds(row, 1), :]) + sh_ref[pl.ds(row, 1), :]
    for j in range(ROW_CHUNKS):
        chunks_ref[pl.ds(j, tm, stride=ROW_CHUNKS), :] = h[:, j * LANES:(j + 1) * LANES]
    route_ref[...] = _route(h, w_ref[...])


def _norm_route(x_rows, gain, mod, layer, shift_chunk, scale_chunk, w_router_padded, n_rows):
    tm = 512
    return pl.pallas_call(
        functools.partial(_norm_route_kernel, tm=tm),
        grid=(n_rows // tm,),
        in_specs=[
            pl.BlockSpec((tm, D_MODEL), lambda m: (m, 0)),
            pl.BlockSpec((None, 1, D_MODEL), lambda m: (layer, 0, 0)),
            pl.BlockSpec((None, MOD_ROWS, D_MODEL), lambda m: (layer, 0, shift_chunk)),
            pl.BlockSpec((None, MOD_ROWS, D_MODEL), lambda m: (layer, 0, scale_chunk)),
            pl.BlockSpec((D_MODEL, LANES), lambda m: (0, 0)),
        ],
        out_specs=[pl.BlockSpec((tm * ROW_CHUNKS, LANES), lambda m: (m, 0)),
                   pl.BlockSpec((tm, LANES), lambda m: (m, 0))],
        out_shape=[jax.ShapeDtypeStruct((n_rows * ROW_CHUNKS, LANES), F32),
                   jax.ShapeDtypeStruct((n_rows, LANES), F32)],
        compiler_params=_params(1),
        name="norm_route",
    )(x_rows, gain.reshape(DEPTH, 1, D_MODEL), mod, mod, w_router_padded)


def _route(h, w):
    h_hi = h.astype(BF16)
    h_lo = (h - h_hi.astype(F32)).astype(BF16)
    w_hi = w.astype(BF16)
    w_lo = (w - w_hi.astype(F32)).astype(BF16)
    logits = (jnp.dot(h_hi, w_hi, preferred_element_type=F32)
              + jnp.dot(h_hi, w_lo, preferred_element_type=F32)
              + jnp.dot(h_lo, w_hi, preferred_element_type=F32))
    lane = lax.broadcasted_iota(jnp.int32, logits.shape, 1)
    lowest = float(jnp.finfo(F32).min)
    logits = jnp.where(lane < N_EXPERTS, logits, lowest)
    v1 = jnp.max(logits, axis=-1, keepdims=True)
    i1 = jnp.min(jnp.where(logits == v1, lane, LANES), axis=-1, keepdims=True)
    rest = jnp.where(lane == i1, lowest, logits)
    v2 = jnp.max(rest, axis=-1, keepdims=True)
    i2 = jnp.min(jnp.where(rest == v2, lane, LANES), axis=-1, keepdims=True)
    e2 = jnp.exp(v2 - v1)
    p1 = 1.0 / (1.0 + e2)
    p2 = e2 / (1.0 + e2)
    return jnp.where(lane == 0, i1.astype(F32),
                     jnp.where(lane == 1, i2.astype(F32),
                               jnp.where(lane == 2, p1, jnp.where(lane == 3, p2, 0.0))))


DISPATCH_TILE = 512


DMA_UNROLL = 8


def _dispatch_kernel(nused_ref, tok_ref, tok_next_ref, h_hbm, o_ref, buf, sem):
    t = pl.program_id(0)
    slot = t % 2
    n_active = nused_ref[0] * (MOE_TILE // DISPATCH_TILE)
    tile_chunks = DISPATCH_TILE * ROW_CHUNKS

    def gather(idx_ref, dst_slot):
        def body(i, carry):
            for u in range(DMA_UNROLL):
                j = i * DMA_UNROLL + u
                src = pl.multiple_of(idx_ref[0, j] * ROW_CHUNKS, ROW_CHUNKS)
                pltpu.make_async_copy(
                    h_hbm.at[pl.ds(src, ROW_CHUNKS), :],
                    buf.at[dst_slot, pl.ds(j * ROW_CHUNKS, ROW_CHUNKS), :],
                    sem.at[dst_slot]).start(priority=u % 2)
            return carry
        lax.fori_loop(0, DISPATCH_TILE // DMA_UNROLL, body, 0)

    @pl.when((t == 0) & (n_active > 0))
    def _():
        gather(tok_ref, 0)

    @pl.when(t + 1 < n_active)
    def _():
        gather(tok_next_ref, 1 - slot)

    @pl.when(t < n_active)
    def _():
        pltpu.make_async_copy(h_hbm.at[pl.ds(0, tile_chunks), :], buf.at[slot], sem.at[slot]).wait()
        cols = [buf[slot, pl.ds(j, DISPATCH_TILE, stride=ROW_CHUNKS), :] for j in range(ROW_CHUNKS)]
        o_ref[...] = jnp.concatenate(cols, axis=1).astype(o_ref.dtype)

    @pl.when(t >= n_active)
    def _():
        o_ref[...] = jnp.zeros_like(o_ref)


def _dispatch(h_chunks, tok_of_row, n_used):
    n_rows = tok_of_row.shape[0]
    n_steps = n_rows // DISPATCH_TILE
    tok_blocks = tok_of_row.reshape(n_steps, 1, DISPATCH_TILE)
    grid_spec = pltpu.PrefetchScalarGridSpec(
        num_scalar_prefetch=1,
        grid=(n_steps,),
        in_specs=[
            pl.BlockSpec((None, 1, DISPATCH_TILE), lambda t, u: (t, 0, 0), memory_space=pltpu.SMEM),
            pl.BlockSpec((None, 1, DISPATCH_TILE),
                         lambda t, u: (jnp.minimum(t + 1, n_steps - 1), 0, 0),
                         memory_space=pltpu.SMEM),
            pl.BlockSpec(memory_space=pl.ANY),
        ],
        out_specs=pl.BlockSpec((DISPATCH_TILE, D_MODEL), lambda t, u: (t, 0)),
        scratch_shapes=[pltpu.VMEM((2, DISPATCH_TILE * ROW_CHUNKS, LANES), F32),
                        pltpu.SemaphoreType.DMA((2,))],
    )
    return pl.pallas_call(
        _dispatch_kernel,
        grid_spec=grid_spec,
        out_shape=jax.ShapeDtypeStruct((n_rows, D_MODEL), BF16),
        compiler_params=_params(1),
        name="moe_dispatch",
    )(n_used, tok_blocks, tok_blocks, h_chunks)


COMBINE_TILE = 512


def _combine_kernel(pos_ref, pos_next_ref, y_hbm, x_ref, route_ref, gate_ref, gain_ref, o_ref,
                    buf, sem):
    t = pl.program_id(0)
    n_steps = pl.num_programs(0)
    slot = t % 2

    def gather(idx_ref, dst_slot):
        def body(i, carry):
            for u in range(DMA_UNROLL):
                j = i * DMA_UNROLL + u
                for k in range(TOP_K):
                    pltpu.make_async_copy(
                        y_hbm.at[pl.ds(idx_ref[0, TOP_K * j + k], 1), :],
                        buf.at[dst_slot, k, pl.ds(j, 1), :],
                        sem.at[dst_slot]).start(priority=k)
            return carry
        lax.fori_loop(0, COMBINE_TILE // DMA_UNROLL, body, 0)

    @pl.when(t == 0)
    def _():
        gather(pos_ref, 0)

    @pl.when(t + 1 < n_steps)
    def _():
        gather(pos_next_ref, 1 - slot)

    for k in range(TOP_K):
        pltpu.make_async_copy(
            y_hbm.at[pl.ds(0, COMBINE_TILE), :], buf.at[slot, k], sem.at[slot]).wait()
    route = route_ref[...]
    moe = route[:, 2:3] * buf[slot, 0] + route[:, 3:4] * buf[slot, 1]
    row = _mod_row(t * COMBINE_TILE)
    xn = x_ref[...] + gate_ref[pl.ds(row, 1), :] * moe
    inv = lax.rsqrt(jnp.mean(xn * xn, axis=-1, keepdims=True) + RMS_EPS)
    o_ref[...] = (xn * inv) * gain_ref[...]


def _combine(y_rows, pos, x_rows, route, mod, layer, gate_chunk, norm_f):
    n_steps = N_LAT // COMBINE_TILE
    pos_blocks = pos.reshape(n_steps, 1, TOP_K * COMBINE_TILE)
    idx_shape = (None, 1, TOP_K * COMBINE_TILE)
    return pl.pallas_call(
        _combine_kernel,
        grid=(n_steps,),
        in_specs=[
            pl.BlockSpec(idx_shape, lambda t: (t, 0, 0), memory_space=pltpu.SMEM),
            pl.BlockSpec(idx_shape, lambda t: (jnp.minimum(t + 1, n_steps - 1), 0, 0),
                         memory_space=pltpu.SMEM),
            pl.BlockSpec(memory_space=pl.ANY),
            pl.BlockSpec((COMBINE_TILE, D_MODEL), lambda t: (t, 0)),
            pl.BlockSpec((COMBINE_TILE, LANES), lambda t: (t, 0)),
            pl.BlockSpec((None, MOD_ROWS, D_MODEL), lambda t: (layer, 0, gate_chunk)),
            pl.BlockSpec((1, D_MODEL), lambda t: (0, 0)),
        ],
        out_specs=pl.BlockSpec((COMBINE_TILE, D_MODEL), lambda t: (t, 0)),
        out_shape=jax.ShapeDtypeStruct((N_LAT, D_MODEL), F32),
        scratch_shapes=[pltpu.VMEM((2, TOP_K, COMBINE_TILE, D_MODEL), F32),
                        pltpu.SemaphoreType.DMA((2,))],
        compiler_params=_params(1),
        name="moe_combine",
    )(pos_blocks, pos_blocks, y_rows, x_rows, route, mod, norm_f.reshape(1, D_MODEL))


def _routing_tables(route):
    experts = route[:, :TOP_K].astype(jnp.int32).reshape(-1)
    onehot = (experts[:, None] == jnp.arange(N_EXPERTS, dtype=jnp.int32)[None, :]).astype(jnp.int32)
    rank = jnp.sum((jnp.cumsum(onehot, axis=0) - onehot) * onehot, axis=1)
    counts = jnp.sum(onehot, axis=0)
    padded = ((counts + MOE_TILE - 1) // MOE_TILE) * MOE_TILE
    ends = jnp.cumsum(padded)
    starts = ends - counts
    pos = (jnp.sum(onehot * starts[None, :], axis=1) + rank).astype(jnp.int32)
    tok_of_row = jnp.zeros((MOE_ROWS,), jnp.int32).at[pos].set(
        jnp.arange(TOP_K * N_LAT, dtype=jnp.int32) // TOP_K)
    n_tiles = MOE_ROWS // MOE_TILE
    n_used = (ends[-1] // MOE_TILE).astype(jnp.int32)
    tile_start = jnp.arange(n_tiles, dtype=jnp.int32) * MOE_TILE
    tile_expert = jnp.sum((tile_start[:, None] >= ends[None, :]).astype(jnp.int32), axis=1)
    last_expert = jnp.sum((tile_start[jnp.maximum(n_used - 1, 0)] >= ends).astype(jnp.int32))
    in_use = jnp.arange(n_tiles) < n_used
    tile_expert = jnp.where(in_use, tile_expert, last_expert)
    tile_expert = jnp.minimum(tile_expert, N_EXPERTS - 1).astype(jnp.int32)
    tile_first = jnp.clip(starts[tile_expert] - tile_start, 0, MOE_TILE)
    tile_first = jnp.where(in_use, tile_first, MOE_TILE).astype(jnp.int32)
    return pos, tok_of_row, tile_expert, tile_first, n_used.reshape(1)


def kernel(x, c, ctx, c_ctx, w_mod, b_mod, norm1, w_in, sink, conv_w, w_o_attn, w_o_conv, w_out,
           norm2, ffn_w1, ffn_w3, ffn_w2, router, moe_w1, moe_w3, moe_w2, norm_f):
    tm = ROW_TILE
    cvec = jnp.concatenate(
        [c, c_ctx[None, :], jnp.zeros((MOD_ROWS - BATCH - 1, D_MODEL), F32)], axis=0)
    mod = _modulation(cvec, w_mod, b_mod)
    x_main, x_tail = x.reshape(N_LAT, D_MODEL), ctx.reshape(N_CTX, D_MODEL)
    cos_t, sa_t, sb_t = _rope_tables(tm)
    conv_w8 = jnp.pad(conv_w, ((0, 0), (0, 8 - conv_w.shape[1]), (0, 0)))
    sink_tbl = jnp.broadcast_to(sink[:, :, None], (DEPTH, N_HEADS, LANES))
    masks = _band_masks()
    seq_tiles = SEQ // tm
    lat_tiles = N_LAT // tm
    out = None

    for layer in range(DEPTH):
        ctx_out = layer < DEPTH - 1
        n_rows = N_ALL if ctx_out else N_LAT
        m_tiles = n_rows // tm
        grp = _layer_group(layer, m_tiles)
        chunk = D_MODEL // 512

        h = _norm_mod(x_main, norm1, mod, layer, 0, 1, N_ALL, BF16, x_tail=x_tail)
        tn_qkv = 512

        def variant(n):
            return jnp.where(n < Q_W // tn_qkv, 0, jnp.where(n < (Q_W + KV_W) // tn_qkv, 1, 2))

        def tbl_map(n, m):
            return (variant(n), jnp.where(m < lat_tiles, m % seq_tiles, seq_tiles), 0)

        tbl_shape = (None, tm, HEAD_DIM)
        qkv = _ws_matmul(
            [h], [w_in], pairs=[(0, 0)], epilogue=_ep_rope,
            extras=[(cos_t, tbl_shape, tbl_map), (sa_t, tbl_shape, tbl_map),
                    (sb_t, tbl_shape, tbl_map)],
            group=grp, m_tiles=m_tiles, tm=tm, tn=tn_qkv, n_out=QKV_W, n_sub=2,
            out_dtype=BF16, name="in_proj_qkv")
        rest = _ws_matmul(
            [h], [w_in], pairs=[(0, 0)], epilogue=_ep_plain,
            group=grp, m_tiles=m_tiles, tm=tm, tn=1024, n_out=REST_W,
            col_off=QKV_W // 1024, out_dtype=BF16, name="in_proj_rest")
        if ctx_out:
            kv_ctx, ctx_row_blk = qkv, N_LAT // CTX_LEN
            ctx_k_col, ctx_v_col = Q_W // HEAD_DIM, (Q_W + KV_W) // HEAD_DIM
        else:
            kv_ctx = _ws_matmul(
                [h], [w_in], pairs=[(0, 0)], epilogue=_ep_plain,
                group=_layer_group(layer, N_CTX // tm), m_tiles=N_CTX // tm, tm=tm, tn=512,
                n_out=2 * KV_W, col_off=Q_W // 512, lhs_row_off=lat_tiles, out_dtype=BF16,
                name="in_proj_ctx_kv")
            ctx_row_blk, ctx_k_col, ctx_v_col = 0, 0, KV_W // HEAD_DIM
        y_attn, y_attn_ctx = _attention(
            qkv, kv_ctx, sink_tbl[layer], masks, with_ctx_queries=ctx_out,
            ctx_row_blk=ctx_row_blk, ctx_k_col=ctx_k_col, ctx_v_col=ctx_v_col)
        y_conv = _short_conv(rest, conv_w8, layer, n_rows)
        tn = 512
        gate_blk = D_MODEL // tn
        merged = _ws_matmul(
            [y_attn, y_conv], [w_o_attn, w_o_conv], pairs=[(0, 0), (1, 1)], epilogue=_ep_merge,
            extras=[(rest, (tm, tn), lambda n, m: (m, 3 * gate_blk + n)),
                    (rest, (tm, tn), lambda n, m: (m, 4 * gate_blk + n))],
            group=grp, m_tiles=m_tiles, tm=tm, tn=tn, n_out=D_MODEL,
            lhs0_tail=y_attn_ctx, out_dtype=BF16, name="merge")
        gate_spec = (mod, (None, MOD_ROWS, tn), lambda n, m: (layer, 0, 2 * chunk + n))
        if x_tail is None:
            res_extras = [(x_main, (tm, tn), lambda n, m: (m, n)), gate_spec]
            res_tail = None
        else:
            res_tail = x_main.shape[0] // tm
            res_extras = [
                (x_main, (tm, tn), lambda n, m: (jnp.minimum(m, res_tail - 1), n)),
                (x_tail, (tm, tn), lambda n, m: (jnp.maximum(m - res_tail, 0), n)),
                gate_spec]
        x_mid = _ws_matmul(
            [merged], [w_out], pairs=[(0, 0)],
            epilogue=functools.partial(_ep_residual, tm=tm, tail_tiles=res_tail),
            extras=res_extras, group=grp, m_tiles=m_tiles, tm=tm, tn=tn, n_out=D_MODEL,
            out_dtype=F32, name="out_proj")

        if layer % 2 == 0:
            i = layer // 2
            h2 = _norm_mod(x_mid, norm2, mod, layer, 3, 4, n_rows, BF16)
            hidden = _ws_matmul(
                [h2], [ffn_w1, ffn_w3], pairs=[(0, 0), (0, 1)], epilogue=_ep_swiglu,
                group=_layer_group(i, m_tiles), m_tiles=m_tiles, tm=tm, tn=512, n_out=D_FF,
                out_dtype=BF16, name="ffn_up")
            tm2, tn2 = 512, 512
            x_main, x_tail = _ws_matmul(
                [hidden], [ffn_w2], pairs=[(0, 0)],
                epilogue=functools.partial(_ep_residual, tm=tm2),
                extras=[(x_mid, (tm2, tn2), lambda n, m: (m, n)),
                        (mod, (None, MOD_ROWS, tn2),
                         lambda n, m: (layer, 0, 5 * (D_MODEL // tn2) + n))],
                group=_layer_group(i, n_rows // tm2), m_tiles=n_rows // tm2, tm=tm2, tn=tn2,
                n_out=D_MODEL, out_dtype=F32, name="ffn_down"), None
        else:
            i = layer // 2
            w_router = jnp.pad(router[i], ((0, 0), (0, LANES - N_EXPERTS)))
            h2_chunks, route = _norm_route(x_mid, norm2, mod, layer, 3, 4, w_router, n_rows)
            pos, tok_of_row, tile_expert, tile_first, n_used = _routing_tables(route)
            rows = _dispatch(h2_chunks, tok_of_row, n_used)
            moe_tiles = MOE_ROWS // MOE_TILE
            n_stack = moe_w1.shape[0] * N_EXPERTS
            w1s = moe_w1.reshape(n_stack, D_MODEL, D_FF_EXPERT)
            w3s = moe_w3.reshape(n_stack, D_MODEL, D_FF_EXPERT)
            w2s = moe_w2.reshape(n_stack, D_FF_EXPERT, D_MODEL)
            tile_group = tile_expert + i * N_EXPERTS
            hidden = _ws_matmul(
                [rows], [w1s, w3s], pairs=[(0, 0), (0, 1)], epilogue=_ep_swiglu,
                group=tile_group, tile_first=tile_first, m_tiles=moe_tiles,
                tm=MOE_TILE, tn=1024, n_out=D_FF_EXPERT, out_dtype=BF16, name="moe_up")
            y_rows = _ws_matmul(
                [hidden], [w2s], pairs=[(0, 0)], epilogue=_ep_plain,
                group=tile_group, tile_first=tile_first, m_tiles=moe_tiles,
                tm=MOE_TILE, tn=512, n_out=D_MODEL, out_dtype=F32, name="moe_down")
            out = _combine(y_rows, pos, x_mid, route, mod, layer, 5, norm_f)

    return out.reshape(BATCH, SEQ, D_MODEL)
```

```python
import functools

import numpy as np
import jax
import jax.numpy as jnp
from jax import lax
from jax.experimental import pallas as pl
from jax.experimental.pallas import tpu as pltpu

D_MODEL = 2048
BATCH = 4
SEQ = 2048
DEPTH = 2
GRID_W = 64
CTX_LEN = 256
N_HEADS = 16
N_KV_HEADS = 4
GROUP = N_HEADS // N_KV_HEADS
HEAD_DIM = 128
WINDOW = 128
BLOCK = 128
ROPE_THETA = 10000.0
D_FF = 5632
N_EXPERTS = 8
TOP_K = 2
D_FF_EXPERT = 7168
RMS_EPS = 1e-6
NEG_INF = -1e30
LOG2_E = 1.4426950408889634
N_MOD = 6
Q_W = N_HEADS * HEAD_DIM
KV_W = N_KV_HEADS * HEAD_DIM
QKV_W = Q_W + 2 * KV_W
REST_W = 3 * D_MODEL + 2 * D_MODEL

N_LAT = BATCH * SEQ
N_CTX = BATCH * CTX_LEN
N_ALL = N_LAT + N_CTX
CTX_MOD_ROW = BATCH
MOD_ROWS = 8

LANES = 128
VMEM_LIMIT = 60 * 1024 * 1024
ROW_TILE = 1024
MOE_TILE = 512
MOE_ROWS = TOP_K * N_LAT + N_EXPERTS * MOE_TILE
PART_ROWS = 128

F32 = jnp.float32
BF16 = jnp.bfloat16


def _params(n_axes):
    return pltpu.CompilerParams(
        dimension_semantics=("arbitrary",) * n_axes, vmem_limit_bytes=VMEM_LIMIT)


def _sigmoid(v):
    return 1.0 / (1.0 + jnp.exp(-v))


def _mod_row(m_tile_start_row):
    return jnp.minimum(m_tile_start_row // SEQ, CTX_MOD_ROW)


def _modulation_kernel(c_ref, w_ref, b_ref, o_ref):
    cv = c_ref[...]
    act = (cv * _sigmoid(cv)).astype(BF16)
    o_ref[...] = jnp.dot(act, w_ref[...].astype(BF16), preferred_element_type=F32) + b_ref[...]


def _modulation(cvec, w_mod, b_mod):
    tn = 1024
    n_tiles = N_MOD * D_MODEL // tn
    return pl.pallas_call(
        _modulation_kernel,
        grid=(DEPTH, n_tiles),
        in_specs=[
            pl.BlockSpec((MOD_ROWS, D_MODEL), lambda l, n: (0, 0)),
            pl.BlockSpec((None, D_MODEL, tn), lambda l, n: (l, 0, n)),
            pl.BlockSpec((None, 1, tn), lambda l, n: (l, 0, n)),
        ],
        out_specs=pl.BlockSpec((None, MOD_ROWS, tn), lambda l, n: (l, 0, n)),
        out_shape=jax.ShapeDtypeStruct((DEPTH, MOD_ROWS, N_MOD * D_MODEL), F32),
        compiler_params=_params(2),
        name="modulation",
    )(cvec, w_mod, b_mod.reshape(DEPTH, 1, N_MOD * D_MODEL))


def _norm_kernel(x_ref, *refs, tm, tail_tiles):
    m = pl.program_id(0)
    row = _mod_row(m * tm)
    if tail_tiles is None:
        g_ref, sh_ref, sc_ref, o_ref = refs
        xf = x_ref[...]
    else:
        x_tail_ref, g_ref, sh_ref, sc_ref, o_ref = refs
        xf = jnp.where(m < tail_tiles, x_ref[...], x_tail_ref[...])
    inv = lax.rsqrt(jnp.mean(xf * xf, axis=-1, keepdims=True) + RMS_EPS)
    y = (xf * inv) * g_ref[...]
    y = y * (1.0 + sc_ref[pl.ds(row, 1), :]) + sh_ref[pl.ds(row, 1), :]
    o_ref[...] = y.astype(o_ref.dtype)


def _norm_mod(x_rows, gain, mod, layer, shift_chunk, scale_chunk, n_rows, out_dtype, x_tail=None):
    tm = 512
    xs = [x_rows]
    x_specs = [pl.BlockSpec((tm, D_MODEL), lambda m: (m, 0))]
    tail_tiles = None
    if x_tail is not None:
        tail_tiles = x_rows.shape[0] // tm
        xs.append(x_tail)
        x_specs = [pl.BlockSpec((tm, D_MODEL), lambda m: (jnp.minimum(m, tail_tiles - 1), 0)),
                   pl.BlockSpec((tm, D_MODEL), lambda m: (jnp.maximum(m - tail_tiles, 0), 0))]
    return pl.pallas_call(
        functools.partial(_norm_kernel, tm=tm, tail_tiles=tail_tiles),
        grid=(n_rows // tm,),
        in_specs=x_specs + [
            pl.BlockSpec((None, 1, D_MODEL), lambda m: (layer, 0, 0)),
            pl.BlockSpec((None, MOD_ROWS, D_MODEL), lambda m: (layer, 0, shift_chunk)),
            pl.BlockSpec((None, MOD_ROWS, D_MODEL), lambda m: (layer, 0, scale_chunk)),
        ],
        out_specs=pl.BlockSpec((tm, D_MODEL), lambda m: (m, 0)),
        out_shape=jax.ShapeDtypeStruct((n_rows, D_MODEL), out_dtype),
        compiler_params=_params(1),
        name="norm_mod",
    )(*xs, gain.reshape(DEPTH, 1, D_MODEL), mod, mod)


def _ws_kernel(group_ref, first_ref, runs_ref, *refs, n_lhs, n_w, n_extra, pairs, epilogue,
               tail_tiles, tm, tn, col_blk, n_sub, partial):
    lhs_refs = refs[:n_lhs]
    w_refs = refs[n_lhs:n_lhs + n_w]
    ex_refs = refs[n_lhs + n_w:n_lhs + n_w + n_extra]
    out_ref, wbuf, sem, slot_ref = refs[n_lhs + n_w + n_extra:]
    n = pl.program_id(0)
    m = pl.program_id(1)

    def weight_copies(col, grp, slot):
        cols = pl.ds(pl.multiple_of((col + col_blk) * tn, tn), tn)
        return [pltpu.make_async_copy(w_ref.at[grp, :, cols], wbuf.at[slot, wi], sem.at[slot, wi])
                for wi, w_ref in enumerate(w_refs)]

    @pl.when((n == 0) & (m == 0))
    def _():
        slot_ref[0] = 0
        for copy in weight_copies(0, group_ref[0], 0):
            copy.start()

    cur = slot_ref[0]

    @pl.when(runs_ref[0, m] == 1)
    def _():
        for copy in weight_copies(n, group_ref[m], cur):
            copy.wait()
        next_col = n + runs_ref[3, m]

        @pl.when(next_col < pl.num_programs(0))
        def _():
            for copy in weight_copies(next_col, runs_ref[2, m], 1 - cur):
                copy.start()

    def lhs_tile(li, rs):
        if li == 0 and tail_tiles is not None:
            return jnp.where(m < tail_tiles, lhs_refs[0][rs, :], lhs_refs[n_lhs - 1][rs, :])
        return lhs_refs[li][rs, :]

    def compute(row_slices):
        w_bf = [wbuf[cur, wi].astype(BF16) for wi in range(n_w)]
        for rs in row_slices:
            dots = [jnp.dot(lhs_tile(li, rs), w_bf[wi], preferred_element_type=F32)
                    for li, wi in pairs]
            out_ref[rs, :] = epilogue(dots, ex_refs, n, m, rs).astype(out_ref.dtype)

    def slices(size):
        return [slice(r, r + size) for r in range(0, tm, size)]

    if not partial:
        compute(slices(tm // n_sub))
    else:
        first = first_ref[m]

        @pl.when(first == 0)
        def _():
            compute(slices(tm // n_sub))

        for rs in slices(PART_ROWS):
            @pl.when((first > 0) & (rs.stop > first))
            def _():
                compute([rs])

            @pl.when((first > 0) & (rs.stop <= first))
            def _():
                out_ref[rs, :] = jnp.zeros((PART_ROWS, out_ref.shape[1]), out_ref.dtype)

    @pl.when(runs_ref[1, m] == 1)
    def _():
        slot_ref[0] = 1 - cur


def _ws_matmul(lhs, weights, *, pairs, epilogue, extras=(), group, tile_first=None,
               m_tiles, tm, tn, n_out, col_off=0, lhs_row_off=0, lhs0_tail=None, n_sub=1,
               out_dtype, name):
    k_dim = lhs[0].shape[1]
    col_blk = col_off
    row_blk = lhs_row_off
    partial = tile_first is not None
    if tile_first is None:
        tile_first = jnp.zeros((m_tiles,), jnp.int32)
    tile_idx = jnp.arange(m_tiles, dtype=jnp.int32)
    has_data = tile_first < tm
    last_used = jnp.max(jnp.where(has_data, tile_idx, 0))
    lhs_idx = jnp.where(has_data, tile_idx, last_used).astype(jnp.int32)
    runs = jnp.concatenate([_run_table(group), lhs_idx[None, :]], axis=0)

    def wrap(fn):
        return lambda n, m, g, u, r: fn(n, m)

    in_specs = [pl.BlockSpec((tm, k_dim), lambda n, m, g, u, r: (r[4, m] + row_blk, 0))
                for _ in lhs]
    tail_tiles = None
    if lhs0_tail is not None:
        tail_tiles = lhs[0].shape[0] // tm
        in_specs[0] = pl.BlockSpec(
            (tm, k_dim), lambda n, m, g, u, r: (jnp.minimum(m, tail_tiles - 1), 0))
        in_specs.append(pl.BlockSpec(
            (tm, k_dim), lambda n, m, g, u, r: (jnp.maximum(m - tail_tiles, 0), 0)))
        lhs = list(lhs) + [lhs0_tail]
    in_specs += [pl.BlockSpec(memory_space=pl.ANY) for _ in weights]
    scratch = [pltpu.VMEM((2, len(weights), k_dim, tn), F32),
               pltpu.SemaphoreType.DMA((2, len(weights))),
               pltpu.SMEM((1,), jnp.int32)]
    in_specs += [pl.BlockSpec(shape, wrap(fn)) for _, shape, fn in extras]
    grid_spec = pltpu.PrefetchScalarGridSpec(
        num_scalar_prefetch=3,
        grid=(n_out // tn, m_tiles),
        in_specs=in_specs,
        out_specs=pl.BlockSpec((tm, tn), lambda n, m, g, u, r: (m, n)),
        scratch_shapes=scratch,
    )
    kern = functools.partial(
        _ws_kernel, n_lhs=len(lhs), n_w=len(weights), n_extra=len(extras), pairs=pairs,
        epilogue=epilogue, tail_tiles=tail_tiles, tm=tm, tn=tn, col_blk=col_blk, n_sub=n_sub,
        partial=partial)
    return pl.pallas_call(
        kern,
        grid_spec=grid_spec,
        out_shape=jax.ShapeDtypeStruct((m_tiles * tm, n_out), out_dtype),
        compiler_params=_params(2),
        name=name,
    )(group, tile_first, runs, *lhs, *weights, *[e[0] for e in extras])


def _run_table(group):
    m_tiles = group.shape[0]
    idx = jnp.arange(m_tiles, dtype=jnp.int32)
    start = group != jnp.concatenate([group[:1] - 1, group[:-1]])
    end = group != jnp.concatenate([group[1:], group[-1:] - 1])
    start_pos = jnp.where(start, idx, m_tiles)
    after = jnp.concatenate([start_pos[1:], jnp.full((1,), m_tiles, jnp.int32)])
    next_start = lax.cummin(after[::-1])[::-1]
    wraps = next_start >= m_tiles
    next_group = group[jnp.where(wraps, 0, next_start)]
    return jnp.stack([start, end, next_group, wraps]).astype(jnp.int32)


def _layer_group(layer, m_tiles):
    return jnp.full((m_tiles,), layer, jnp.int32)


def _ep_plain(dots, ex, n, m, rs):
    return dots[0]


def _ep_rope(dots, ex, n, m, rs):
    cos_ref, sa_ref, sb_ref = ex
    d = dots[0]
    cos, sa, sb = cos_ref[rs, :], sa_ref[rs, :], sb_ref[rs, :]
    quarter = HEAD_DIM // 4
    heads = []
    for h in range(d.shape[1] // HEAD_DIM):
        t = d[:, h * HEAD_DIM:(h + 1) * HEAD_DIM]
        heads.append(t * cos + pltpu.roll(t, HEAD_DIM - quarter, 1) * sa
                     + pltpu.roll(t, quarter, 1) * sb)
    return jnp.concatenate(heads, axis=1)


def _ep_merge(dots, ex, n, m, rs):
    ga_ref, gc_ref = ex
    return (_sigmoid(ga_ref[rs, :].astype(F32)) * dots[0]
            + _sigmoid(gc_ref[rs, :].astype(F32)) * dots[1])


def _ep_residual(dots, ex, n, m, rs, *, tm, tail_tiles=None):
    if tail_tiles is None:
        x_ref, gate_ref = ex
        x = x_ref[rs, :]
    else:
        x_ref, x_tail_ref, gate_ref = ex
        x = jnp.where(m < tail_tiles, x_ref[rs, :], x_tail_ref[rs, :])
    row = _mod_row(m * tm)
    return x + gate_ref[pl.ds(row, 1), :] * dots[0]


def _ep_swiglu(dots, ex, n, m, rs):
    a = dots[0]
    return (a * _sigmoid(a)) * dots[1]


def _rope_tables(tm):
    quarter = HEAD_DIM // 4
    inv_freq = np.power(np.float32(ROPE_THETA),
                        -np.arange(quarter, dtype=np.float32) / np.float32(quarter)).astype(np.float32)
    pos = np.arange(SEQ)
    rows = (pos // GRID_W).astype(np.float32)
    cols = (pos % GRID_W).astype(np.float32)
    ang = np.concatenate([rows[:, None] * inv_freq[None, :]] * 2
                         + [cols[:, None] * inv_freq[None, :]] * 2, axis=1).astype(np.float32)
    cos = np.cos(ang).astype(np.float32)
    sin = np.sin(ang).astype(np.float32)
    lane = np.arange(HEAD_DIM)
    first = ((lane // quarter) % 2 == 0)[None, :]
    sa = np.where(first, -sin, 0.0).astype(np.float32)
    sb = np.where(first, 0.0, sin).astype(np.float32)
    ones = np.ones((tm, HEAD_DIM), np.float32)
    zeros = np.zeros((tm, HEAD_DIM), np.float32)
    scale = np.float32(HEAD_DIM ** -0.5 * LOG2_E)

    def variants(tbl, ident):
        k = np.concatenate([tbl, ident], axis=0)
        ident_all = np.concatenate([np.broadcast_to(ident[:1], tbl.shape), ident], axis=0)
        return np.stack([k * scale, k, ident_all])

    return (jnp.asarray(variants(cos, ones)), jnp.asarray(variants(sa, zeros)),
            jnp.asarray(variants(sb, zeros)))


ATTN_ROWS = 2 * BLOCK
BAND = 3 * BLOCK


def _band_masks():
    r = np.arange(BLOCK)[:, None]
    c = np.arange(BAND)[None, :]
    masks = [(np.abs(off * BLOCK + r - c) <= WINDOW).astype(np.float32) for off in range(3)]
    return jnp.asarray(np.stack([np.tile(m, (GROUP, 1)) for m in masks]))


def _stack_heads(q):
    return jnp.concatenate([q[:, h * HEAD_DIM:(h + 1) * HEAD_DIM] for h in range(GROUP)], axis=0)


def _unstack_heads(o):
    return jnp.concatenate([o[h * BLOCK:(h + 1) * BLOCK, :] for h in range(GROUP)], axis=1)


def _sink_column(sink_ref, g):
    return LOG2_E * jnp.concatenate(
        [jnp.broadcast_to(sink_ref[pl.ds(g * GROUP + h, 1), 0:1], (BLOCK, 1)) for h in range(GROUP)],
        axis=0)


def _softmax_pv(s, sink, vals):
    mx = jnp.maximum(jnp.max(s, axis=-1, keepdims=True), sink)
    p = jnp.exp2(s - mx).astype(BF16)
    ones = jnp.ones((vals.shape[0], HEAD_DIM), BF16)
    r = jnp.dot(p, jnp.concatenate([vals, ones], axis=1), preferred_element_type=F32)
    denom = r[:, HEAD_DIM:HEAD_DIM + 1] + jnp.exp2(sink - mx)
    return r[:, :HEAD_DIM] / denom


_CONTRACT_LAST = (((1,), (1,)), ((), ()))


def _attn_latent_kernel(sink_ref, mask_ref, q_ref, k_ref, v_ref, kx_ref, vx_ref, o_ref, s_even, s_odd):
    n_blocks = SEQ // BLOCK
    sink = _sink_column(sink_ref, pl.program_id(1))

    def band_start(i):
        first = jnp.clip(i - 1, 0, n_blocks - 3)
        return first, pl.multiple_of(first * BLOCK, BLOCK)

    def scores(i, s_ref):
        first, start = band_start(i)
        qs = _stack_heads(q_ref[pl.ds(pl.multiple_of(i * BLOCK, BLOCK), BLOCK), :])
        s_loc = lax.dot_general(qs, k_ref[pl.ds(start, BAND), :], _CONTRACT_LAST,
                                preferred_element_type=F32)
        s_ref[:, 0:BAND] = jnp.where(mask_ref[i - first] > 0.5, s_loc, NEG_INF)
        s_ref[:, BAND:] = lax.dot_general(qs, kx_ref[...], _CONTRACT_LAST,
                                          preferred_element_type=F32)

    def finish(i, s_ref):
        _, start = band_start(i)
        vals = jnp.concatenate([v_ref[pl.ds(start, BAND), :], vx_ref[...]], axis=0)
        o = _softmax_pv(s_ref[...], sink, vals)
        o_ref[pl.ds(pl.multiple_of(i * BLOCK, BLOCK), BLOCK), :] = _unstack_heads(o).astype(o_ref.dtype)

    scores(0, s_even)

    def pair(t, carry):
        i = 2 * t
        scores(i + 1, s_odd)
        finish(i, s_even)
        scores(jnp.minimum(i + 2, n_blocks - 1), s_even)
        finish(i + 1, s_odd)
        return carry

    lax.fori_loop(0, n_blocks // 2, pair, 0)


def _attn_ctx_kernel(sink_ref, q_ref, kx_ref, vx_ref, o_ref):
    width = GROUP * HEAD_DIM
    for g in range(N_KV_HEADS):
        sink = _sink_column(sink_ref, g)
        kx = kx_ref[:, g * HEAD_DIM:(g + 1) * HEAD_DIM]
        vx = vx_ref[:, g * HEAD_DIM:(g + 1) * HEAD_DIM]
        for sub in range(CTX_LEN // BLOCK):
            rows = slice(sub * BLOCK, (sub + 1) * BLOCK)
            cols = slice(g * width, (g + 1) * width)
            qs = _stack_heads(q_ref[rows, cols])
            s = lax.dot_general(qs, kx, _CONTRACT_LAST, preferred_element_type=F32)
            o_ref[rows, cols] = _unstack_heads(_softmax_pv(s, sink, vx)).astype(o_ref.dtype)


def _attention(qkv, kv_ctx, sink_tbl, masks, *, with_ctx_queries, ctx_row_blk, ctx_k_col, ctx_v_col):
    k_col = Q_W // HEAD_DIM
    v_col = (Q_W + KV_W) // HEAD_DIM
    width = GROUP * HEAD_DIM
    sink_spec = pl.BlockSpec((N_HEADS, LANES), lambda b, g: (0, 0))
    kx_spec = pl.BlockSpec((CTX_LEN, HEAD_DIM), lambda b, g: (ctx_row_blk + b, ctx_k_col + g))
    vx_spec = pl.BlockSpec((CTX_LEN, HEAD_DIM), lambda b, g: (ctx_row_blk + b, ctx_v_col + g))
    y = pl.pallas_call(
        _attn_latent_kernel,
        grid=(BATCH, N_KV_HEADS),
        in_specs=[
            sink_spec,
            pl.BlockSpec((3, GROUP * BLOCK, BAND), lambda b, g: (0, 0, 0)),
            pl.BlockSpec((SEQ, width), lambda b, g: (b, g)),
            pl.BlockSpec((SEQ, HEAD_DIM), lambda b, g: (b, k_col + g)),
            pl.BlockSpec((SEQ, HEAD_DIM), lambda b, g: (b, v_col + g)),
            kx_spec, vx_spec,
        ],
        out_specs=pl.BlockSpec((SEQ, width), lambda b, g: (b, g)),
        out_shape=jax.ShapeDtypeStruct((N_LAT, Q_W), BF16),
        scratch_shapes=[pltpu.VMEM((GROUP * BLOCK, BAND + CTX_LEN), F32),
                        pltpu.VMEM((GROUP * BLOCK, BAND + CTX_LEN), F32)],
        compiler_params=_params(2),
        name="attention",
    )(sink_tbl, masks, qkv, qkv, qkv, kv_ctx, kv_ctx)
    if not with_ctx_queries:
        return y, None
    ctx_blk = N_LAT // CTX_LEN
    kv_blk = KV_W // HEAD_DIM
    y_ctx = pl.pallas_call(
        _attn_ctx_kernel,
        grid=(BATCH,),
        in_specs=[
            pl.BlockSpec((N_HEADS, LANES), lambda b: (0, 0)),
            pl.BlockSpec((CTX_LEN, Q_W), lambda b: (ctx_blk + b, 0)),
            pl.BlockSpec((CTX_LEN, KV_W), lambda b: (ctx_row_blk + b, ctx_k_col // kv_blk)),
            pl.BlockSpec((CTX_LEN, KV_W), lambda b: (ctx_row_blk + b, ctx_v_col // kv_blk)),
        ],
        out_specs=pl.BlockSpec((CTX_LEN, Q_W), lambda b: (b, 0)),
        out_shape=jax.ShapeDtypeStruct((N_CTX, Q_W), BF16),
        compiler_params=_params(1),
        name="attention_ctx",
    )(sink_tbl, qkv, kv_ctx, kv_ctx)
    return y, y_ctx


CONV_ROWS = CTX_LEN
CONV_ROWS_LATENT = 512
HALO = 16


def _conv_kernel(w_ref, u_ref, gb_ref, gc_ref, up_ref, gcp_ref, un_ref, gcn_ref, o_ref, *, rows):
    r = pl.program_id(0)
    lat_blocks = N_LAT // rows
    per_seq = SEQ // rows
    is_ctx = r >= lat_blocks
    seq_start = is_ctx | (r % per_seq == 0)
    seq_end = is_ctx | (r % per_seq == per_seq - 1)
    z = gc_ref[...].astype(F32) * u_ref[...].astype(F32)
    z_before = (gcp_ref[HALO - 1:HALO, :].astype(F32) * up_ref[HALO - 1:HALO, :].astype(F32))
    z_after = gcn_ref[0:1, :].astype(F32) * un_ref[0:1, :].astype(F32)
    z_before = jnp.where(seq_start, 0.0, z_before)
    z_after = jnp.where(seq_end, 0.0, z_after)
    row = lax.broadcasted_iota(jnp.int32, z.shape, 0)
    z_prev = jnp.where(row == 0, z_before, pltpu.roll(z, 1, 0))
    z_next = jnp.where(row == rows - 1, z_after, pltpu.roll(z, rows - 1, 0))
    y = z_prev * w_ref[0:1, :] + z * w_ref[1:2, :] + z_next * w_ref[2:3, :]
    o_ref[...] = (gb_ref[...].astype(F32) * y).astype(o_ref.dtype)


def _short_conv(rest, conv_w, layer, n_rows):
    rows = CONV_ROWS_LATENT if n_rows == N_LAT else CONV_ROWS
    n_blocks = n_rows // rows
    tc = D_MODEL
    halo_per_block = rows // HALO
    last_halo = n_rows // HALO - 1

    def main(chunk):
        return pl.BlockSpec((rows, tc), lambda r: (r, chunk))

    def before(chunk):
        return pl.BlockSpec((HALO, tc), lambda r: (jnp.maximum(r * halo_per_block - 1, 0), chunk))

    def after(chunk):
        return pl.BlockSpec((HALO, tc),
                            lambda r: (jnp.minimum((r + 1) * halo_per_block, last_halo), chunk))

    return pl.pallas_call(
        functools.partial(_conv_kernel, rows=rows),
        grid=(n_blocks,),
        in_specs=[
            pl.BlockSpec((None, 8, tc), lambda r: (layer, 0, 0)),
            main(0), main(1), main(2), before(0), before(2), after(0), after(2),
        ],
        out_specs=pl.BlockSpec((rows, tc), lambda r: (r, 0)),
        out_shape=jax.ShapeDtypeStruct((n_rows, D_MODEL), BF16),
        compiler_params=_params(1),
        name="short_conv",
    )(conv_w, rest, rest, rest, rest, rest, rest, rest)


ROW_CHUNKS = D_MODEL // LANES


def _norm_route_kernel(x_ref, g_ref, sh_ref, sc_ref, w_ref, chunks_ref, route_ref, *, tm):
    row = _mod_row(pl.program_id(0) * tm)
    xf = x_ref[...]
    inv = lax.rsqrt(jnp.mean(xf * xf, axis=-1, keepdims=True) + RMS_EPS)
    h = (xf * inv) * g_ref[...]
    h = h * (1.0 + sc_ref[pl.ds(row, 1), :]) + sh_ref[pl.ds(row, 1), :]
    for j in range(ROW_CHUNKS):
        chunks_ref[pl.ds(j, tm, stride=ROW_CHUNKS), :] = h[:, j * LANES:(j + 1) * LANES]
    route_ref[...] = _route(h, w_ref[...])


def _norm_route(x_rows, gain, mod, layer, shift_chunk, scale_chunk, w_router_padded, n_rows):
    tm = 512
    return pl.pallas_call(
        functools.partial(_norm_route_kernel, tm=tm),
        grid=(n_rows // tm,),
        in_specs=[
            pl.BlockSpec((tm, D_MODEL), lambda m: (m, 0)),
            pl.BlockSpec((None, 1, D_MODEL), lambda m: (layer, 0, 0)),
            pl.BlockSpec((None, MOD_ROWS, D_MODEL), lambda m: (layer, 0, shift_chunk)),
            pl.BlockSpec((None, MOD_ROWS, D_MODEL), lambda m: (layer, 0, scale_chunk)),
            pl.BlockSpec((D_MODEL, LANES), lambda m: (0, 0)),
        ],
        out_specs=[pl.BlockSpec((tm * ROW_CHUNKS, LANES), lambda m: (m, 0)),
                   pl.BlockSpec((tm, LANES), lambda m: (m, 0))],
        out_shape=[jax.ShapeDtypeStruct((n_rows * ROW_CHUNKS, LANES), F32),
                   jax.ShapeDtypeStruct((n_rows, LANES), F32)],
        compiler_params=_params(1),
        name="norm_route",
    )(x_rows, gain.reshape(DEPTH, 1, D_MODEL), mod, mod, w_router_padded)


def _route(h, w):
    h_hi = h.astype(BF16)
    h_lo = (h - h_hi.astype(F32)).astype(BF16)
    w_hi = w.astype(BF16)
    w_lo = (w - w_hi.astype(F32)).astype(BF16)
    logits = (jnp.dot(h_hi, w_hi, preferred_element_type=F32)
              + jnp.dot(h_hi, w_lo, preferred_element_type=F32)
              + jnp.dot(h_lo, w_hi, preferred_element_type=F32))
    lane = lax.broadcasted_iota(jnp.int32, logits.shape, 1)
    lowest = float(jnp.finfo(F32).min)
    logits = jnp.where(lane < N_EXPERTS, logits, lowest)
    v1 = jnp.max(logits, axis=-1, keepdims=True)
    i1 = jnp.min(jnp.where(logits == v1, lane, LANES), axis=-1, keepdims=True)
    rest = jnp.where(lane == i1, lowest, logits)
    v2 = jnp.max(rest, axis=-1, keepdims=True)
    i2 = jnp.min(jnp.where(rest == v2, lane, LANES), axis=-1, keepdims=True)
    e2 = jnp.exp(v2 - v1)
    p1 = 1.0 / (1.0 + e2)
    p2 = e2 / (1.0 + e2)
    return jnp.where(lane == 0, i1.astype(F32),
                     jnp.where(lane == 1, i2.astype(F32),
                               jnp.where(lane == 2, p1, jnp.where(lane == 3, p2, 0.0))))


DISPATCH_TILE = 512


DMA_UNROLL = 8


def _dispatch_kernel(nused_ref, tok_ref, tok_next_ref, h_hbm, o_ref, buf, sem):
    t = pl.program_id(0)
    slot = t % 2
    n_active = nused_ref[0] * (MOE_TILE // DISPATCH_TILE)
    tile_chunks = DISPATCH_TILE * ROW_CHUNKS

    def gather(idx_ref, dst_slot):
        def body(i, carry):
            for u in range(DMA_UNROLL):
                j = i * DMA_UNROLL + u
                src = pl.multiple_of(idx_ref[0, j] * ROW_CHUNKS, ROW_CHUNKS)
                pltpu.make_async_copy(
                    h_hbm.at[pl.ds(src, ROW_CHUNKS), :],
                    buf.at[dst_slot, pl.ds(j * ROW_CHUNKS, ROW_CHUNKS), :],
                    sem.at[dst_slot]).start(priority=u % 2)
            return carry
        lax.fori_loop(0, DISPATCH_TILE // DMA_UNROLL, body, 0)

    @pl.when((t == 0) & (n_active > 0))
    def _():
        gather(tok_ref, 0)

    @pl.when(t + 1 < n_active)
    def _():
        gather(tok_next_ref, 1 - slot)

    @pl.when(t < n_active)
    def _():
        pltpu.make_async_copy(h_hbm.at[pl.ds(0, tile_chunks), :], buf.at[slot], sem.at[slot]).wait()
        cols = [buf[slot, pl.ds(j, DISPATCH_TILE, stride=ROW_CHUNKS), :] for j in range(ROW_CHUNKS)]
        o_ref[...] = jnp.concatenate(cols, axis=1).astype(o_ref.dtype)

    @pl.when(t >= n_active)
    def _():
        o_ref[...] = jnp.zeros_like(o_ref)


def _dispatch(h_chunks, tok_of_row, n_used):
    n_rows = tok_of_row.shape[0]
    n_steps = n_rows // DISPATCH_TILE
    tok_blocks = tok_of_row.reshape(n_steps, 1, DISPATCH_TILE)
    grid_spec = pltpu.PrefetchScalarGridSpec(
        num_scalar_prefetch=1,
        grid=(n_steps,),
        in_specs=[
            pl.BlockSpec((None, 1, DISPATCH_TILE), lambda t, u: (t, 0, 0), memory_space=pltpu.SMEM),
            pl.BlockSpec((None, 1, DISPATCH_TILE),
                         lambda t, u: (jnp.minimum(t + 1, n_steps - 1), 0, 0),
                         memory_space=pltpu.SMEM),
            pl.BlockSpec(memory_space=pl.ANY),
        ],
        out_specs=pl.BlockSpec((DISPATCH_TILE, D_MODEL), lambda t, u: (t, 0)),
        scratch_shapes=[pltpu.VMEM((2, DISPATCH_TILE * ROW_CHUNKS, LANES), F32),
                        pltpu.SemaphoreType.DMA((2,))],
    )
    return pl.pallas_call(
        _dispatch_kernel,
        grid_spec=grid_spec,
        out_shape=jax.ShapeDtypeStruct((n_rows, D_MODEL), BF16),
        compiler_params=_params(1),
        name="moe_dispatch",
    )(n_used, tok_blocks, tok_blocks, h_chunks)


COMBINE_TILE = 512


def _combine_kernel(pos_ref, pos_next_ref, y_hbm, x_ref, route_ref, gate_ref, gain_ref, o_ref,
                    buf, sem):
    t = pl.program_id(0)
    n_steps = pl.num_programs(0)
    slot = t % 2

    def gather(idx_ref, dst_slot):
        def body(i, carry):
            for u in range(DMA_UNROLL):
                j = i * DMA_UNROLL + u
                for k in range(TOP_K):
                    pltpu.make_async_copy(
                        y_hbm.at[pl.ds(idx_ref[0, TOP_K * j + k], 1), :],
                        buf.at[dst_slot, k, pl.ds(j, 1), :],
                        sem.at[dst_slot]).start(priority=k)
            return carry
        lax.fori_loop(0, COMBINE_TILE // DMA_UNROLL, body, 0)

    @pl.when(t == 0)
    def _():
        gather(pos_ref, 0)

    @pl.when(t + 1 < n_steps)
    def _():
        gather(pos_next_ref, 1 - slot)

    for k in range(TOP_K):
        pltpu.make_async_copy(
            y_hbm.at[pl.ds(0, COMBINE_TILE), :], buf.at[slot, k], sem.at[slot]).wait()
    route = route_ref[...]
    moe = route[:, 2:3] * buf[slot, 0] + route[:, 3:4] * buf[slot, 1]
    row = _mod_row(t * COMBINE_TILE)
    xn = x_ref[...] + gate_ref[pl.ds(row, 1), :] * moe
    inv = lax.rsqrt(jnp.mean(xn * xn, axis=-1, keepdims=True) + RMS_EPS)
    o_ref[...] = (xn * inv) * gain_ref[...]


def _combine(y_rows, pos, x_rows, route, mod, layer, gate_chunk, norm_f):
    n_steps = N_LAT // COMBINE_TILE
    pos_blocks = pos.reshape(n_steps, 1, TOP_K * COMBINE_TILE)
    idx_shape = (None, 1, TOP_K * COMBINE_TILE)
    return pl.pallas_call(
        _combine_kernel,
        grid=(n_steps,),
        in_specs=[
            pl.BlockSpec(idx_shape, lambda t: (t, 0, 0), memory_space=pltpu.SMEM),
            pl.BlockSpec(idx_shape, lambda t: (jnp.minimum(t + 1, n_steps - 1), 0, 0),
                         memory_space=pltpu.SMEM),
            pl.BlockSpec(memory_space=pl.ANY),
            pl.BlockSpec((COMBINE_TILE, D_MODEL), lambda t: (t, 0)),
            pl.BlockSpec((COMBINE_TILE, LANES), lambda t: (t, 0)),
            pl.BlockSpec((None, MOD_ROWS, D_MODEL), lambda t: (layer, 0, gate_chunk)),
            pl.BlockSpec((1, D_MODEL), lambda t: (0, 0)),
        ],
        out_specs=pl.BlockSpec((COMBINE_TILE, D_MODEL), lambda t: (t, 0)),
        out_shape=jax.ShapeDtypeStruct((N_LAT, D_MODEL), F32),
        scratch_shapes=[pltpu.VMEM((2, TOP_K, COMBINE_TILE, D_MODEL), F32),
                        pltpu.SemaphoreType.DMA((2,))],
        compiler_params=_params(1),
        name="moe_combine",
    )(pos_blocks, pos_blocks, y_rows, x_rows, route, mod, norm_f.reshape(1, D_MODEL))


def _routing_tables(route):
    experts = route[:, :TOP_K].astype(jnp.int32).reshape(-1)
    onehot = (experts[:, None] == jnp.arange(N_EXPERTS, dtype=jnp.int32)[None, :]).astype(jnp.int32)
    rank = jnp.sum((jnp.cumsum(onehot, axis=0) - onehot) * onehot, axis=1)
    counts = jnp.sum(onehot, axis=0)
    padded = ((counts + MOE_TILE - 1) // MOE_TILE) * MOE_TILE
    ends = jnp.cumsum(padded)
    starts = ends - counts
    pos = (jnp.sum(onehot * starts[None, :], axis=1) + rank).astype(jnp.int32)
    tok_of_row = jnp.zeros((MOE_ROWS,), jnp.int32).at[pos].set(
        jnp.arange(TOP_K * N_LAT, dtype=jnp.int32) // TOP_K)
    n_tiles = MOE_ROWS // MOE_TILE
    n_used = (ends[-1] // MOE_TILE).astype(jnp.int32)
    tile_start = jnp.arange(n_tiles, dtype=jnp.int32) * MOE_TILE
    tile_expert = jnp.sum((tile_start[:, None] >= ends[None, :]).astype(jnp.int32), axis=1)
    last_expert = jnp.sum((tile_start[jnp.maximum(n_used - 1, 0)] >= ends).astype(jnp.int32))
    in_use = jnp.arange(n_tiles) < n_used
    tile_expert = jnp.where(in_use, tile_expert, last_expert)
    tile_expert = jnp.minimum(tile_expert, N_EXPERTS - 1).astype(jnp.int32)
    tile_first = jnp.clip(starts[tile_expert] - tile_start, 0, MOE_TILE)
    tile_first = jnp.where(in_use, tile_first, MOE_TILE).astype(jnp.int32)
    return pos, tok_of_row, tile_expert, tile_first, n_used.reshape(1)


def kernel(x, c, ctx, c_ctx, w_mod, b_mod, norm1, w_in, sink, conv_w, w_o_attn, w_o_conv, w_out,
           norm2, ffn_w1, ffn_w3, ffn_w2, router, moe_w1, moe_w3, moe_w2, norm_f):
    tm = ROW_TILE
    cvec = jnp.concatenate(
        [c, c_ctx[None, :], jnp.zeros((MOD_ROWS - BATCH - 1, D_MODEL), F32)], axis=0)
    mod = _modulation(cvec, w_mod, b_mod)
    x_main, x_tail = x.reshape(N_LAT, D_MODEL), ctx.reshape(N_CTX, D_MODEL)
    cos_t, sa_t, sb_t = _rope_tables(tm)
    conv_w8 = jnp.pad(conv_w, ((0, 0), (0, 8 - conv_w.shape[1]), (0, 0)))
    sink_tbl = jnp.broadcast_to(sink[:, :, None], (DEPTH, N_HEADS, LANES))
    masks = _band_masks()
    seq_tiles = SEQ // tm
    lat_tiles = N_LAT // tm
    out = None

    for layer in range(DEPTH):
        ctx_out = layer < DEPTH - 1
        n_rows = N_ALL if ctx_out else N_LAT
        m_tiles = n_rows // tm
        grp = _layer_group(layer, m_tiles)
        chunk = D_MODEL // 512

        h = _norm_mod(x_main, norm1, mod, layer, 0, 1, N_ALL, BF16, x_tail=x_tail)
        tn_qkv = 512

        def variant(n):
            return jnp.where(n < Q_W // tn_qkv, 0, jnp.where(n < (Q_W + KV_W) // tn_qkv, 1, 2))

        def tbl_map(n, m):
            return (variant(n), jnp.where(m < lat_tiles, m % seq_tiles, seq_tiles), 0)

        tbl_shape = (None, tm, HEAD_DIM)
        qkv = _ws_matmul(
            [h], [w_in], pairs=[(0, 0)], epilogue=_ep_rope,
            extras=[(cos_t, tbl_shape, tbl_map), (sa_t, tbl_shape, tbl_map),
                    (sb_t, tbl_shape, tbl_map)],
            group=grp, m_tiles=m_tiles, tm=tm, tn=tn_qkv, n_out=QKV_W, n_sub=2,
            out_dtype=BF16, name="in_proj_qkv")
        rest = _ws_matmul(
            [h], [w_in], pairs=[(0, 0)], epilogue=_ep_plain,
            group=grp, m_tiles=m_tiles, tm=tm, tn=1024, n_out=REST_W,
            col_off=QKV_W // 1024, out_dtype=BF16, name="in_proj_rest")
        if ctx_out:
            kv_ctx, ctx_row_blk = qkv, N_LAT // CTX_LEN
            ctx_k_col, ctx_v_col = Q_W // HEAD_DIM, (Q_W + KV_W) // HEAD_DIM
        else:
            kv_ctx = _ws_matmul(
                [h], [w_in], pairs=[(0, 0)], epilogue=_ep_plain,
                group=_layer_group(layer, N_CTX // tm), m_tiles=N_CTX // tm, tm=tm, tn=512,
                n_out=2 * KV_W, col_off=Q_W // 512, lhs_row_off=lat_tiles, out_dtype=BF16,
                name="in_proj_ctx_kv")
            ctx_row_blk, ctx_k_col, ctx_v_col = 0, 0, KV_W // HEAD_DIM
        y_attn, y_attn_ctx = _attention(
            qkv, kv_ctx, sink_tbl[layer], masks, with_ctx_queries=ctx_out,
            ctx_row_blk=ctx_row_blk, ctx_k_col=ctx_k_col, ctx_v_col=ctx_v_col)
        y_conv = _short_conv(rest, conv_w8, layer, n_rows)
        tn = 512
        gate_blk = D_MODEL // tn
        merged = _ws_matmul(
            [y_attn, y_conv], [w_o_attn, w_o_conv], pairs=[(0, 0), (1, 1)], epilogue=_ep_merge,
            extras=[(rest, (tm, tn), lambda n, m: (m, 3 * gate_blk + n)),
                    (rest, (tm, tn), lambda n, m: (m, 4 * gate_blk + n))],
            group=grp, m_tiles=m_tiles, tm=tm, tn=tn, n_out=D_MODEL,
            lhs0_tail=y_attn_ctx, out_dtype=BF16, name="merge")
        gate_spec = (mod, (None, MOD_ROWS, tn), lambda n, m: (layer, 0, 2 * chunk + n))
        if x_tail is None:
            res_extras = [(x_main, (tm, tn), lambda n, m: (m, n)), gate_spec]
            res_tail = None
        else:
            res_tail = x_main.shape[0] // tm
            res_extras = [
                (x_main, (tm, tn), lambda n, m: (jnp.minimum(m, res_tail - 1), n)),
                (x_tail, (tm, tn), lambda n, m: (jnp.maximum(m - res_tail, 0), n)),
                gate_spec]
        x_mid = _ws_matmul(
            [merged], [w_out], pairs=[(0, 0)],
            epilogue=functools.partial(_ep_residual, tm=tm, tail_tiles=res_tail),
            extras=res_extras, group=grp, m_tiles=m_tiles, tm=tm, tn=tn, n_out=D_MODEL,
            out_dtype=F32, name="out_proj")

        if layer % 2 == 0:
            i = layer // 2
            h2 = _norm_mod(x_mid, norm2, mod, layer, 3, 4, n_rows, BF16)
            hidden = _ws_matmul(
                [h2], [ffn_w1, ffn_w3], pairs=[(0, 0), (0, 1)], epilogue=_ep_swiglu,
                group=_layer_group(i, m_tiles), m_tiles=m_tiles, tm=tm, tn=512, n_out=D_FF,
                out_dtype=BF16, name="ffn_up")
            tm2, tn2 = 512, 512
            x_main, x_tail = _ws_matmul(
                [hidden], [ffn_w2], pairs=[(0, 0)],
                epilogue=functools.partial(_ep_residual, tm=tm2),
                extras=[(x_mid, (tm2, tn2), lambda n, m: (m, n)),
                        (mod, (None, MOD_ROWS, tn2),
                         lambda n, m: (layer, 0, 5 * (D_MODEL // tn2) + n))],
                group=_layer_group(i, n_rows // tm2), m_tiles=n_rows // tm2, tm=tm2, tn=tn2,
                n_out=D_MODEL, out_dtype=F32, name="ffn_down"), None
        else:
            i = layer // 2
            w_router = jnp.pad(router[i], ((0, 0), (0, LANES - N_EXPERTS)))
            h2_chunks, route = _norm_route(x_mid, norm2, mod, layer, 3, 4, w_router, n_rows)
            pos, tok_of_row, tile_expert, tile_first, n_used = _routing_tables(route)
            rows = _dispatch(h2_chunks, tok_of_row, n_used)
            moe_tiles = MOE_ROWS // MOE_TILE
            n_stack = moe_w1.shape[0] * N_EXPERTS
            w1s = moe_w1.reshape(n_stack, D_MODEL, D_FF_EXPERT)
            w3s = moe_w3.reshape(n_stack, D_MODEL, D_FF_EXPERT)
            w2s = moe_w2.reshape(n_stack, D_FF_EXPERT, D_MODEL)
            tile_group = tile_expert + i * N_EXPERTS
            hidden = _ws_matmul(
                [rows], [w1s, w3s], pairs=[(0, 0), (0, 1)], epilogue=_ep_swiglu,
                group=tile_group, tile_first=tile_first, m_tiles=moe_tiles,
                tm=MOE_TILE, tn=1024, n_out=D_FF_EXPERT, out_dtype=BF16, name="moe_up")
            y_rows = _ws_matmul(
                [hidden], [w2s], pairs=[(0, 0)], epilogue=_ep_plain,
                group=tile_group, tile_first=tile_first, m_tiles=moe_tiles,
                tm=MOE_TILE, tn=512, n_out=D_MODEL, out_dtype=F32, name="moe_down")
            out = _combine(y_rows, pos, x_mid, route, mod, layer, 5, norm_f)

    return out.reshape(BATCH, SEQ, D_MODEL)
```
